```python
import math
import jax, jax.numpy as jnp
from jax import lax
import numpy as np

D_MODEL = 1024
BATCH = 8
SEQ = 2048
DEPTH = 4
DEC_BATCH = 32
DEC_SEQ = 1
PAST_LEN = 16384
PAGE_SIZE = 128

N_BRANCH = 4
BRANCH_WIDTH = 256
EPS = 1e-6
NEG = -1e30
LB_FLOOR = 1e-20
Q_BLOCK = 128
NSA_HEADS = 4
NSA_DK = 64
NSA_BLOCK = 64
NSA_TOPK = 16
NSA_WINDOW = 512
NSA_Q_BLOCK = 64
MLA_HEADS = 4
MLA_Q_LORA = 256
MLA_KV_LORA = 128
MLA_NOPE = 64
MLA_ROPE = 32
MLA_V = 64
ROPE_THETA = 10000.0
SSD_HEADS = 4
SSD_HEAD_DIM = 64
SSD_GROUPS = 2
SSD_STATE = 64
SSD_CONV = 4
SSD_CHUNK = 128
SSD_WIDTH = SSD_HEADS * SSD_HEAD_DIM
SSD_CONV_DIM = SSD_WIDTH + 2 * SSD_GROUPS * SSD_STATE
HG_HEADS = 4
HG_KDIM = 64
HG_VDIM = 64
HG_CHUNK = 32
D_FF = 2816

IN_SIZES = (NSA_HEADS * NSA_DK, 6 * NSA_DK, 3 * NSA_HEADS,
            MLA_Q_LORA, MLA_KV_LORA, MLA_ROPE,
            SSD_WIDTH, SSD_CONV_DIM, SSD_HEADS,
            HG_HEADS * HG_KDIM, HG_HEADS * HG_KDIM, HG_HEADS * HG_VDIM, HG_HEADS * HG_VDIM,
            N_BRANCH * D_MODEL)
IN_WIDTH = sum(IN_SIZES)

kernel_name = 'hybrid_nsa_mla_ssd_hgrn2_decode_step'


def rmsnorm(x, g):
    xf = x.astype(jnp.float32)
    y = xf * lax.rsqrt(jnp.mean(xf * xf, axis=-1, keepdims=True) + EPS)
    return (y * g.astype(jnp.float32)).astype(x.dtype)


def swiglu(x, w_in, w_out):
    gate, up = jnp.split(x @ w_in, 2, axis=-1)
    return (jax.nn.silu(gate) * up) @ w_out


def masked_softmax(s, mask):
    s = jnp.where(mask, s.astype(jnp.float32), NEG)
    return jnp.where(mask, jax.nn.softmax(s, axis=-1), 0.0)


def rope(x, pos):
    half = x.shape[-1] // 2
    freq = ROPE_THETA ** (-jnp.arange(half, dtype=jnp.float32) / half)
    ang = pos.astype(jnp.float32)[:, None] * freq
    ang = ang.reshape((1, pos.shape[0]) + (1,) * (x.ndim - 3) + (half,))
    cos, sin = jnp.cos(ang), jnp.sin(ang)
    xf = x.astype(jnp.float32)
    x1, x2 = xf[..., :half], xf[..., half:]
    return jnp.concatenate([x1 * cos - x2 * sin, x2 * cos + x1 * sin], axis=-1).astype(x.dtype)


def to_chunks(a, c):
    T = a.shape[1]
    n = -(-T // c)
    a = jnp.pad(a, [(0, 0), (0, n * c - T)] + [(0, 0)] * (a.ndim - 2))
    return jnp.moveaxis(a.reshape((a.shape[0], n, c) + a.shape[2:]), 1, 0)


def from_chunks(a, T):
    a = jnp.moveaxis(a, 0, 1)
    return a.reshape((a.shape[0], a.shape[1] * a.shape[2]) + a.shape[3:])[:, :T]


def over_query_blocks(fn, qb, q_arrays, q_pos):
    T = q_pos.shape[0]
    qb = min(qb, T)
    n = -(-T // qb)
    pos_b = jnp.pad(q_pos, (0, n * qb - T), mode='edge').reshape(n, qb)
    blocks = tuple(to_chunks(a, qb) for a in q_arrays)
    outs = lax.map(lambda a: fn(a[0], a[1]), (blocks, pos_b))
    return jax.tree_util.tree_map(lambda o: from_chunks(o, T), outs)


def nsa_cmp_sel(q, kc, vc, ks, vs, q_pos):
    B, L, dk = kc.shape
    H = q.shape[2]
    nb = -(-L // NSA_BLOCK)
    padL = nb * NSA_BLOCK - L

    def blocks(a):
        return jnp.pad(a, ((0, 0), (0, padL), (0, 0))).reshape(B, nb, NSA_BLOCK, dk)

    kc_m = blocks(kc).astype(jnp.float32).mean(axis=2).astype(kc.dtype)
    vc_m = blocks(vc).astype(jnp.float32).mean(axis=2).astype(vc.dtype)
    ks_b, vs_b = blocks(ks), blocks(vs)
    k_top = min(NSA_TOPK, nb)
    blk = jnp.arange(nb)
    tok = jnp.arange(NSA_BLOCK)
    bidx = jnp.arange(B)[:, None, None]
    scale = dk ** -0.5

    def fn(qs, pos):
        (qq,) = qs
        qb = pos.shape[0]
        s = jnp.einsum('bqhd,bnd->bhqn', qq, kc_m) * scale
        cmask = ((blk + 1) * NSA_BLOCK - 1)[None, :] <= pos[:, None]
        p = masked_softmax(s, cmask[None, None])
        o_c = jnp.einsum('bhqn,bnd->bqhd', p.astype(vc_m.dtype), vc_m)
        cur = pos // NSA_BLOCK
        forced = (blk[None, :] == 0) | (blk[None, :] == cur[:, None]) | (blk[None, :] == cur[:, None] - 1)
        imp = p.sum(axis=1) + jnp.where(forced, float(NSA_HEADS + 1), 0.0)[None]
        imp = jnp.where((blk[None, :] > cur[:, None])[None], -1.0, imp)
        _, idx = lax.top_k(imp, k_top)
        ksel = ks_b[bidx, idx]
        vsel = vs_b[bidx, idx].reshape(B, qb, k_top * NSA_BLOCK, dk)
        tpos = idx[..., None] * NSA_BLOCK + tok
        smask = (tpos <= pos[None, :, None, None]).reshape(B, 1, qb, k_top * NSA_BLOCK)
        s2 = jnp.einsum('bqhd,bqkjd->bhqkj', qq, ksel).reshape(B, H, qb, k_top * NSA_BLOCK) * scale
        p2 = masked_softmax(s2, smask)
        o_s = jnp.einsum('bhqm,bqmd->bqhd', p2.astype(vsel.dtype), vsel)
        return (o_c, o_s)

    return over_query_blocks(fn, NSA_Q_BLOCK, (q,), q_pos)


def nsa_window(q, kw, vw, q_pos, past_len, n_buf):
    T = q_pos.shape[0]
    qb = min(Q_BLOCK, T)
    front = NSA_WINDOW - n_buf
    kp = jnp.pad(kw, ((0, 0), (front, qb), (0, 0)))
    vp = jnp.pad(vw, ((0, 0), (front, qb), (0, 0)))
    band = NSA_WINDOW + qb
    off = jnp.arange(band)
    scale = kw.shape[-1] ** -0.5

    def fn(qs, pos):
        (qq,) = qs
        start = pos[0] - past_len
        kb = lax.dynamic_slice_in_dim(kp, start, band, axis=1)
        vb = lax.dynamic_slice_in_dim(vp, start, band, axis=1)
        kpos = past_len - NSA_WINDOW + start + off
        mask = ((kpos[None, :] <= pos[:, None]) & (kpos[None, :] >= pos[:, None] - NSA_WINDOW)
                & (kpos[None, :] >= past_len - n_buf))
        s = jnp.einsum('bqhd,bkd->bhqk', qq, kb) * scale
        p = masked_softmax(s, mask[None, None])
        return jnp.einsum('bhqk,bkd->bqhd', p.astype(vb.dtype), vb)

    return over_query_blocks(fn, qb, (q,), q_pos)


def mla_attend(q_lat, q_pe, ckv, kpe, q_pos):
    kpos = jnp.arange(ckv.shape[1])
    scale = (MLA_NOPE + MLA_ROPE) ** -0.5

    def fn(qs, pos):
        ql, qp = qs
        s = (jnp.einsum('bqhr,bkr->bhqk', ql, ckv) + jnp.einsum('bqhp,bkp->bhqk', qp, kpe)) * scale
        p = masked_softmax(s, (kpos[None, :] <= pos[:, None])[None, None])
        return jnp.einsum('bhqk,bkr->bqhr', p.astype(ckv.dtype), ckv)

    return over_query_blocks(fn, Q_BLOCK, (q_lat, q_pe), q_pos)


def ssd_scan(x, dt, a, bm, cm, s0):
    T = x.shape[1]
    c = min(SSD_CHUNK, T)
    tri = jnp.tril(jnp.ones((c, c), dtype=bool))[None, :, :, None]

    def step(s, inp):
        xc, ac, bc, cc = inp
        cum = jnp.cumsum(ac, axis=1)
        decay = jnp.exp(jnp.where(tri, cum[:, :, None] - cum[:, None], NEG))
        scores = jnp.einsum('bthn,bshn->btsh', cc, bc) * decay
        y = (jnp.einsum('btsh,bshp->bthp', scores, xc)
             + jnp.einsum('bthn,bhpn->bthp', cc, s) * jnp.exp(cum)[..., None])
        last = cum[:, -1]
        w = jnp.exp(last[:, None] - cum)[..., None]
        s = s * jnp.exp(last)[:, :, None, None] + jnp.einsum('bshn,bshp->bhpn', bc * w, xc)
        return s, y

    s, ys = lax.scan(step, s0, (to_chunks(x * dt[..., None], c), to_chunks(a, c),
                                to_chunks(bm, c), to_chunks(cm, c)))
    return from_chunks(ys, T), s


def gla_scan(q, k, v, g, s0):
    T = q.shape[1]
    c = min(HG_CHUNK, T)
    tri = jnp.tril(jnp.ones((c, c), dtype=bool))[None, :, :, None, None]

    def step(s, inp):
        qc, kc, vc, gc = inp
        G = jnp.cumsum(gc, axis=1)
        dec = jnp.exp(jnp.where(tri, G[:, :, None] - G[:, None], NEG))
        att = jnp.einsum('bthk,bshk,btshk->btsh', qc, kc, dec)
        y = jnp.einsum('btsh,bshv->bthv', att, vc) + jnp.einsum('bthk,bhkv->bthv', qc * jnp.exp(G), s)
        last = G[:, -1]
        s = s * jnp.exp(last)[..., None] + jnp.einsum('bshk,bshv->bhkv', kc * jnp.exp(last[:, None] - G), vc)
        return s, y

    s, ys = lax.scan(step, s0, (to_chunks(q, c), to_chunks(k, c), to_chunks(v, c), to_chunks(g, c)))
    return from_chunks(ys, T), s


def token_mixers(h, past_len, n_keep, past, lw):
    (w_in_l, mqn, mkvn, wuq, wukv, cw, cb, dtb, alog, dskip, snorm, lb, hnorm, wbr, wout) = lw
    past_nsa, past_mla, win_buf, ssd_s0, conv_buf, hg_s0 = past
    B, T, _ = h.shape
    f32 = jnp.float32
    pos = past_len + jnp.arange(T, dtype=jnp.int32)
    offs = np.cumsum(IN_SIZES)[:-1].tolist()
    (nsa_q, nsa_kv, nsa_g, mla_cq, mla_ckv, mla_kpe, ssd_z, ssd_xbc, ssd_dt,
     hg_q, hg_f, hg_i, hg_g, merge_g) = jnp.split(h @ w_in_l, offs, axis=-1)

    q = nsa_q.reshape(B, T, NSA_HEADS, NSA_DK)
    kv_new = nsa_kv.reshape(B, T, 6, NSA_DK)
    paged_new = kv_new[:, :, :4]
    full = jnp.concatenate([past_nsa, paged_new], axis=1)
    o_c, o_s = nsa_cmp_sel(q, full[:, :, 0], full[:, :, 1], full[:, :, 2], full[:, :, 3], pos)
    wfull = jnp.concatenate([win_buf, kv_new[:, :, 4:]], axis=1)
    o_w = nsa_window(q, wfull[:, :, 0], wfull[:, :, 1], pos, past_len, win_buf.shape[1])
    gts = jax.nn.sigmoid(nsa_g.reshape(B, T, NSA_HEADS, 3))
    o_nsa = (gts[..., 0:1] * o_c + gts[..., 1:2] * o_s + gts[..., 2:3] * o_w).reshape(B, T, -1)
    new_win = wfull[:, wfull.shape[1] - n_keep:]

    qf = jnp.einsum('btr,rhe->bthe', rmsnorm(mla_cq, mqn), wuq)
    q_nope, q_pe = qf[..., :MLA_NOPE], rope(qf[..., MLA_NOPE:], pos)
    mla_new = jnp.concatenate([rmsnorm(mla_ckv, mkvn), rope(mla_kpe, pos)], axis=-1)
    mla_full = jnp.concatenate([past_mla, mla_new], axis=1)
    q_lat = jnp.einsum('bthn,rhn->bthr', q_nope, wukv[..., :MLA_NOPE])
    o_lat = mla_attend(q_lat, q_pe, mla_full[..., :MLA_KV_LORA], mla_full[..., MLA_KV_LORA:], pos)
    o_mla = jnp.einsum('bthr,rhv->bthv', o_lat, wukv[..., MLA_NOPE:]).reshape(B, T, -1)

    cfull = jnp.concatenate([conv_buf.astype(ssd_xbc.dtype), ssd_xbc], axis=1)
    conv = lax.conv_general_dilated(cfull, cw[:, None, :].astype(cfull.dtype), window_strides=(1,),
                                    padding='VALID', dimension_numbers=('NWC', 'WIO', 'NWC'),
                                    feature_group_count=SSD_CONV_DIM) + cb
    xbc = jax.nn.silu(conv).astype(f32)
    new_conv = cfull[:, cfull.shape[1] - (SSD_CONV - 1):]
    xs = xbc[..., :SSD_WIDTH].reshape(B, T, SSD_HEADS, SSD_HEAD_DIM)
    rep = SSD_HEADS // SSD_GROUPS
    bm = jnp.repeat(xbc[..., SSD_WIDTH:SSD_WIDTH + SSD_GROUPS * SSD_STATE].reshape(B, T, SSD_GROUPS, SSD_STATE), rep, axis=2)
    cm = jnp.repeat(xbc[..., SSD_WIDTH + SSD_GROUPS * SSD_STATE:].reshape(B, T, SSD_GROUPS, SSD_STATE), rep, axis=2)
    dt = jax.nn.softplus(ssd_dt.astype(f32) + dtb.astype(f32))
    A = -jnp.exp(alog.astype(f32))
    y, ssd_s = ssd_scan(xs, dt, dt * A, bm, cm, ssd_s0.astype(f32))
    y = (y + dskip.astype(f32)[:, None] * xs).reshape(B, T, SSD_WIDTH)
    o_ssd = rmsnorm(y * jax.nn.silu(ssd_z.astype(f32)), snorm).astype(h.dtype)

    lbh = lb.reshape(HG_HEADS, HG_KDIM)
    fr = hg_f.astype(f32).reshape(B, T, HG_HEADS, HG_KDIM)
    log_f = jnp.logaddexp(jnp.log(jnp.maximum(lbh, LB_FLOOR)), jnp.log1p(-lbh) + jax.nn.log_sigmoid(fr))
    k_in = (1.0 - lbh) * jax.nn.sigmoid(-fr)
    o, hg_s = gla_scan(hg_q.astype(f32).reshape(B, T, HG_HEADS, HG_KDIM), k_in,
                       hg_i.astype(f32).reshape(B, T, HG_HEADS, HG_VDIM), log_f, hg_s0.astype(f32))
    o = rmsnorm(o, hnorm.reshape(HG_HEADS, HG_VDIM)).reshape(B, T, -1)
    o_hg = (o * jax.nn.silu(hg_g.astype(f32))).astype(h.dtype)

    br = jnp.stack([o_nsa, o_mla, o_ssd, o_hg], axis=2)
    u = jnp.einsum('btkc,kcd->btkd', br, wbr)
    gate = jax.nn.sigmoid(merge_g.reshape(B, T, N_BRANCH, D_MODEL))
    out = jnp.sum(gate * u, axis=2) @ wout
    state = (paged_new, mla_new, new_win, ssd_s.astype(h.dtype), new_conv, hg_s.astype(h.dtype))
    return out, state


def decoder_layer(x, past_len, n_keep, past, lw, g, wi, wo):
    x = x + 0.5 * rmsnorm(swiglu(rmsnorm(x, g[0]), wi[0], wo[0]), g[1])
    y, state = token_mixers(rmsnorm(x, g[2]), past_len, n_keep, past, lw)
    x = x + rmsnorm(y, g[3])
    x = x + 0.5 * rmsnorm(swiglu(rmsnorm(x, g[4]), wi[1], wo[1]), g[5])
    return x, state


def setup_inputs(seed: int = 0) -> dict:
    key = jax.random.key(seed)
    k = jax.random.split(key, 32)
    f32 = jnp.float32

    def nrm(i, shape, s=1.0):
        return jax.random.normal(k[i], shape, f32) * s

    n_pages = PAST_LEN // PAGE_SIZE
    n_used = DEC_BATCH * n_pages
    n_pool = n_used + max(1, n_used // 4)
    win_buf = min(NSA_WINDOW, PAST_LEN)
    page_table = jax.random.permutation(k[0], n_pool)[:n_used].reshape(DEC_BATCH, n_pages).astype(jnp.int32)
    dt0 = jnp.exp(jax.random.uniform(k[1], (DEPTH, SSD_HEADS), f32, math.log(1e-3), math.log(1e-1)))
    dt_bias = dt0 + jnp.log(-jnp.expm1(-dt0))
    a_log = jnp.log(jax.random.uniform(k[2], (DEPTH, SSD_HEADS), f32, 1.0, 16.0))
    return {
        'x_prompt': nrm(3, (BATCH, SEQ, D_MODEL)),
        'x_sample': nrm(4, (DEC_BATCH, DEC_SEQ, D_MODEL)),
        'cache_nsa_kv': nrm(5, (DEPTH, n_pool, PAGE_SIZE, 4, NSA_DK)),
        'cache_mla': nrm(6, (DEPTH, n_pool, PAGE_SIZE, MLA_KV_LORA + MLA_ROPE)),
        'cache_nsa_win': nrm(7, (DEPTH, DEC_BATCH, win_buf, 2, NSA_DK)),
        'state_ssd': nrm(8, (DEPTH, DEC_BATCH, SSD_HEADS, SSD_HEAD_DIM, SSD_STATE), 0.5),
        'state_ssd_conv': nrm(9, (DEPTH, DEC_BATCH, SSD_CONV - 1, SSD_CONV_DIM)),
        'state_hgrn': nrm(10, (DEPTH, DEC_BATCH, HG_HEADS, HG_KDIM, HG_VDIM), 0.5),
        'page_table': page_table,
        'norm_g': 1.0 + nrm(11, (DEPTH, 6, D_MODEL), 0.05),
        'ffn_w_in': nrm(12, (DEPTH, 2, D_MODEL, 2 * D_FF), D_MODEL ** -0.5),
        'ffn_w_out': nrm(13, (DEPTH, 2, D_FF, D_MODEL), D_FF ** -0.5),
        'w_in': nrm(14, (DEPTH, D_MODEL, IN_WIDTH), D_MODEL ** -0.5),
        'mla_q_norm': 1.0 + nrm(15, (DEPTH, MLA_Q_LORA), 0.05),
        'mla_kv_norm': 1.0 + nrm(16, (DEPTH, MLA_KV_LORA), 0.05),
        'mla_w_uq': nrm(17, (DEPTH, MLA_Q_LORA, MLA_HEADS, MLA_NOPE + MLA_ROPE), MLA_Q_LORA ** -0.5),
        'mla_w_ukv': nrm(18, (DEPTH, MLA_KV_LORA, MLA_HEADS, MLA_NOPE + MLA_V), MLA_KV_LORA ** -0.5),
        'ssd_conv_w': nrm(19, (DEPTH, SSD_CONV, SSD_CONV_DIM), SSD_CONV ** -0.5),
        'ssd_conv_b': nrm(20, (DEPTH, SSD_CONV_DIM), 0.02),
        'ssd_dt_bias': dt_bias,
        'ssd_a_log': a_log,
        'ssd_d': 1.0 + nrm(21, (DEPTH, SSD_HEADS), 0.1),
        'ssd_norm': 1.0 + nrm(22, (DEPTH, SSD_WIDTH), 0.05),
        'hg_lb_logits': nrm(23, (DEPTH, HG_HEADS * HG_KDIM), 0.5),
        'hg_norm': 1.0 + nrm(24, (DEPTH, HG_HEADS * HG_VDIM), 0.05),
        'w_branch': nrm(25, (DEPTH, N_BRANCH, BRANCH_WIDTH, D_MODEL), BRANCH_WIDTH ** -0.5),
        'w_out': nrm(26, (DEPTH, D_MODEL, D_MODEL), D_MODEL ** -0.5),
    }


def reference(x_prompt, x_sample, cache_nsa_kv, cache_mla, cache_nsa_win, state_ssd, state_ssd_conv,
              state_hgrn, page_table, norm_g, ffn_w_in, ffn_w_out, w_in, mla_q_norm, mla_kv_norm,
              mla_w_uq, mla_w_ukv, ssd_conv_w, ssd_conv_b, ssd_dt_bias, ssd_a_log, ssd_d, ssd_norm,
              hg_lb_logits, hg_norm, w_branch, w_out):
    bp, sp = x_prompt.shape[0], x_prompt.shape[1]
    bs = x_sample.shape[0]
    past_len = page_table.shape[1] * cache_nsa_kv.shape[2]
    n_win = cache_nsa_win.shape[2]
    dt = x_prompt.dtype
    lbp = jax.nn.softmax(hg_lb_logits.astype(jnp.float32), axis=0)
    lower_bounds = jnp.cumsum(lbp, axis=0) - lbp[0]
    xp, xs = x_prompt, x_sample
    st_p, st_s = [], []
    for l in range(DEPTH):
        lw = (w_in[l], mla_q_norm[l], mla_kv_norm[l], mla_w_uq[l], mla_w_ukv[l], ssd_conv_w[l], ssd_conv_b[l],
              ssd_dt_bias[l], ssd_a_log[l], ssd_d[l], ssd_norm[l], lower_bounds[l], hg_norm[l], w_branch[l], w_out[l])
        prompt_past = (jnp.zeros((bp, 0, 4, NSA_DK), dt), jnp.zeros((bp, 0, MLA_KV_LORA + MLA_ROPE), dt),
                       jnp.zeros((bp, 0, 2, NSA_DK), dt),
                       jnp.zeros((bp, SSD_HEADS, SSD_HEAD_DIM, SSD_STATE), jnp.float32),
                       jnp.zeros((bp, SSD_CONV - 1, SSD_CONV_DIM), dt),
                       jnp.zeros((bp, HG_HEADS, HG_KDIM, HG_VDIM), jnp.float32))
        xp, sp_l = decoder_layer(xp, 0, min(NSA_WINDOW, sp), prompt_past, lw, norm_g[l], ffn_w_in[l], ffn_w_out[l])
        sample_past = (cache_nsa_kv[l, page_table].reshape(bs, past_len, 4, NSA_DK),
                       cache_mla[l, page_table].reshape(bs, past_len, MLA_KV_LORA + MLA_ROPE),
                       cache_nsa_win[l], state_ssd[l], state_ssd_conv[l], state_hgrn[l])
        xs, ss_l = decoder_layer(xs, past_len, n_win, sample_past, lw, norm_g[l], ffn_w_in[l], ffn_w_out[l])
        st_p.append(sp_l)
        st_s.append(ss_l)
    nsa_kv_p, mla_p, win_p, ssd_p, conv_p, hgrn_p = (jnp.stack(z) for z in zip(*st_p))
    nsa_kv_s, mla_s, win_s, ssd_s, conv_s, hgrn_s = (jnp.stack(z) for z in zip(*st_s))
    return (xp, xs, nsa_kv_p, nsa_kv_s, mla_p, mla_s, win_p, win_s, ssd_p, ssd_s, conv_p, conv_s, hgrn_p, hgrn_s)
```

```python
import functools
import math

import numpy as np
import jax
import jax.numpy as jnp
from jax import lax
from jax.experimental import pallas as pl
from jax.experimental.pallas import tpu as pltpu

F32 = jnp.float32
BF16 = jnp.bfloat16
I32 = jnp.int32
HIGHEST = lax.Precision.HIGHEST

EPS = 1e-6
NEG = -1e30
LB_FLOOR = 1e-20
ROPE_THETA = 10000.0

D_MODEL = 1024
D_FF = 2816
HEADS = 4
DK = 64
BRANCH = HEADS * DK
NSA_BLOCK = 64
NSA_TOPK = 16
NSA_WINDOW = 512
NSA_FORCED_BONUS = float(HEADS + 1)
MLA_Q_LORA = 256
MLA_KV_LORA = 128
MLA_NOPE = 64
MLA_ROPE = 32
MLA_V = 64
MLA_CACHE = MLA_KV_LORA + MLA_ROPE
SSD_GROUPS = 2
SSD_CONV = 4
SSD_CONV_DIM = BRANCH + 2 * SSD_GROUPS * DK
CHUNK = 128
QB = 128

LANES = 128
SUBLANES = 8
VMEM_LIMIT_BYTES = 56 * 1024 * 1024

_SEG = {}
_off = 0
for _name, _w in (("nsa_q", 256), ("nsa_pg", 256), ("nsa_w", 128), ("nsa_g", 128),
                  ("mla_cq", 256), ("mla_ckv", 128), ("mla_kpe", 128), ("mla_kpr", 128),
                  ("ssd_z", 256), ("ssd_xbc", 512), ("ssd_dt", 128), ("hg", 1024)):
    _SEG[_name] = (_off, _off + _w)
    _off += _w
PACKED_WIDTH = _off


def _cparams(sem):
    return pltpu.CompilerParams(dimension_semantics=sem, vmem_limit_bytes=VMEM_LIMIT_BYTES)


def _rms(x, g):
    return x * lax.rsqrt(jnp.mean(x * x, axis=-1, keepdims=True) + EPS) * g


def _dot(a, b):
    return jnp.dot(a.astype(BF16), b.astype(BF16), preferred_element_type=F32)


def _dot_nt(a, b):
    return lax.dot_general(a.astype(BF16), b.astype(BF16), (((1,), (1,)), ((), ())),
                           preferred_element_type=F32)


def _dot_f32(a, b):
    return jnp.dot(a, b, precision=HIGHEST, preferred_element_type=F32)


def _dot_nt_f32(a, b):
    return lax.dot_general(a, b, (((1,), (1,)), ((), ())), precision=HIGHEST,
                           preferred_element_type=F32)


def _sigmoid(x):
    return 1.0 / (1.0 + jnp.exp(-x))


def _silu(x):
    return x * _sigmoid(x)


def _softplus(x):
    return jnp.maximum(x, 0.0) + jnp.log1p(jnp.exp(-jnp.abs(x)))


def _masked_softmax(s, mask):
    s = jnp.where(mask, s, NEG)
    m = jnp.max(s, axis=-1, keepdims=True)
    e = jnp.exp(s - m)
    r = 1.0 / jnp.sum(e, axis=-1, keepdims=True)
    return jnp.where(mask, e * r, 0.0)


def _row_tile(rows, cap=864):
    best = SUBLANES
    for t in range(SUBLANES, min(rows, cap) + 1, SUBLANES):
        if rows % t == 0:
            best = t
    assert rows % best == 0
    return best


def _const_spec(shape):
    nd = len(shape)
    return pl.BlockSpec(shape, lambda *_: (0,) * nd)


def _ffn_kernel(x_ref, ga_ref, gb_ref, wg_ref, wu_ref, wo_ref, o_ref, hn_ref, acc_ref):
    j = pl.program_id(1)

    @pl.when(j == 0)
    def _():
        hn_ref[...] = _rms(x_ref[...], ga_ref[...]).astype(BF16)
        acc_ref[...] = jnp.zeros_like(acc_ref)

    h = hn_ref[...]
    gate = jnp.dot(h, wg_ref[...], preferred_element_type=F32)
    up = jnp.dot(h, wu_ref[...], preferred_element_type=F32)
    acc_ref[...] += _dot(_silu(gate) * up, wo_ref[...])

    @pl.when(j == pl.num_programs(1) - 1)
    def _():
        o_ref[...] = x_ref[...] + 0.5 * _rms(acc_ref[...], gb_ref[...])


def _ffn(x, ga, gb, wi, wo):
    rows = x.shape[0]
    tm = _row_tile(rows)
    fc = D_FF // 2
    nch = D_FF // fc
    return pl.pallas_call(
        _ffn_kernel,
        grid=(rows // tm, nch),
        in_specs=[
            pl.BlockSpec((tm, D_MODEL), lambda i, j: (i, 0)),
            _const_spec((1, D_MODEL)),
            _const_spec((1, D_MODEL)),
            pl.BlockSpec((D_MODEL, fc), lambda i, j: (0, j)),
            pl.BlockSpec((D_MODEL, fc), lambda i, j: (0, nch + j)),
            pl.BlockSpec((fc, D_MODEL), lambda i, j: (j, 0)),
        ],
        out_specs=pl.BlockSpec((tm, D_MODEL), lambda i, j: (i, 0)),
        out_shape=jax.ShapeDtypeStruct((rows, D_MODEL), F32),
        scratch_shapes=[pltpu.VMEM((tm, D_MODEL), BF16), pltpu.VMEM((tm, D_MODEL), F32)],
        compiler_params=_cparams(("parallel", "arbitrary")),
        name="ffn",
    )(x, ga, gb, wi, wi, wo)


def _inproj_kernel(x_ref, g_ref, cs_ref, w_ref, mqn_ref, mkvn_ref, wuq_ref, wukn_ref,
                   q_o, pg_o, w_o, g_o, qlat_o, qpe_o, mla_o, z_o, xbc_o, dt_o, hg_o):
    hn = _rms(x_ref[...], g_ref[...]).astype(BF16)

    def seg(name):
        a, b = _SEG[name]
        return jnp.dot(hn, w_ref[:, a:b], preferred_element_type=F32)

    q_o[...] = seg("nsa_q") * (DK ** -0.5)
    pg_o[...] = seg("nsa_pg")
    w_o[...] = seg("nsa_w")
    g_o[...] = _sigmoid(seg("nsa_g"))
    z_o[...] = seg("ssd_z")
    xbc_o[...] = seg("ssd_xbc")
    dt_o[...] = seg("ssd_dt")
    hg_o[...] = seg("hg")

    cos_p = cs_ref[:, 0:LANES]
    sin_p = cs_ref[:, LANES:2 * LANES]
    cqn = _rms(seg("mla_cq"), mqn_ref[...])
    qf = _dot(cqn, wuq_ref[...])
    nope_w = HEADS * MLA_NOPE
    qlat_o[...] = _dot(qf[:, 0:nope_w], wukn_ref[...])
    for h in range(HEADS):
        pe = qf[:, nope_w + LANES * h: nope_w + LANES * (h + 1)]
        pr = qf[:, nope_w + LANES * (HEADS + h): nope_w + LANES * (HEADS + h + 1)]
        qpe_o[:, LANES * h:LANES * (h + 1)] = pe * cos_p + pr * sin_p
    mla_o[:, 0:MLA_KV_LORA] = _rms(seg("mla_ckv"), mkvn_ref[...])
    kpe = seg("mla_kpe") * cos_p + seg("mla_kpr") * sin_p
    mla_o[:, MLA_KV_LORA:MLA_CACHE] = kpe[:, 0:MLA_ROPE]


_INPROJ_OUT_WIDTHS = (256, 256, 128, 128, 512, 512, MLA_CACHE, 256, 512, 128, 1024)


def _inproj(x, g, cs, w, mqn, mkvn, wuq, wukn):
    rows = x.shape[0]
    tm = _row_tile(rows)
    row_spec = lambda c: pl.BlockSpec((tm, c), lambda i: (i, 0))
    return pl.pallas_call(
        _inproj_kernel,
        grid=(rows // tm,),
        in_specs=[row_spec(D_MODEL), _const_spec((1, D_MODEL)), row_spec(2 * LANES),
                  _const_spec(w.shape), _const_spec(mqn.shape), _const_spec(mkvn.shape),
                  _const_spec(wuq.shape), _const_spec(wukn.shape)],
        out_specs=[row_spec(c) for c in _INPROJ_OUT_WIDTHS],
        out_shape=[jax.ShapeDtypeStruct((rows, c), F32) for c in _INPROJ_OUT_WIDTHS],
        compiler_params=_cparams(("parallel",)),
        name="inproj",
    )(x, g, cs, w, mqn, mkvn, wuq, wukn)


def _merge_kernel(x_ref, b0_ref, b1_ref, b2_ref, b3_ref, g2_ref, g3_ref, wg_ref, wbr_ref, wout_ref,
                  o_ref):
    x = x_ref[...]
    hn = _rms(x, g2_ref[...]).astype(BF16)
    acc = None
    for k, b_ref in enumerate((b0_ref, b1_ref, b2_ref, b3_ref)):
        gate = _sigmoid(jnp.dot(hn, wg_ref[:, D_MODEL * k:D_MODEL * (k + 1)],
                                preferred_element_type=F32))
        u = _dot(b_ref[...], wbr_ref[k])
        acc = gate * u if acc is None else acc + gate * u
    out = _dot(acc, wout_ref[...])
    o_ref[...] = x + _rms(out, g3_ref[...])


def _merge(x, branches, g2, g3, wgate, wbr, wout):
    rows = x.shape[0]
    tm = _row_tile(rows, cap=512)
    row_spec = lambda c: pl.BlockSpec((tm, c), lambda i: (i, 0))
    return pl.pallas_call(
        _merge_kernel,
        grid=(rows // tm,),
        in_specs=[row_spec(D_MODEL)] + [row_spec(BRANCH)] * 4 +
                 [_const_spec((1, D_MODEL)), _const_spec((1, D_MODEL)),
                  _const_spec(wgate.shape), _const_spec(wbr.shape), _const_spec(wout.shape)],
        out_specs=row_spec(D_MODEL),
        out_shape=jax.ShapeDtypeStruct((rows, D_MODEL), F32),
        compiler_params=_cparams(("parallel",)),
        name="merge",
    )(x, *branches, g2, g3, wgate, wbr, wout)


def _stack_heads(x, width, take):
    return jnp.concatenate([x[:, width * h: width * h + take] for h in range(HEADS)], axis=0)


def _nsa_prompt_kernel(q_ref, pg_ref, wk_ref, g_ref, e_ref, o_ref, *, T, band):
    nb = T // NSA_BLOCK
    k_top = min(NSA_TOPK, nb)
    q0 = pl.program_id(1) * QB
    q = q_ref[...]
    zeros = jnp.zeros((QB, DK), F32)
    q_rows = jnp.concatenate(
        [jnp.concatenate([q[:, DK * h:DK * (h + 1)], zeros], axis=1) for h in range(HEADS)],
        axis=0).astype(BF16)
    pos = q0 + lax.broadcasted_iota(I32, (QB, 1), 0)
    lane = lax.broadcasted_iota(I32, (QB, LANES), 1)

    def per_head(fn):
        return jnp.concatenate([fn(h) for h in range(HEADS)], axis=0)

    cm = jnp.sum(pg_ref[:, 0:LANES].reshape(nb, NSA_BLOCK, LANES), axis=1) * (1.0 / NSA_BLOCK)
    if nb < LANES:
        cm = jnp.concatenate([cm, jnp.zeros((LANES - nb, LANES), F32)], axis=0)
    cm = cm.astype(BF16)
    s_c = _dot_nt(q_rows, cm)
    cmask = (lane + 1) * NSA_BLOCK - 1 <= pos
    p_c = per_head(lambda h: _masked_softmax(s_c[QB * h:QB * (h + 1)], cmask))
    o_c = _dot(p_c, cm)

    cur = pos // NSA_BLOCK
    imp = p_c[0:QB] + p_c[QB:2 * QB] + p_c[2 * QB:3 * QB] + p_c[3 * QB:4 * QB]
    forced = (lane == 0) | (lane == cur) | (lane == cur - 1)
    imp = imp + jnp.where(forced, NSA_FORCED_BONUS, 0.0)
    imp = jnp.where(lane > cur, -1.0, imp)
    rank = jnp.zeros((QB, LANES), F32)
    for n in range(nb):
        c = imp[:, n:n + 1]
        beats = (c > imp) | ((c == imp) & (lane > n))
        rank = rank + jnp.where(beats, 1.0, 0.0)
    sel = jnp.where(rank < float(k_top), 1.0, 0.0)

    in_sel = jnp.dot(sel.astype(BF16), e_ref[...], preferred_element_type=F32)
    kpos = lax.broadcasted_iota(I32, (QB, T), 1)
    smask = (in_sel > 0.5) & (kpos <= pos)
    kv_s = pg_ref[:, LANES:2 * LANES].astype(BF16)
    s_s = _dot_nt(q_rows, kv_s)
    p_s = per_head(lambda h: _masked_softmax(s_s[QB * h:QB * (h + 1)], smask))
    o_s = _dot(p_s, kv_s)

    start = pl.multiple_of(jnp.clip(q0 - NSA_WINDOW, 0, T - band), LANES)
    kv_w = wk_ref[pl.ds(start, band), :].astype(BF16)
    kposw = start + lax.broadcasted_iota(I32, (QB, band), 1)
    wmask = (kposw <= pos) & (kposw >= pos - NSA_WINDOW)
    s_w = _dot_nt(q_rows, kv_w)
    p_w = per_head(lambda h: _masked_softmax(s_w[QB * h:QB * (h + 1)], wmask))
    o_w = _dot(p_w, kv_w)

    g = g_ref[...]
    outs = []
    for h in range(HEADS):
        r = slice(QB * h, QB * (h + 1))
        comb = (g[:, 3 * h:3 * h + 1] * o_c[r] + g[:, 3 * h + 1:3 * h + 2] * o_s[r]
                + g[:, 3 * h + 2:3 * h + 3] * o_w[r])
        outs.append(comb[:, DK:2 * DK])
    o_ref[...] = jnp.concatenate(outs, axis=1)


def _nsa_prompt(q, pg, wkv, g, B, T):
    assert T % QB == 0 and T // NSA_BLOCK <= LANES
    rows = B * T
    nq = T // QB
    band = min(NSA_WINDOW + QB, T)
    nb = T // NSA_BLOCK
    expand = (np.arange(LANES)[:, None] == (np.arange(T)[None, :] // NSA_BLOCK)) & (np.arange(LANES)[:, None] < nb)
    expand = jnp.asarray(expand, BF16)
    return pl.pallas_call(
        functools.partial(_nsa_prompt_kernel, T=T, band=band),
        grid=(B, nq),
        in_specs=[pl.BlockSpec((QB, BRANCH), lambda b, i: (b * nq + i, 0)),
                  pl.BlockSpec((T, 256), lambda b, i: (b, 0)),
                  pl.BlockSpec((T, LANES), lambda b, i: (b, 0)),
                  pl.BlockSpec((QB, LANES), lambda b, i: (b * nq + i, 0)),
                  _const_spec((LANES, T))],
        out_specs=pl.BlockSpec((QB, BRANCH), lambda b, i: (b * nq + i, 0)),
        out_shape=jax.ShapeDtypeStruct((rows, BRANCH), F32),
        compiler_params=_cparams(("parallel", "parallel")),
        name="nsa_prompt",
    )(q, pg, wkv, g, expand)


def _mla_prompt_kernel(qlat_ref, qpe_ref, kv_ref, wv_ref, o_ref, *, T):
    q0 = pl.program_id(1) * QB
    ql = _stack_heads(qlat_ref[...], LANES, LANES)
    qp = _stack_heads(qpe_ref[...], LANES, MLA_ROPE)
    ckv = kv_ref[:, 0:MLA_KV_LORA].astype(BF16)
    kpe = kv_ref[:, MLA_KV_LORA:MLA_CACHE].astype(BF16)
    scale = (MLA_NOPE + MLA_ROPE) ** -0.5
    s = (_dot_nt(ql, ckv) + _dot_nt(qp, kpe)) * scale
    pos = q0 + lax.broadcasted_iota(I32, (QB, 1), 0)
    mask = lax.broadcasted_iota(I32, (QB, T), 1) <= pos
    p = jnp.concatenate([_masked_softmax(s[QB * h:QB * (h + 1)], mask) for h in range(HEADS)],
                        axis=0)
    o_lat = _dot(p, ckv)
    o_cat = jnp.concatenate([o_lat[QB * h:QB * (h + 1)] for h in range(HEADS)], axis=1)
    o_ref[...] = _dot(o_cat, wv_ref[...])


def _mla_prompt(qlat, qpe, mla_new, wv, B, T):
    nq = T // QB
    rows = B * T
    return pl.pallas_call(
        functools.partial(_mla_prompt_kernel, T=T),
        grid=(B, nq),
        in_specs=[pl.BlockSpec((QB, 512), lambda b, i: (b * nq + i, 0)),
                  pl.BlockSpec((QB, 512), lambda b, i: (b * nq + i, 0)),
                  pl.BlockSpec((T, MLA_CACHE), lambda b, i: (b, 0)),
                  _const_spec(wv.shape)],
        out_specs=pl.BlockSpec((QB, BRANCH), lambda b, i: (b * nq + i, 0)),
        out_shape=jax.ShapeDtypeStruct((rows, BRANCH), F32),
        compiler_params=_cparams(("parallel", "parallel")),
        name="mla_prompt",
    )(qlat, qpe, mla_new, wv)


def _head_lanes(x_cols, shape):
    lane = lax.broadcasted_iota(I32, shape, 1)
    out = jnp.broadcast_to(x_cols[:, HEADS - 1:HEADS], shape)
    for h in range(HEADS - 2, -1, -1):
        out = jnp.where(lane < DK * (h + 1), jnp.broadcast_to(x_cols[:, h:h + 1], shape), out)
    return out


def _ssd_prompt_kernel(xbc_ref, dt_ref, z_ref, cw_ref, cb_ref, dtb_ref, alog_ref, dsk_ref, sn_ref,
                       o_ref, st_ref, buf_ref, s_ref):
    c = pl.program_id(1)
    C = CHUNK

    @pl.when(c == 0)
    def _():
        buf_ref[0:SUBLANES, :] = jnp.zeros((SUBLANES, SSD_CONV_DIM), F32)
        s_ref[...] = jnp.zeros_like(s_ref)

    x = xbc_ref[...]
    buf_ref[SUBLANES:SUBLANES + C, :] = x
    conv = cb_ref[...] + cw_ref[SSD_CONV - 1:SSD_CONV, :] * x
    for j in range(SSD_CONV - 1):
        conv = conv + cw_ref[j:j + 1, :] * buf_ref[pl.ds(SUBLANES - (SSD_CONV - 1) + j, C), :]
    buf_ref[0:SUBLANES, :] = x[C - SUBLANES:C, :]
    xa = _silu(conv)
    xs = xa[:, 0:BRANCH]
    dt = _softplus(dt_ref[...] + dtb_ref[...])
    a = dt * (-jnp.exp(alog_ref[...]))
    row = lax.broadcasted_iota(I32, (C, C), 0)
    col = lax.broadcasted_iota(I32, (C, C), 1)
    tril = row >= col
    tri = jnp.where(tril, 1.0, 0.0)
    cum_c = _dot_f32(tri, a)
    cum_r = _dot_nt_f32(a.T, tri)
    xdt = xs * _head_lanes(dt, (C, BRANCH))
    xdt_t = xdt.T
    dsk = dsk_ref[...]
    ys = []
    for h in range(HEADS):
        g = h // (HEADS // SSD_GROUPS)
        bm = xa[:, BRANCH + DK * g:BRANCH + DK * (g + 1)]
        cm = xa[:, BRANCH + DK * (SSD_GROUPS + g):BRANCH + DK * (SSD_GROUPS + g + 1)]
        ch = cum_c[:, h:h + 1]
        decay = jnp.exp(jnp.where(tril, ch - cum_r[h:h + 1, :], NEG))
        scores = _dot_nt(cm, bm) * decay
        xdt_h = xdt[:, DK * h:DK * (h + 1)]
        s_old = s_ref[h]
        y = _dot(scores, xdt_h) + _dot_nt(cm, s_old) * jnp.exp(ch)
        last = cum_c[C - 1:C, h:h + 1]
        w = jnp.exp(last - ch)
        s_ref[h] = s_old * jnp.exp(last) + _dot(xdt_t[DK * h:DK * (h + 1), :], bm * w)
        ys.append(y + dsk[:, h:h + 1] * xs[:, DK * h:DK * (h + 1)])
    y = jnp.concatenate(ys, axis=1)
    o_ref[...] = _rms(y * _silu(z_ref[...]), sn_ref[...])

    @pl.when(c == pl.num_programs(1) - 1)
    def _():
        st_ref[0] = s_ref[...]


def _ssd_prompt(xbc, dt, z, cw, cb, dtb, alog, dsk, sn, B, T):
    assert T % CHUNK == 0
    nc = T // CHUNK
    rows = B * T
    row_spec = lambda w: pl.BlockSpec((CHUNK, w), lambda b, c: (b * nc + c, 0))
    return pl.pallas_call(
        _ssd_prompt_kernel,
        grid=(B, nc),
        in_specs=[row_spec(SSD_CONV_DIM), row_spec(LANES), row_spec(BRANCH),
                  _const_spec(cw.shape), _const_spec(cb.shape), _const_spec(dtb.shape),
                  _const_spec(alog.shape), _const_spec(dsk.shape), _const_spec(sn.shape)],
        out_specs=[row_spec(BRANCH), pl.BlockSpec((1, HEADS, DK, DK), lambda b, c: (b, 0, 0, 0))],
        out_shape=[jax.ShapeDtypeStruct((rows, BRANCH), F32),
                   jax.ShapeDtypeStruct((B, HEADS, DK, DK), F32)],
        scratch_shapes=[pltpu.VMEM((SUBLANES + CHUNK, SSD_CONV_DIM), F32),
                        pltpu.VMEM((HEADS, DK, DK), F32)],
        compiler_params=_cparams(("parallel", "arbitrary")),
        name="ssd_prompt",
    )(xbc, dt, z, cw, cb, dtb, alog, dsk, sn)


def _hgrn_gates(fr, lb):
    log_sig = jnp.minimum(fr, 0.0) - jnp.log1p(jnp.exp(-jnp.abs(fr)))
    a = jnp.log(jnp.maximum(lb, LB_FLOOR))
    b = jnp.log1p(-lb) + log_sig
    log_f = jnp.maximum(a, b) + jnp.log1p(jnp.exp(-jnp.abs(a - b)))
    return log_f, (1.0 - lb) * _sigmoid(-fr)


def _block_reference_rows(G, m):
    C = G.shape[0]
    if m >= 4:
        parts = []
        for p in range(C // (2 * m)):
            r = p * 2 * m + m - 1
            parts.append(jnp.broadcast_to(G[r:r + 1, :], (2 * m, G.shape[1])))
        return parts[0] if len(parts) == 1 else jnp.concatenate(parts, axis=0)
    t = lax.broadcasted_iota(I32, G.shape, 0)
    if m == 1:
        return jnp.where((t & 1) == 1, pltpu.roll(G, 1, 0), G)
    r = t & 3
    return jnp.where(r == 0, pltpu.roll(G, C - 1, 0),
                     jnp.where(r == 1, G, jnp.where(r == 2, pltpu.roll(G, 1, 0), pltpu.roll(G, 2, 0))))


def _hgrn_prompt_kernel(hg_ref, lb_ref, hn_ref, seg_ref, o_ref, st_ref, s_ref):
    c = pl.program_id(1)
    C = CHUNK
    W = BRANCH

    @pl.when(c == 0)
    def _():
        s_ref[...] = jnp.zeros_like(s_ref)

    q = hg_ref[:, 0:W]
    v = hg_ref[:, 2 * W:3 * W]
    log_f, kin = _hgrn_gates(hg_ref[:, W:2 * W], lb_ref[...])
    row = lax.broadcasted_iota(I32, (C, C), 0)
    col = lax.broadcasted_iota(I32, (C, C), 1)
    G = _dot_f32(jnp.where(row >= col, 1.0, 0.0), log_f)
    lane_head = lax.broadcasted_iota(I32, (C, W), 1) // DK
    t_idx = lax.broadcasted_iota(I32, (C, W), 0)

    def stack_heads(a):
        return jnp.concatenate([jnp.where(lane_head == h, a, 0.0) for h in range(HEADS)],
                               axis=0).astype(BF16)

    t4 = lax.broadcasted_iota(I32, (HEADS * C, C), 0) & (C - 1)
    s4 = lax.broadcasted_iota(I32, (HEADS * C, C), 1)
    att = jnp.where(t4 == s4, _dot_nt(stack_heads(q), kin), 0.0)
    m = C // 2
    while m >= 1:
        R = _block_reference_rows(G, m)
        upper = (t_idx & m) != 0
        A = jnp.where(upper, q * jnp.exp(jnp.minimum(G - R, 0.0)), 0.0)
        Bm = jnp.where(upper, 0.0, kin * jnp.exp(jnp.minimum(R - G, 0.0)))
        lm = int(math.log2(m))
        pair = ((t4 >> lm) ^ (s4 >> lm)) == 1
        att = att + jnp.where(pair & (t4 > s4), _dot_nt(stack_heads(A), Bm), 0.0)
        m //= 2
    y4 = _dot(att, v)
    y = jnp.zeros((C, W), F32)
    for h in range(HEADS):
        y = y + jnp.where(lane_head == h, y4[C * h:C * (h + 1)], 0.0)
    s_old = s_ref[...]
    y = y + _dot(q * jnp.exp(G), s_old)
    last = G[C - 1:C, :]
    ke_t = (kin * jnp.exp(last - G)).T
    G_t = G.T
    blk = (lax.broadcasted_iota(I32, (W, W), 0) // DK) == (lax.broadcasted_iota(I32, (W, W), 1) // DK)
    s_ref[...] = s_old * jnp.exp(G_t[:, C - 1:C]) + jnp.where(blk, _dot(ke_t, v), 0.0)
    ms = _dot_f32(y * y, seg_ref[...])
    o = y * lax.rsqrt(ms + EPS) * hn_ref[...]
    o_ref[...] = o * _silu(hg_ref[:, 3 * W:4 * W])

    @pl.when(c == pl.num_programs(1) - 1)
    def _():
        for h in range(HEADS):
            st_ref[0, h] = s_ref[DK * h:DK * (h + 1), DK * h:DK * (h + 1)]


def _hgrn_prompt(hg, lb, hn, B, T):
    nc = T // CHUNK
    rows = B * T
    seg = (np.arange(BRANCH)[:, None] // DK == np.arange(BRANCH)[None, :] // DK) / float(DK)
    seg = jnp.asarray(seg, F32)
    row_spec = lambda w: pl.BlockSpec((CHUNK, w), lambda b, c: (b * nc + c, 0))
    return pl.pallas_call(
        _hgrn_prompt_kernel,
        grid=(B, nc),
        in_specs=[row_spec(4 * BRANCH), _const_spec(lb.shape), _const_spec(hn.shape),
                  _const_spec(seg.shape)],
        out_specs=[row_spec(BRANCH), pl.BlockSpec((1, HEADS, DK, DK), lambda b, c: (b, 0, 0, 0))],
        out_shape=[jax.ShapeDtypeStruct((rows, BRANCH), F32),
                   jax.ShapeDtypeStruct((B, HEADS, DK, DK), F32)],
        scratch_shapes=[pltpu.VMEM((BRANCH, BRANCH), F32)],
        compiler_params=_cparams(("parallel", "arbitrary")),
        name="hgrn_prompt",
    )(hg, lb, hn, seg)


PAGES_PER_STEP = 8


def _mla_decode_kernel(pt_ref, qlat_ref, qpe_ref, new_ref, *rest, n_pg):
    page_refs = rest[:n_pg]
    wv_ref, o_ref, m_ref, l_ref, acc_ref = rest[n_pg:]
    del pt_ref
    b = pl.program_id(0)
    j = pl.program_id(1)
    scale = (MLA_NOPE + MLA_ROPE) ** -0.5

    @pl.when(j == 0)
    def _():
        m_ref[...] = jnp.full_like(m_ref, NEG)
        l_ref[...] = jnp.zeros_like(l_ref)
        acc_ref[...] = jnp.zeros_like(acc_ref)

    ql = qlat_ref[0]
    qp = qpe_ref[0]
    kv = jnp.concatenate([r[...] for r in page_refs], axis=0)
    ckv = kv[:, 0:MLA_KV_LORA].astype(BF16)
    s = (_dot_nt(ql, ckv) + _dot_nt(qp, kv[:, MLA_KV_LORA:MLA_CACHE])) * scale
    m_old = m_ref[...]
    m_new = jnp.maximum(m_old, jnp.max(s, axis=-1, keepdims=True))
    alpha = jnp.exp(m_old - m_new)
    p = jnp.exp(s - m_new[:, 0:1])
    l_ref[...] = alpha * l_ref[...] + jnp.sum(p, axis=-1, keepdims=True)
    acc_ref[...] = alpha * acc_ref[...] + _dot(p, ckv)
    m_ref[...] = m_new

    @pl.when(j == pl.num_programs(1) - 1)
    def _():
        new = new_ref[0]
        s_n = (jnp.sum(ql * new[:, 0:MLA_KV_LORA], axis=-1, keepdims=True)
               + jnp.sum(qp * new[:, MLA_KV_LORA:MLA_CACHE], axis=-1, keepdims=True)) * scale
        m_old = m_ref[...]
        m_fin = jnp.maximum(m_old, s_n)
        alpha = jnp.exp(m_old - m_fin)
        p_n = jnp.exp(s_n - m_fin)
        l = alpha * l_ref[...] + p_n
        o_lat = (alpha * acc_ref[...] + p_n * new[:, 0:MLA_KV_LORA]) / l
        o_cat = jnp.concatenate([o_lat[h:h + 1, :] for h in range(HEADS)], axis=1)
        o = _dot(jnp.broadcast_to(o_cat, (SUBLANES, HEADS * MLA_KV_LORA)), wv_ref[...])
        o_ref[pl.ds(b, 1), :] = o[0:1, :]


def _mla_decode(layer, page_table, qlat8, qpe8, new_rows, cache, wv):
    BS, npages = page_table.shape
    page = cache.shape[2]
    n_pg = min(PAGES_PER_STEP, npages)
    assert npages % n_pg == 0
    steps = npages // n_pg

    def page_spec(i):
        return pl.BlockSpec((None, None, page, MLA_CACHE),
                            lambda b, j, pt: (layer, pt[b, j * n_pg + i], 0, 0))

    grid_spec = pltpu.PrefetchScalarGridSpec(
        num_scalar_prefetch=1,
        grid=(BS, steps),
        in_specs=[pl.BlockSpec((1, SUBLANES, LANES), lambda b, j, pt: (b, 0, 0)),
                  pl.BlockSpec((1, SUBLANES, MLA_ROPE), lambda b, j, pt: (b, 0, 0)),
                  pl.BlockSpec((1, 1, MLA_CACHE), lambda b, j, pt: (b, 0, 0))]
                 + [page_spec(i) for i in range(n_pg)]
                 + [pl.BlockSpec(wv.shape, lambda b, j, pt: (0, 0))],
        out_specs=pl.BlockSpec((BS, BRANCH), lambda b, j, pt: (0, 0)),
        scratch_shapes=[pltpu.VMEM((SUBLANES, LANES), F32), pltpu.VMEM((SUBLANES, LANES), F32),
                        pltpu.VMEM((SUBLANES, LANES), F32)],
    )
    return pl.pallas_call(
        functools.partial(_mla_decode_kernel, n_pg=n_pg),
        grid_spec=grid_spec,
        out_shape=jax.ShapeDtypeStruct((BS, BRANCH), F32),
        compiler_params=_cparams(("arbitrary", "arbitrary")),
        name="mla_decode",
    )(page_table, qlat8, qpe8, new_rows, *([cache] * n_pg), wv)


def _nsa_cmp_decode_kernel(pt_ref, q_ref, *rest, n_pg, past):
    page_refs = rest[:n_pg]
    oc_ref, imp_ref, cme_ref, cmo_ref = rest[n_pg:]
    del pt_ref
    j = pl.program_id(1)

    @pl.when(j == 0)
    def _():
        cme_ref[...] = jnp.zeros_like(cme_ref)
        cmo_ref[...] = jnp.zeros_like(cmo_ref)

    for i, r in enumerate(page_refs):
        pg = r[...]
        idx = j * n_pg + i
        cme_ref[pl.ds(idx, 1), :] = jnp.sum(pg[0:NSA_BLOCK], axis=0, keepdims=True) * (1.0 / NSA_BLOCK)
        cmo_ref[pl.ds(idx, 1), :] = jnp.sum(pg[NSA_BLOCK:2 * NSA_BLOCK], axis=0, keepdims=True) * (1.0 / NSA_BLOCK)

    @pl.when(j == pl.num_programs(1) - 1)
    def _():
        q = q_ref[0]
        cm = jnp.concatenate([cme_ref[...], cmo_ref[...]], axis=0).astype(BF16)
        s = _dot_nt(q, cm)
        lane = lax.broadcasted_iota(I32, (SUBLANES, 2 * LANES), 1)
        blk = jnp.where(lane < LANES, 2 * lane, 2 * (lane - LANES) + 1)
        cmask = (blk + 1) * NSA_BLOCK - 1 <= past
        p = _masked_softmax(s, cmask)
        oc_ref[0] = _dot(p, cm)
        cur = past // NSA_BLOCK
        imp = jnp.sum(p[0:HEADS], axis=0, keepdims=True)
        blk1 = blk[0:1]
        forced = (blk1 == 0) | (blk1 == cur) | (blk1 == cur - 1)
        imp = imp + jnp.where(forced, NSA_FORCED_BONUS, 0.0)
        imp_ref[0] = jnp.where(blk1 > cur, -1.0, imp)


def _nsa_cmp_decode(layer, page_table, q8, cache4, past):
    BS, npages = page_table.shape
    page = cache4.shape[2]
    assert page == 2 * NSA_BLOCK and npages <= LANES
    n_pg = min(PAGES_PER_STEP, npages)
    steps = npages // n_pg

    def page_spec(i):
        return pl.BlockSpec((None, None, page, LANES),
                            lambda b, j, pt: (layer, pt[b, j * n_pg + i], 0, 0))

    grid_spec = pltpu.PrefetchScalarGridSpec(
        num_scalar_prefetch=1,
        grid=(BS, steps),
        in_specs=[pl.BlockSpec((1, SUBLANES, LANES), lambda b, j, pt: (b, 0, 0))]
                 + [page_spec(i) for i in range(n_pg)],
        out_specs=[pl.BlockSpec((1, SUBLANES, LANES), lambda b, j, pt: (b, 0, 0)),
                   pl.BlockSpec((1, 1, 2 * LANES), lambda b, j, pt: (b, 0, 0))],
        scratch_shapes=[pltpu.VMEM((LANES, LANES), F32), pltpu.VMEM((LANES, LANES), F32)],
    )
    return pl.pallas_call(
        functools.partial(_nsa_cmp_decode_kernel, n_pg=n_pg, past=past),
        grid_spec=grid_spec,
        out_shape=[jax.ShapeDtypeStruct((BS, SUBLANES, LANES), F32),
                   jax.ShapeDtypeStruct((BS, 1, 2 * LANES), F32)],
        compiler_params=_cparams(("parallel", "arbitrary")),
        name="nsa_cmp_decode",
    )(page_table, q8, *([cache4] * n_pg))


def _topk_kernel(imp_ref, idx_ref, *, past, k_top):
    BS = imp_ref.shape[0]
    cur = past // NSA_BLOCK
    lane = lax.broadcasted_iota(I32, (BS, 2 * LANES), 1)
    blk = jnp.where(lane < LANES, 2 * lane, 2 * (lane - LANES) + 1).astype(F32)
    lane_x = lax.broadcasted_iota(I32, (BS, LANES), 1)
    imp = jnp.concatenate([imp_ref[...], jnp.where(lane_x == 0, NSA_FORCED_BONUS, -3.0)], axis=1)
    blk = jnp.concatenate([blk, (cur + lane_x).astype(F32)], axis=1)
    out = jnp.zeros((BS, LANES), F32)
    for k in range(k_top):
        m = jnp.max(imp, axis=-1, keepdims=True)
        pick = jnp.min(jnp.where(imp == m, blk, 1e9), axis=-1, keepdims=True)
        imp = jnp.where(blk == pick, -4.0, imp)
        out = jnp.where(lane_x == k, pick, out)
    idx_ref[...] = out.astype(I32)


def _topk(imp, past, k_top):
    BS = imp.shape[0]
    return pl.pallas_call(
        functools.partial(_topk_kernel, past=past, k_top=k_top),
        out_shape=jax.ShapeDtypeStruct((BS, LANES), I32),
        name="nsa_topk",
    )(imp)


def _nsa_sel_decode_kernel(pt_ref, idx_ref, q_ref, pgn_ref, wn_ref, win_ref, g_ref, oc_ref, *rest,
                           k_top, n_blocks):
    blk_refs = rest[:k_top]
    (o_ref,) = rest[k_top:]
    del pt_ref
    b = pl.program_id(0)
    q = q_ref[0]

    def attend(kv, s_mask, new_row, new_on):
        s = _dot_nt(q, kv)
        if s_mask is not None:
            s = jnp.where(s_mask, s, NEG)
        s_n = jnp.where(new_on, jnp.sum(q * new_row, axis=-1, keepdims=True), NEG)
        m = jnp.maximum(jnp.max(s, axis=-1, keepdims=True), s_n)
        e = jnp.exp(s - m)
        if s_mask is not None:
            e = jnp.where(s_mask, e, 0.0)
        e_n = jnp.where(new_on, jnp.exp(s_n - m), 0.0)
        l = jnp.sum(e, axis=-1, keepdims=True) + e_n
        return (_dot(e, kv) + e_n * new_row) / l

    kv_s = jnp.concatenate([r[...] for r in blk_refs], axis=0).astype(BF16)
    slot = lax.broadcasted_iota(I32, (1, k_top * NSA_BLOCK), 1) // NSA_BLOCK
    valid = jnp.zeros((1, k_top * NSA_BLOCK), I32)
    has_new = jnp.zeros((), jnp.bool_)
    for k in range(k_top):
        valid = jnp.where(slot == k, (idx_ref[b, k] < n_blocks).astype(I32), valid)
        has_new = has_new | (idx_ref[b, k] == n_blocks)
    o_s = attend(kv_s, valid > 0, pgn_ref[0][:, LANES:2 * LANES], has_new)
    o_w = attend(win_ref[...].astype(BF16), None, wn_ref[0], True)
    o_c = oc_ref[0]
    g = g_ref[0]
    outs = []
    for h in range(HEADS):
        comb = (g[:, 3 * h:3 * h + 1] * o_c[h:h + 1] + g[:, 3 * h + 1:3 * h + 2] * o_s[h:h + 1]
                + g[:, 3 * h + 2:3 * h + 3] * o_w[h:h + 1])
        outs.append(comb[:, DK:2 * DK])
    o_ref[pl.ds(b, 1), :] = jnp.concatenate(outs, axis=1)


def _nsa_sel_decode(layer, page_table, idx, q8, pg_new, w_new, win_cache, g_new, o_c, cache_blk,
                    k_top):
    BS, npages = page_table.shape
    n_blocks = 2 * npages
    nwin = win_cache.shape[2]
    blk = jnp.minimum(idx[:, :k_top], n_blocks - 1)
    phys = jnp.take_along_axis(page_table, blk // 2, axis=1) * 2 + blk % 2

    def blk_spec(k):
        return pl.BlockSpec((None, None, NSA_BLOCK, LANES), lambda b, ph, ix: (layer, ph[b, k], 0, 1))

    row3 = lambda w: pl.BlockSpec((1, 1, w), lambda b, pt, ix: (b, 0, 0))
    grid_spec = pltpu.PrefetchScalarGridSpec(
        num_scalar_prefetch=2,
        grid=(BS,),
        in_specs=[pl.BlockSpec((1, SUBLANES, LANES), lambda b, pt, ix: (b, 0, 0)),
                  row3(256), row3(LANES),
                  pl.BlockSpec((None, None, nwin, LANES), lambda b, pt, ix: (layer, b, 0, 0)),
                  row3(LANES),
                  pl.BlockSpec((1, SUBLANES, LANES), lambda b, pt, ix: (b, 0, 0))]
                 + [blk_spec(k) for k in range(k_top)],
        out_specs=pl.BlockSpec((BS, BRANCH), lambda b, pt, ix: (0, 0)),
    )
    return pl.pallas_call(
        functools.partial(_nsa_sel_decode_kernel, k_top=k_top, n_blocks=n_blocks),
        grid_spec=grid_spec,
        out_shape=jax.ShapeDtypeStruct((BS, BRANCH), F32),
        compiler_params=_cparams(("arbitrary",)),
        name="nsa_sel_decode",
    )(phys, idx, q8, pg_new, w_new, win_cache, g_new, o_c, *([cache_blk] * k_top))


def _column(row, eye):
    return jnp.sum(jnp.where(eye, jnp.broadcast_to(row, eye.shape), 0.0), axis=1, keepdims=True)


def _rec_decode_kernel(xbc_ref, cbuf_ref, dt_ref, z_ref, hg_ref, s_ref, hs_ref,
                       cw_ref, cb_ref, dtb_ref, alog_ref, dsk_ref, sn_ref, lb_ref, hn_ref,
                       os_ref, oh_ref, sn_out, hs_out):
    b = pl.program_id(0)
    eye = lax.broadcasted_iota(I32, (DK, DK), 0) == lax.broadcasted_iota(I32, (DK, DK), 1)

    cbuf = cbuf_ref[...]
    xn = xbc_ref[0]
    conv = cb_ref[...] + cw_ref[SSD_CONV - 1:SSD_CONV, :] * xn
    for j in range(SSD_CONV - 1):
        conv = conv + cw_ref[j:j + 1, :] * cbuf[j:j + 1, :]
    xa = _silu(conv)
    dt = _softplus(dt_ref[0] + dtb_ref[...])
    ea = jnp.exp(dt * (-jnp.exp(alog_ref[...])))
    dsk = dsk_ref[...]
    ys = []
    for h in range(HEADS):
        g = h // (HEADS // SSD_GROUPS)
        xs = xa[:, DK * h:DK * (h + 1)]
        bm = xa[:, BRANCH + DK * g:BRANCH + DK * (g + 1)]
        cm = xa[:, BRANCH + DK * (SSD_GROUPS + g):BRANCH + DK * (SSD_GROUPS + g + 1)]
        xdt_col = _column(xs * dt[:, h:h + 1], eye)
        s_old = s_ref[h]
        ea_h = ea[:, h:h + 1]
        y_col = (jnp.sum(cm * bm, axis=-1, keepdims=True) * xdt_col
                 + jnp.sum(s_old * cm, axis=-1, keepdims=True) * ea_h)
        sn_out[0, h] = s_old * ea_h + xdt_col * bm
        y_row = jnp.sum(jnp.where(eye, jnp.broadcast_to(y_col, (DK, DK)), 0.0), axis=0, keepdims=True)
        ys.append(y_row + dsk[:, h:h + 1] * xs)
    y = jnp.concatenate(ys, axis=1)
    os_ref[pl.ds(b, 1), :] = _rms(y * _silu(z_ref[0]), sn_ref[...])

    W = BRANCH
    hg = hg_ref[0]
    q = hg[:, 0:W]
    v = hg[:, 2 * W:3 * W]
    log_f, kin = _hgrn_gates(hg[:, W:2 * W], lb_ref[...])
    ef = jnp.exp(log_f)
    hn = hn_ref[...]
    outs = []
    for h in range(HEADS):
        r = slice(DK * h, DK * (h + 1))
        s_old = hs_ref[h]
        att = jnp.sum(q[:, r] * kin[:, r], axis=-1, keepdims=True)
        y = att * v[:, r] + jnp.sum(s_old * _column(q[:, r] * ef[:, r], eye), axis=0, keepdims=True)
        hs_out[0, h] = s_old * _column(ef[:, r], eye) + _column(kin[:, r], eye) * v[:, r]
        outs.append(_rms(y, hn[:, r]))
    o = jnp.concatenate(outs, axis=1)
    oh_ref[pl.ds(b, 1), :] = o * _silu(hg[:, 3 * W:4 * W])


def _rec_decode(layer, xbc_new, conv_state, dt_new, z_new, hg_new, state_ssd, state_hgrn,
                cw, cb, dtb, alog, dsk, sn, lb, hn):
    BS = xbc_new.shape[0]
    row3 = lambda w: pl.BlockSpec((1, 1, w), lambda b: (b, 0, 0))
    st_in = pl.BlockSpec((None, None, HEADS, DK, DK), lambda b: (layer, b, 0, 0, 0))
    st_out = pl.BlockSpec((1, HEADS, DK, DK), lambda b: (b, 0, 0, 0))
    o_spec = pl.BlockSpec((BS, BRANCH), lambda b: (0, 0))
    consts = (cw, cb, dtb, alog, dsk, sn, lb, hn)
    return pl.pallas_call(
        _rec_decode_kernel,
        grid=(BS,),
        in_specs=[row3(SSD_CONV_DIM),
                  pl.BlockSpec((None, None, SSD_CONV - 1, SSD_CONV_DIM), lambda b: (layer, b, 0, 0)),
                  row3(LANES), row3(BRANCH), row3(4 * BRANCH), st_in, st_in]
                 + [_const_spec(c.shape) for c in consts],
        out_specs=[o_spec, o_spec, st_out, st_out],
        out_shape=[jax.ShapeDtypeStruct((BS, BRANCH), F32), jax.ShapeDtypeStruct((BS, BRANCH), F32),
                   jax.ShapeDtypeStruct((BS, HEADS, DK, DK), F32),
                   jax.ShapeDtypeStruct((BS, HEADS, DK, DK), F32)],
        compiler_params=_cparams(("arbitrary",)),
        name="rec_decode",
    )(xbc_new, conv_state, dt_new, z_new, hg_new, state_ssd, state_hgrn, *consts)


def _rot_cols(w):
    half = MLA_ROPE // 2
    return jnp.concatenate([-w[..., half:], w[..., :half]], axis=-1)


def _pad_cols(w, width):
    return jnp.pad(w, [(0, 0)] * (w.ndim - 1) + [(0, width - w.shape[-1])])


def _pack_w_in(w):
    sizes = (256, 384, 12, MLA_Q_LORA, MLA_KV_LORA, MLA_ROPE, 256, SSD_CONV_DIM, HEADS, 256, 256, 256, 256)
    offs = np.cumsum((0,) + sizes)
    (nsa_q, nsa_kv, nsa_g, cq, ckv, kpe, z, xbc, dt, hq, hf, hi, hgate) = (
        w[:, offs[i]:offs[i + 1]] for i in range(len(sizes)))
    gate = w[:, offs[-1]:]
    packed = jnp.concatenate([
        nsa_q, nsa_kv[:, :256], nsa_kv[:, 256:], _pad_cols(nsa_g, LANES), cq, ckv,
        _pad_cols(kpe, LANES), _pad_cols(_rot_cols(kpe), LANES), z, xbc, _pad_cols(dt, LANES),
        hq, hf, hi, hgate], axis=1)
    assert packed.shape[1] == PACKED_WIDTH
    return packed.astype(BF16), gate.astype(BF16)


def _pack_mla(wuq, wukv):
    nope = wuq[:, :, :MLA_NOPE].reshape(MLA_Q_LORA, HEADS * MLA_NOPE)
    pe = wuq[:, :, MLA_NOPE:]
    pe_p = _pad_cols(pe, LANES).reshape(MLA_Q_LORA, HEADS * LANES)
    pr_p = _pad_cols(_rot_cols(pe), LANES).reshape(MLA_Q_LORA, HEADS * LANES)
    wuq_p = jnp.concatenate([nope, pe_p, pr_p], axis=1).astype(BF16)
    eye = jnp.eye(HEADS, dtype=wukv.dtype)
    wukn = jnp.einsum('rhn,hg->hngr', wukv[:, :, :MLA_NOPE], eye).reshape(HEADS * MLA_NOPE, HEADS * MLA_KV_LORA)
    wv = jnp.einsum('rhv,hg->hrgv', wukv[:, :, MLA_NOPE:], eye).reshape(HEADS * MLA_KV_LORA, HEADS * MLA_V)
    return wuq_p, wukn.astype(BF16), wv.astype(BF16)


def _rope_table(positions):
    half = MLA_ROPE // 2
    freq = ROPE_THETA ** (-jnp.arange(half, dtype=F32) / half)
    ang = positions.astype(F32)[:, None] * freq
    cos = jnp.concatenate([jnp.cos(ang), jnp.cos(ang)], axis=1)
    sin = jnp.concatenate([jnp.sin(ang), jnp.sin(ang)], axis=1)
    return jnp.concatenate([_pad_cols(cos, LANES), _pad_cols(sin, LANES)], axis=1)


def _pad_row(v, width=LANES):
    return _pad_cols(v.reshape(1, -1).astype(F32), width)


def kernel(x_prompt, x_sample, cache_nsa_kv, cache_mla, cache_nsa_win, state_ssd, state_ssd_conv, state_hgrn, page_table, norm_g, ffn_w_in, ffn_w_out, w_in, mla_q_norm, mla_kv_norm, mla_w_uq, mla_w_ukv, ssd_conv_w, ssd_conv_b, ssd_dt_bias, ssd_a_log, ssd_d, ssd_norm, hg_lb_logits, hg_norm, w_branch, w_out):
    BP, T, D = x_prompt.shape
    BS = x_sample.shape[0]
    depth = w_in.shape[0]
    n_pool, page = cache_nsa_kv.shape[1], cache_nsa_kv.shape[2]
    npages = page_table.shape[1]
    past = npages * page
    n_win = cache_nsa_win.shape[2]
    rows_p = BP * T
    assert D == D_MODEL and x_sample.shape[1] == 1 and BS % SUBLANES == 0
    assert n_win == min(NSA_WINDOW, past) and past % NSA_BLOCK == 0
    k_top = min(NSA_TOPK, past // NSA_BLOCK + 1)

    xp = x_prompt.reshape(rows_p, D)
    xs = x_sample.reshape(BS, D)
    cs_p = _rope_table(jnp.tile(jnp.arange(T, dtype=I32), BP))
    cs_s = _rope_table(jnp.full((BS,), past, I32))
    lbp = jax.nn.softmax(hg_lb_logits.astype(F32), axis=0)
    lower_bounds = jnp.cumsum(lbp, axis=0) - lbp[0]

    cache4 = cache_nsa_kv.reshape(depth, n_pool, page, 4 * DK)
    cache_blk = cache_nsa_kv.reshape(depth, n_pool * (page // NSA_BLOCK), NSA_BLOCK, 4 * DK)
    win_cache = cache_nsa_win.reshape(depth, BS, n_win, 2 * DK)

    def row3(a):
        return a.reshape(BS, 1, a.shape[1])

    outs = [[] for _ in range(12)]
    for l in range(depth):
        g = norm_g[l].astype(F32)
        grow = lambda i: g[i].reshape(1, D)
        wi = ffn_w_in[l].astype(BF16)
        wo = ffn_w_out[l].astype(BF16)
        w_pack, w_gate = _pack_w_in(w_in[l])
        wuq_p, wukn, wv = _pack_mla(mla_w_uq[l], mla_w_ukv[l])
        mqn = mla_q_norm[l].reshape(1, -1)
        mkvn = mla_kv_norm[l].reshape(1, -1)

        xp = _ffn(xp, grow(0), grow(1), wi[0], wo[0])
        xs = _ffn(xs, grow(0), grow(1), wi[0], wo[0])
        (nsa_q, nsa_pg, nsa_w, nsa_g, qlat, qpe, mla_new, ssd_z, ssd_xbc, ssd_dt, hg) = _inproj(
            xp, grow(2), cs_p, w_pack, mqn, mkvn, wuq_p, wukn)
        (nsa_q_s, nsa_pg_s, nsa_w_s, nsa_g_s, qlat_s, qpe_s, mla_new_s, ssd_z_s, ssd_xbc_s, ssd_dt_s,
         hg_s) = _inproj(xs, grow(2), cs_s, w_pack, mqn, mkvn, wuq_p, wukn)

        o_nsa = _nsa_prompt(nsa_q, nsa_pg, nsa_w, nsa_g, BP, T)
        q8 = jnp.pad(nsa_q_s.reshape(BS, HEADS, DK), ((0, 0), (0, SUBLANES - HEADS), (0, LANES - DK)))
        o_c, imp = _nsa_cmp_decode(l, page_table, q8, cache4, past)
        idx = _topk(imp.reshape(BS, 2 * LANES), past, k_top)
        o_nsa_s = _nsa_sel_decode(l, page_table, idx, q8, row3(nsa_pg_s), row3(nsa_w_s), win_cache,
                                  row3(nsa_g_s), o_c, cache_blk, k_top)

        o_mla = _mla_prompt(qlat, qpe, mla_new, wv, BP, T)
        pad8 = lambda a: jnp.pad(a, ((0, 0), (0, SUBLANES - HEADS), (0, 0)))
        qlat8 = pad8(qlat_s.reshape(BS, HEADS, LANES))
        qpe8 = pad8(qpe_s.reshape(BS, HEADS, LANES)[:, :, :MLA_ROPE])
        o_mla_s = _mla_decode(l, page_table, qlat8, qpe8, row3(mla_new_s), cache_mla, wv)

        cw = ssd_conv_w[l].astype(F32)
        cb = ssd_conv_b[l].reshape(1, -1).astype(F32)
        dtb = _pad_row(ssd_dt_bias[l])
        alog = _pad_row(ssd_a_log[l])
        dsk = _pad_row(ssd_d[l])
        sn = ssd_norm[l].reshape(1, -1).astype(F32)
        lb = lower_bounds[l].reshape(1, -1)
        hn = hg_norm[l].reshape(1, -1).astype(F32)
        o_ssd, ssd_p = _ssd_prompt(ssd_xbc, ssd_dt, ssd_z, cw, cb, dtb, alog, dsk, sn, BP, T)
        o_hg, hgrn_p = _hgrn_prompt(hg, lb, hn, BP, T)
        o_ssd_s, o_hg_s, ssd_s, hgrn_s = _rec_decode(
            l, row3(ssd_xbc_s), state_ssd_conv, row3(ssd_dt_s), row3(ssd_z_s), row3(hg_s), state_ssd,
            state_hgrn, cw, cb, dtb, alog, dsk, sn, lb, hn)

        wbr = w_branch[l].astype(BF16)
        wout = w_out[l].astype(BF16)
        xp = _merge(xp, (o_nsa, o_mla, o_ssd, o_hg), grow(2), grow(3), w_gate, wbr, wout)
        xs = _merge(xs, (o_nsa_s, o_mla_s, o_ssd_s, o_hg_s), grow(2), grow(3), w_gate, wbr, wout)
        xp = _ffn(xp, grow(4), grow(5), wi[1], wo[1])
        xs = _ffn(xs, grow(4), grow(5), wi[1], wo[1])

        n_keep = min(NSA_WINDOW, T)
        layer_out = (
            nsa_pg.reshape(BP, T, 4, DK), nsa_pg_s.reshape(BS, 1, 4, DK),
            mla_new.reshape(BP, T, MLA_CACHE), mla_new_s.reshape(BS, 1, MLA_CACHE),
            nsa_w.reshape(BP, T, 2, DK)[:, T - n_keep:],
            jnp.concatenate([cache_nsa_win[l], nsa_w_s.reshape(BS, 1, 2, DK)], axis=1)[:, 1:],
            ssd_p, ssd_s,
            ssd_xbc.reshape(BP, T, SSD_CONV_DIM)[:, T - (SSD_CONV - 1):],
            jnp.concatenate([state_ssd_conv[l], ssd_xbc_s.reshape(BS, 1, SSD_CONV_DIM)], axis=1)[:, 1:],
            hgrn_p, hgrn_s)
        for acc, o in zip(outs, layer_out):
            acc.append(o)

    stacked = tuple(jnp.stack(o) for o in outs)
    return (xp.reshape(BP, T, D), xs.reshape(BS, 1, D)) + stacked
```

```python
import functools
import math

import numpy as np
import jax
import jax.numpy as jnp
from jax import lax
from jax.experimental import pallas as pl
from jax.experimental.pallas import tpu as pltpu

F32 = jnp.float32
BF16 = jnp.bfloat16
I32 = jnp.int32
HIGHEST = lax.Precision.HIGHEST

EPS = 1e-6
NEG = -1e30
LB_FLOOR = 1e-20
ROPE_THETA = 10000.0

D_MODEL = 1024
D_FF = 2816
HEADS = 4
DK = 64
BRANCH = HEADS * DK
NSA_BLOCK = 64
NSA_TOPK = 16
NSA_WINDOW = 512
NSA_FORCED_BONUS = float(HEADS + 1)
MLA_Q_LORA = 256
MLA_KV_LORA = 128
MLA_NOPE = 64
MLA_ROPE = 32
MLA_V = 64
MLA_CACHE = MLA_KV_LORA + MLA_ROPE
SSD_GROUPS = 2
SSD_CONV = 4
SSD_CONV_DIM = BRANCH + 2 * SSD_GROUPS * DK
CHUNK = 128
QB = 128

LANES = 128
SUBLANES = 8
VMEM_LIMIT_BYTES = 56 * 1024 * 1024

_SEG = {}
_off = 0
for _name, _w in (("nsa_q", 256), ("nsa_pg", 256), ("nsa_w", 128), ("nsa_g", 128),
                  ("mla_cq", 256), ("mla_ckv", 128), ("mla_kpe", 128), ("mla_kpr", 128),
                  ("ssd_z", 256), ("ssd_xbc", 512), ("ssd_dt", 128), ("hg", 1024)):
    _SEG[_name] = (_off, _off + _w)
    _off += _w
PACKED_WIDTH = _off


def _cparams(sem):
    return pltpu.CompilerParams(dimension_semantics=sem, vmem_limit_bytes=VMEM_LIMIT_BYTES)


def _rms(x, g):
    return x * lax.rsqrt(jnp.mean(x * x, axis=-1, keepdims=True) + EPS) * g


def _dot(a, b):
    return jnp.dot(a.astype(BF16), b.astype(BF16), preferred_element_type=F32)


def _dot_nt(a, b):
    return lax.dot_general(a.astype(BF16), b.astype(BF16), (((1,), (1,)), ((), ())),
                           preferred_element_type=F32)


def _dot_f32(a, b):
    return jnp.dot(a, b, precision=HIGHEST, preferred_element_type=F32)


def _dot_nt_f32(a, b):
    return lax.dot_general(a, b, (((1,), (1,)), ((), ())), precision=HIGHEST,
                           preferred_element_type=F32)


def _sigmoid(x):
    return 1.0 / (1.0 + jnp.exp(-x))


def _silu(x):
    return x * _sigmoid(x)


def _softplus(x):
    return jnp.maximum(x, 0.0) + jnp.log1p(jnp.exp(-jnp.abs(x)))


def _masked_softmax(s, mask):
    s = jnp.where(mask, s, NEG)
    m = jnp.max(s, axis=-1, keepdims=True)
    e = jnp.exp(s - m)
    r = 1.0 / jnp.sum(e, axis=-1, keepdims=True)
    return jnp.where(mask, e * r, 0.0)


def _row_tile(rows, cap=864):
    best = SUBLANES
    for t in range(SUBLANES, min(rows, cap) + 1, SUBLANES):
        if rows % t == 0:
            best = t
    assert rows % best == 0
    return best


def _const_spec(shape):
    nd = len(shape)
    return pl.BlockSpec(shape, lambda *_: (0,) * nd)


def _ffn_kernel(x_ref, ga_ref, gb_ref, wg_ref, wu_ref, wo_ref, o_ref, hn_ref, acc_ref):
    j = pl.program_id(1)

    @pl.when(j == 0)
    def _():
        hn_ref[...] = _rms(x_ref[...], ga_ref[...]).astype(BF16)
        acc_ref[...] = jnp.zeros_like(acc_ref)

    h = hn_ref[...]
    gate = jnp.dot(h, wg_ref[...], preferred_element_type=F32)
    up = jnp.dot(h, wu_ref[...], preferred_element_type=F32)
    acc_ref[...] += _dot(_silu(gate) * up, wo_ref[...])

    @pl.when(j == pl.num_programs(1) - 1)
    def _():
        o_ref[...] = x_ref[...] + 0.5 * _rms(acc_ref[...], gb_ref[...])


def _ffn(x, ga, gb, wi, wo):
    rows = x.shape[0]
    tm = _row_tile(rows)
    fc = D_FF // 2
    nch = D_FF // fc
    return pl.pallas_call(
        _ffn_kernel,
        grid=(rows // tm, nch),
        in_specs=[
            pl.BlockSpec((tm, D_MODEL), lambda i, j: (i, 0)),
            _const_spec((1, D_MODEL)),
            _const_spec((1, D_MODEL)),
            pl.BlockSpec((D_MODEL, fc), lambda i, j: (0, j)),
            pl.BlockSpec((D_MODEL, fc), lambda i, j: (0, nch + j)),
            pl.BlockSpec((fc, D_MODEL), lambda i, j: (j, 0)),
        ],
        out_specs=pl.BlockSpec((tm, D_MODEL), lambda i, j: (i, 0)),
        out_shape=jax.ShapeDtypeStruct((rows, D_MODEL), F32),
        scratch_shapes=[pltpu.VMEM((tm, D_MODEL), BF16), pltpu.VMEM((tm, D_MODEL), F32)],
        compiler_params=_cparams(("parallel", "arbitrary")),
        name="ffn",
    )(x, ga, gb, wi, wi, wo)


def _inproj_kernel(x_ref, g_ref, cs_ref, w_ref, mqn_ref, mkvn_ref, wuq_ref, wukn_ref,
                   q_o, pg_o, w_o, g_o, qlat_o, qpe_o, mla_o, z_o, xbc_o, dt_o, hg_o):
    hn = _rms(x_ref[...], g_ref[...]).astype(BF16)

    def seg(name):
        a, b = _SEG[name]
        return jnp.dot(hn, w_ref[:, a:b], preferred_element_type=F32)

    q_o[...] = seg("nsa_q") * (DK ** -0.5)
    pg_o[...] = seg("nsa_pg")
    w_o[...] = seg("nsa_w")
    g_o[...] = _sigmoid(seg("nsa_g"))
    z_o[...] = seg("ssd_z")
    xbc_o[...] = seg("ssd_xbc")
    dt_o[...] = seg("ssd_dt")
    hg_o[...] = seg("hg")

    cos_p = cs_ref[:, 0:LANES]
    sin_p = cs_ref[:, LANES:2 * LANES]
    cqn = _rms(seg("mla_cq"), mqn_ref[...])
    qf = _dot(cqn, wuq_ref[...])
    nope_w = HEADS * MLA_NOPE
    qlat_o[...] = _dot(qf[:, 0:nope_w], wukn_ref[...])
    for h in range(HEADS):
        pe = qf[:, nope_w + LANES * h: nope_w + LANES * (h + 1)]
        pr = qf[:, nope_w + LANES * (HEADS + h): nope_w + LANES * (HEADS + h + 1)]
        qpe_o[:, LANES * h:LANES * (h + 1)] = pe * cos_p + pr * sin_p
    mla_o[:, 0:MLA_KV_LORA] = _rms(seg("mla_ckv"), mkvn_ref[...])
    kpe = seg("mla_kpe") * cos_p + seg("mla_kpr") * sin_p
    mla_o[:, MLA_KV_LORA:MLA_CACHE] = kpe[:, 0:MLA_ROPE]


_INPROJ_OUT_WIDTHS = (256, 256, 128, 128, 512, 512, MLA_CACHE, 256, 512, 128, 1024)


def _inproj(x, g, cs, w, mqn, mkvn, wuq, wukn):
    rows = x.shape[0]
    tm = _row_tile(rows)
    row_spec = lambda c: pl.BlockSpec((tm, c), lambda i: (i, 0))
    return pl.pallas_call(
        _inproj_kernel,
        grid=(rows // tm,),
        in_specs=[row_spec(D_MODEL), _const_spec((1, D_MODEL)), row_spec(2 * LANES),
                  _const_spec(w.shape), _const_spec(mqn.shape), _const_spec(mkvn.shape),
                  _const_spec(wuq.shape), _const_spec(wukn.shape)],
        out_specs=[row_spec(c) for c in _INPROJ_OUT_WIDTHS],
        out_shape=[jax.ShapeDtypeStruct((rows, c), F32) for c in _INPROJ_OUT_WIDTHS],
        compiler_params=_cparams(("parallel",)),
        name="inproj",
    )(x, g, cs, w, mqn, mkvn, wuq, wukn)


def _merge_kernel(x_ref, b0_ref, b1_ref, b2_ref, b3_ref, g2_ref, g3_ref, wg_ref, wbr_ref, wout_ref,
                  o_ref):
    x = x_ref[...]
    hn = _rms(x, g2_ref[...]).astype(BF16)
    acc = None
    for k, b_ref in enumerate((b0_ref, b1_ref, b2_ref, b3_ref)):
        gate = _sigmoid(jnp.dot(hn, wg_ref[:, D_MODEL * k:D_MODEL * (k + 1)],
                                preferred_element_type=F32))
        u = _dot(b_ref[...], wbr_ref[k])
        acc = gate * u if acc is None else acc + gate * u
    out = _dot(acc, wout_ref[...])
    o_ref[...] = x + _rms(out, g3_ref[...])


def _merge(x, branches, g2, g3, wgate, wbr, wout):
    rows = x.shape[0]
    tm = _row_tile(rows, cap=512)
    row_spec = lambda c: pl.BlockSpec((tm, c), lambda i: (i, 0))
    return pl.pallas_call(
        _merge_kernel,
        grid=(rows // tm,),
        in_specs=[row_spec(D_MODEL)] + [row_spec(BRANCH)] * 4 +
                 [_const_spec((1, D_MODEL)), _const_spec((1, D_MODEL)),
                  _const_spec(wgate.shape), _const_spec(wbr.shape), _const_spec(wout.shape)],
        out_specs=row_spec(D_MODEL),
        out_shape=jax.ShapeDtypeStruct((rows, D_MODEL), F32),
        compiler_params=_cparams(("parallel",)),
        name="merge",
    )(x, *branches, g2, g3, wgate, wbr, wout)


KEY_STEP = 512


def _key_extents(T):
    his = sorted({min(T, KEY_STEP * (c + 1)) for c in range(-(-T // KEY_STEP))})
    return list(zip([0] + his[:-1], his))


def _stack_heads(x, width, take):
    return jnp.concatenate([x[:, width * h: width * h + take] for h in range(HEADS)], axis=0)


def _nsa_prompt_kernel(q_ref, pg_ref, wk_ref, g_ref, e_ref, o_ref, *, T, band):
    nb = T // NSA_BLOCK
    k_top = min(NSA_TOPK, nb)
    q0 = pl.program_id(1) * QB
    q = q_ref[...]
    zeros = jnp.zeros((QB, DK), F32)
    q_rows = jnp.concatenate(
        [jnp.concatenate([q[:, DK * h:DK * (h + 1)], zeros], axis=1) for h in range(HEADS)],
        axis=0).astype(BF16)
    pos = q0 + lax.broadcasted_iota(I32, (QB, 1), 0)
    lane = lax.broadcasted_iota(I32, (QB, LANES), 1)

    def per_head(fn):
        return jnp.concatenate([fn(h) for h in range(HEADS)], axis=0)

    cm = jnp.sum(pg_ref[:, 0:LANES].reshape(nb, NSA_BLOCK, LANES), axis=1) * (1.0 / NSA_BLOCK)
    if nb < LANES:
        cm = jnp.concatenate([cm, jnp.zeros((LANES - nb, LANES), F32)], axis=0)
    cm = cm.astype(BF16)
    s_c = _dot_nt(q_rows, cm)
    cmask = (lane + 1) * NSA_BLOCK - 1 <= pos
    p_c = per_head(lambda h: _masked_softmax(s_c[QB * h:QB * (h + 1)], cmask))
    o_c = _dot(p_c, cm)

    cur = pos // NSA_BLOCK
    imp = p_c[0:QB] + p_c[QB:2 * QB] + p_c[2 * QB:3 * QB] + p_c[3 * QB:4 * QB]
    forced = (lane == 0) | (lane == cur) | (lane == cur - 1)
    imp = imp + jnp.where(forced, NSA_FORCED_BONUS, 0.0)
    imp = jnp.where(lane > cur, -1.0, imp)
    rank = jnp.zeros((QB, LANES), F32)
    for n in range(nb):
        c = imp[:, n:n + 1]
        beats = (c > imp) | ((c == imp) & (lane > n))
        rank = rank + jnp.where(beats, 1.0, 0.0)
    sel = jnp.where(rank < float(k_top), 1.0, 0.0)

    def selected(tk):
        in_sel = jnp.dot(sel.astype(BF16), e_ref[:, 0:tk], preferred_element_type=F32)
        kpos = lax.broadcasted_iota(I32, (QB, tk), 1)
        smask = (in_sel > 0.5) & (kpos <= pos)
        kv_s = pg_ref[0:tk, LANES:2 * LANES].astype(BF16)
        s_s = _dot_nt(q_rows, kv_s)
        p_s = per_head(lambda h: _masked_softmax(s_s[QB * h:QB * (h + 1)], smask))
        return _dot(p_s, kv_s)

    start = pl.multiple_of(jnp.clip(q0 - NSA_WINDOW, 0, T - band), LANES)
    kv_w = wk_ref[pl.ds(start, band), :].astype(BF16)
    kposw = start + lax.broadcasted_iota(I32, (QB, band), 1)
    wmask = (kposw <= pos) & (kposw >= pos - NSA_WINDOW)
    s_w = _dot_nt(q_rows, kv_w)
    p_w = per_head(lambda h: _masked_softmax(s_w[QB * h:QB * (h + 1)], wmask))
    o_w = _dot(p_w, kv_w)

    g = g_ref[...]

    def finish(o_s):
        outs = []
        for h in range(HEADS):
            r = slice(QB * h, QB * (h + 1))
            comb = (g[:, 3 * h:3 * h + 1] * o_c[r] + g[:, 3 * h + 1:3 * h + 2] * o_s[r]
                    + g[:, 3 * h + 2:3 * h + 3] * o_w[r])
            outs.append(comb[:, DK:2 * DK])
        o_ref[...] = jnp.concatenate(outs, axis=1)

    for lo, tk in _key_extents(T):
        @pl.when((q0 + QB > lo) & (q0 + QB <= tk))
        def _(tk=tk):
            finish(selected(tk))


def _nsa_prompt(q, pg, wkv, g, B, T):
    assert T % QB == 0 and T // NSA_BLOCK <= LANES
    rows = B * T
    nq = T // QB
    band = min(NSA_WINDOW + QB, T)
    nb = T // NSA_BLOCK
    expand = (np.arange(LANES)[:, None] == (np.arange(T)[None, :] // NSA_BLOCK)) & (np.arange(LANES)[:, None] < nb)
    expand = jnp.asarray(expand, BF16)
    return pl.pallas_call(
        functools.partial(_nsa_prompt_kernel, T=T, band=band),
        grid=(B, nq),
        in_specs=[pl.BlockSpec((QB, BRANCH), lambda b, i: (b * nq + i, 0)),
                  pl.BlockSpec((T, 256), lambda b, i: (b, 0)),
                  pl.BlockSpec((T, LANES), lambda b, i: (b, 0)),
                  pl.BlockSpec((QB, LANES), lambda b, i: (b * nq + i, 0)),
                  _const_spec((LANES, T))],
        out_specs=pl.BlockSpec((QB, BRANCH), lambda b, i: (b * nq + i, 0)),
        out_shape=jax.ShapeDtypeStruct((rows, BRANCH), F32),
        compiler_params=_cparams(("parallel", "parallel")),
        name="nsa_prompt",
    )(q, pg, wkv, g, expand)


def _mla_prompt_kernel(qlat_ref, qpe_ref, kv_ref, wv_ref, o_ref, *, T):
    q0 = pl.program_id(1) * QB
    ql = _stack_heads(qlat_ref[...], LANES, LANES)
    qp = _stack_heads(qpe_ref[...], LANES, MLA_ROPE)
    scale = (MLA_NOPE + MLA_ROPE) ** -0.5
    pos = q0 + lax.broadcasted_iota(I32, (QB, 1), 0)

    def attend(tk):
        ckv = kv_ref[0:tk, 0:MLA_KV_LORA].astype(BF16)
        kpe = kv_ref[0:tk, MLA_KV_LORA:MLA_CACHE].astype(BF16)
        s = (_dot_nt(ql, ckv) + _dot_nt(qp, kpe)) * scale
        mask = lax.broadcasted_iota(I32, (QB, tk), 1) <= pos
        p = jnp.concatenate([_masked_softmax(s[QB * h:QB * (h + 1)], mask) for h in range(HEADS)],
                            axis=0)
        o_lat = _dot(p, ckv)
        o_cat = jnp.concatenate([o_lat[QB * h:QB * (h + 1)] for h in range(HEADS)], axis=1)
        o_ref[...] = _dot(o_cat, wv_ref[...])

    for lo, tk in _key_extents(T):
        @pl.when((q0 + QB > lo) & (q0 + QB <= tk))
        def _(tk=tk):
            attend(tk)


def _mla_prompt(qlat, qpe, mla_new, wv, B, T):
    nq = T // QB
    rows = B * T
    return pl.pallas_call(
        functools.partial(_mla_prompt_kernel, T=T),
        grid=(B, nq),
        in_specs=[pl.BlockSpec((QB, 512), lambda b, i: (b * nq + i, 0)),
                  pl.BlockSpec((QB, 512), lambda b, i: (b * nq + i, 0)),
                  pl.BlockSpec((T, MLA_CACHE), lambda b, i: (b, 0)),
                  _const_spec(wv.shape)],
        out_specs=pl.BlockSpec((QB, BRANCH), lambda b, i: (b * nq + i, 0)),
        out_shape=jax.ShapeDtypeStruct((rows, BRANCH), F32),
        compiler_params=_cparams(("parallel", "parallel")),
        name="mla_prompt",
    )(qlat, qpe, mla_new, wv)


def _head_lanes(x_cols, shape):
    lane = lax.broadcasted_iota(I32, shape, 1)
    out = jnp.broadcast_to(x_cols[:, HEADS - 1:HEADS], shape)
    for h in range(HEADS - 2, -1, -1):
        out = jnp.where(lane < DK * (h + 1), jnp.broadcast_to(x_cols[:, h:h + 1], shape), out)
    return out


def _ssd_prompt_kernel(xbc_ref, dt_ref, z_ref, cw_ref, cb_ref, dtb_ref, alog_ref, dsk_ref, sn_ref,
                       o_ref, st_ref, buf_ref, s_ref):
    c = pl.program_id(1)
    C = CHUNK

    @pl.when(c == 0)
    def _():
        buf_ref[0:SUBLANES, :] = jnp.zeros((SUBLANES, SSD_CONV_DIM), F32)
        s_ref[...] = jnp.zeros_like(s_ref)

    x = xbc_ref[...]
    buf_ref[SUBLANES:SUBLANES + C, :] = x
    conv = cb_ref[...] + cw_ref[SSD_CONV - 1:SSD_CONV, :] * x
    for j in range(SSD_CONV - 1):
        conv = conv + cw_ref[j:j + 1, :] * buf_ref[pl.ds(SUBLANES - (SSD_CONV - 1) + j, C), :]
    buf_ref[0:SUBLANES, :] = x[C - SUBLANES:C, :]
    xa = _silu(conv)
    xs = xa[:, 0:BRANCH]
    dt = _softplus(dt_ref[...] + dtb_ref[...])
    a = dt * (-jnp.exp(alog_ref[...]))
    row = lax.broadcasted_iota(I32, (C, C), 0)
    col = lax.broadcasted_iota(I32, (C, C), 1)
    tril = row >= col
    tri = jnp.where(tril, 1.0, 0.0)
    cum_c = _dot_f32(tri, a)
    cum_r = _dot_nt_f32(a.T, tri)
    xdt = xs * _head_lanes(dt, (C, BRANCH))
    xdt_t = xdt.T
    dsk = dsk_ref[...]
    ys = []
    for h in range(HEADS):
        g = h // (HEADS // SSD_GROUPS)
        bm = xa[:, BRANCH + DK * g:BRANCH + DK * (g + 1)]
        cm = xa[:, BRANCH + DK * (SSD_GROUPS + g):BRANCH + DK * (SSD_GROUPS + g + 1)]
        ch = cum_c[:, h:h + 1]
        decay = jnp.exp(jnp.where(tril, ch - cum_r[h:h + 1, :], NEG))
        scores = _dot_nt(cm, bm) * decay
        xdt_h = xdt[:, DK * h:DK * (h + 1)]
        s_old = s_ref[h]
        y = _dot(scores, xdt_h) + _dot_nt(cm, s_old) * jnp.exp(ch)
        last = cum_c[C - 1:C, h:h + 1]
        w = jnp.exp(last - ch)
        s_ref[h] = s_old * jnp.exp(last) + _dot(xdt_t[DK * h:DK * (h + 1), :], bm * w)
        ys.append(y + dsk[:, h:h + 1] * xs[:, DK * h:DK * (h + 1)])
    y = jnp.concatenate(ys, axis=1)
    o_ref[...] = _rms(y * _silu(z_ref[...]), sn_ref[...])

    @pl.when(c == pl.num_programs(1) - 1)
    def _():
        st_ref[0] = s_ref[...]


def _ssd_prompt(xbc, dt, z, cw, cb, dtb, alog, dsk, sn, B, T):
    assert T % CHUNK == 0
    nc = T // CHUNK
    rows = B * T
    row_spec = lambda w: pl.BlockSpec((CHUNK, w), lambda b, c: (b * nc + c, 0))
    return pl.pallas_call(
        _ssd_prompt_kernel,
        grid=(B, nc),
        in_specs=[row_spec(SSD_CONV_DIM), row_spec(LANES), row_spec(BRANCH),
                  _const_spec(cw.shape), _const_spec(cb.shape), _const_spec(dtb.shape),
                  _const_spec(alog.shape), _const_spec(dsk.shape), _const_spec(sn.shape)],
        out_specs=[row_spec(BRANCH), pl.BlockSpec((1, HEADS, DK, DK), lambda b, c: (b, 0, 0, 0))],
        out_shape=[jax.ShapeDtypeStruct((rows, BRANCH), F32),
                   jax.ShapeDtypeStruct((B, HEADS, DK, DK), F32)],
        scratch_shapes=[pltpu.VMEM((SUBLANES + CHUNK, SSD_CONV_DIM), F32),
                        pltpu.VMEM((HEADS, DK, DK), F32)],
        compiler_params=_cparams(("parallel", "arbitrary")),
        name="ssd_prompt",
    )(xbc, dt, z, cw, cb, dtb, alog, dsk, sn)


def _hgrn_gates(fr, lb):
    log_sig = jnp.minimum(fr, 0.0) - jnp.log1p(jnp.exp(-jnp.abs(fr)))
    a = jnp.log(jnp.maximum(lb, LB_FLOOR))
    b = jnp.log1p(-lb) + log_sig
    log_f = jnp.maximum(a, b) + jnp.log1p(jnp.exp(-jnp.abs(a - b)))
    return log_f, (1.0 - lb) * _sigmoid(-fr)


def _block_reference_rows(G, m):
    C = G.shape[0]
    if m >= 4:
        parts = []
        for p in range(C // (2 * m)):
            r = p * 2 * m + m - 1
            parts.append(jnp.broadcast_to(G[r:r + 1, :], (2 * m, G.shape[1])))
        return parts[0] if len(parts) == 1 else jnp.concatenate(parts, axis=0)
    t = lax.broadcasted_iota(I32, G.shape, 0)
    if m == 1:
        return jnp.where((t & 1) == 1, pltpu.roll(G, 1, 0), G)
    r = t & 3
    return jnp.where(r == 0, pltpu.roll(G, C - 1, 0),
                     jnp.where(r == 1, G, jnp.where(r == 2, pltpu.roll(G, 1, 0), pltpu.roll(G, 2, 0))))


def _hgrn_prompt_kernel(hg_ref, lb_ref, hn_ref, seg_ref, o_ref, st_ref, s_ref):
    c = pl.program_id(1)
    C = CHUNK
    W = BRANCH

    @pl.when(c == 0)
    def _():
        s_ref[...] = jnp.zeros_like(s_ref)

    q = hg_ref[:, 0:W]
    v = hg_ref[:, 2 * W:3 * W]
    log_f, kin = _hgrn_gates(hg_ref[:, W:2 * W], lb_ref[...])
    row = lax.broadcasted_iota(I32, (C, C), 0)
    col = lax.broadcasted_iota(I32, (C, C), 1)
    G = _dot_f32(jnp.where(row >= col, 1.0, 0.0), log_f)
    lane_head = lax.broadcasted_iota(I32, (C, W), 1) // DK
    t_idx = lax.broadcasted_iota(I32, (C, W), 0)

    def stack_heads(a):
        return jnp.concatenate([jnp.where(lane_head == h, a, 0.0) for h in range(HEADS)],
                               axis=0).astype(BF16)

    t4 = lax.broadcasted_iota(I32, (HEADS * C, C), 0) & (C - 1)
    s4 = lax.broadcasted_iota(I32, (HEADS * C, C), 1)
    att = jnp.where(t4 == s4, _dot_nt(stack_heads(q), kin), 0.0)
    m = C // 2
    while m >= 1:
        R = _block_reference_rows(G, m)
        upper = (t_idx & m) != 0
        A = jnp.where(upper, q * jnp.exp(jnp.minimum(G - R, 0.0)), 0.0)
        Bm = jnp.where(upper, 0.0, kin * jnp.exp(jnp.minimum(R - G, 0.0)))
        lm = int(math.log2(m))
        pair = ((t4 >> lm) ^ (s4 >> lm)) == 1
        att = att + jnp.where(pair & (t4 > s4), _dot_nt(stack_heads(A), Bm), 0.0)
        m //= 2
    y4 = _dot(att, v)
    y = jnp.zeros((C, W), F32)
    for h in range(HEADS):
        y = y + jnp.where(lane_head == h, y4[C * h:C * (h + 1)], 0.0)
    s_old = s_ref[...]
    y = y + _dot(q * jnp.exp(G), s_old)
    last = G[C - 1:C, :]
    ke_t = (kin * jnp.exp(last - G)).T
    G_t = G.T
    blk = (lax.broadcasted_iota(I32, (W, W), 0) // DK) == (lax.broadcasted_iota(I32, (W, W), 1) // DK)
    s_ref[...] = s_old * jnp.exp(G_t[:, C - 1:C]) + jnp.where(blk, _dot(ke_t, v), 0.0)
    ms = _dot_f32(y * y, seg_ref[...])
    o = y * lax.rsqrt(ms + EPS) * hn_ref[...]
    o_ref[...] = o * _silu(hg_ref[:, 3 * W:4 * W])

    @pl.when(c == pl.num_programs(1) - 1)
    def _():
        for h in range(HEADS):
            st_ref[0, h] = s_ref[DK * h:DK * (h + 1), DK * h:DK * (h + 1)]


def _hgrn_prompt(hg, lb, hn, B, T):
    nc = T // CHUNK
    rows = B * T
    seg = (np.arange(BRANCH)[:, None] // DK == np.arange(BRANCH)[None, :] // DK) / float(DK)
    seg = jnp.asarray(seg, F32)
    row_spec = lambda w: pl.BlockSpec((CHUNK, w), lambda b, c: (b * nc + c, 0))
    return pl.pallas_call(
        _hgrn_prompt_kernel,
        grid=(B, nc),
        in_specs=[row_spec(4 * BRANCH), _const_spec(lb.shape), _const_spec(hn.shape),
                  _const_spec(seg.shape)],
        out_specs=[row_spec(BRANCH), pl.BlockSpec((1, HEADS, DK, DK), lambda b, c: (b, 0, 0, 0))],
        out_shape=[jax.ShapeDtypeStruct((rows, BRANCH), F32),
                   jax.ShapeDtypeStruct((B, HEADS, DK, DK), F32)],
        scratch_shapes=[pltpu.VMEM((BRANCH, BRANCH), F32)],
        compiler_params=_cparams(("parallel", "arbitrary")),
        name="hgrn_prompt",
    )(hg, lb, hn, seg)


PAGES_PER_STEP = 16


def _mla_decode_kernel(pt_ref, q_ref, qt_ref, new_ref, *rest, n_pg):
    page_refs = rest[:n_pg]
    wv_ref, o_ref, qb_ref, m_ref, l_ref, acc_ref = rest[n_pg:]
    del pt_ref
    b = pl.program_id(0)
    j = pl.program_id(1)
    scale = (MLA_NOPE + MLA_ROPE) ** -0.5
    page = page_refs[0].shape[1]

    @pl.when(j == 0)
    def _():
        qt = qt_ref[0]
        for h in range(HEADS):
            qb_ref[h] = jnp.broadcast_to(qt[:, h:h + 1], (MLA_CACHE, LANES))
        m_ref[...] = jnp.full_like(m_ref, NEG)
        l_ref[...] = jnp.zeros_like(l_ref)
        acc_ref[...] = jnp.zeros_like(acc_ref)

    kt = jnp.concatenate([r[...] for r in page_refs], axis=1)
    ckt = kt[0:MLA_KV_LORA]
    for h in range(HEADS):
        qb = jnp.concatenate([qb_ref[h]] * n_pg, axis=1)
        s = jnp.sum(kt * qb, axis=0, keepdims=True) * scale
        m_old = m_ref[h:h + 1, :]
        m_new = jnp.maximum(m_old, jnp.max(s, axis=-1, keepdims=True))
        alpha = jnp.exp(m_old - m_new)
        p = jnp.exp(s - m_new[:, 0:1])
        l_ref[h:h + 1, :] = alpha * l_ref[h:h + 1, :] + jnp.sum(p, axis=-1, keepdims=True)
        m_ref[h:h + 1, :] = m_new
        w = ckt * p
        part = w[:, 0:page]
        for t in range(1, n_pg):
            part = part + w[:, t * page:(t + 1) * page]
        acc_ref[h] = alpha[:, 0:1] * acc_ref[h] + part

    @pl.when(j == pl.num_programs(1) - 1)
    def _():
        new = new_ref[0]
        s_new = jnp.sum(q_ref[0] * new, axis=-1, keepdims=True) * scale
        ones = jnp.ones((SUBLANES, LANES), F32)
        outs = []
        for h in range(HEADS):
            m_old = m_ref[h:h + 1, :]
            m_fin = jnp.maximum(m_old, s_new[h:h + 1, :])
            alpha = jnp.exp(m_old - m_fin)
            p_n = jnp.exp(s_new[h:h + 1, :] - m_fin)
            l = alpha * l_ref[h:h + 1, :] + p_n
            lat = _dot_nt_f32(ones, acc_ref[h])[0:1, :]
            outs.append((alpha * lat + p_n * new[:, 0:MLA_KV_LORA]) / l)
        o_cat = jnp.concatenate(outs, axis=1)
        o = _dot(jnp.broadcast_to(o_cat, (SUBLANES, HEADS * MLA_KV_LORA)), wv_ref[...])
        o_ref[pl.ds(b, 1), :] = o[0:1, :]


def _mla_decode(layer, page_table, q8, qt, new_rows, cache_t, wv):
    BS, npages = page_table.shape
    page = cache_t.shape[3]
    n_pg = min(PAGES_PER_STEP, npages)
    assert npages % n_pg == 0 and page == LANES
    steps = npages // n_pg

    def page_spec(i):
        return pl.BlockSpec((None, None, MLA_CACHE, page),
                            lambda b, j, pt: (layer, pt[b, j * n_pg + i], 0, 0))

    grid_spec = pltpu.PrefetchScalarGridSpec(
        num_scalar_prefetch=1,
        grid=(BS, steps),
        in_specs=[pl.BlockSpec((1, SUBLANES, MLA_CACHE), lambda b, j, pt: (b, 0, 0)),
                  pl.BlockSpec((1, MLA_CACHE, HEADS), lambda b, j, pt: (b, 0, 0)),
                  pl.BlockSpec((1, 1, MLA_CACHE), lambda b, j, pt: (b, 0, 0))]
                 + [page_spec(i) for i in range(n_pg)]
                 + [pl.BlockSpec(wv.shape, lambda b, j, pt: (0, 0))],
        out_specs=pl.BlockSpec((BS, BRANCH), lambda b, j, pt: (0, 0)),
        scratch_shapes=[pltpu.VMEM((HEADS, MLA_CACHE, LANES), F32),
                        pltpu.VMEM((SUBLANES, LANES), F32), pltpu.VMEM((SUBLANES, LANES), F32),
                        pltpu.VMEM((HEADS, MLA_KV_LORA, LANES), F32)],
    )
    return pl.pallas_call(
        functools.partial(_mla_decode_kernel, n_pg=n_pg),
        grid_spec=grid_spec,
        out_shape=jax.ShapeDtypeStruct((BS, BRANCH), F32),
        compiler_params=_cparams(("arbitrary", "arbitrary")),
        name="mla_decode",
    )(page_table, q8, qt, new_rows, *([cache_t] * n_pg), wv)


def _nsa_cmp_decode_kernel(pt_ref, q_ref, *rest, n_pg, past):
    page_refs = rest[:n_pg]
    oc_ref, imp_ref, cm_ref = rest[n_pg:]
    del pt_ref
    j = pl.program_id(1)

    @pl.when(j == 0)
    def _():
        cm_ref[...] = jnp.zeros_like(cm_ref)

    lane = lax.broadcasted_iota(I32, (2 * DK, LANES), 1)
    first = lane < NSA_BLOCK
    col0 = j * (2 * n_pg)
    tile = col0 // LANES
    off = col0 % LANES
    table = cm_ref[tile]
    for i, r in enumerate(page_refs):
        x = r[...]
        m0 = jnp.sum(jnp.where(first, x, 0.0), axis=1, keepdims=True) * (1.0 / NSA_BLOCK)
        m1 = jnp.sum(jnp.where(first, 0.0, x), axis=1, keepdims=True) * (1.0 / NSA_BLOCK)
        table = jnp.where(lane == off + 2 * i, m0, jnp.where(lane == off + 2 * i + 1, m1, table))
    cm_ref[tile] = table

    @pl.when(j == pl.num_programs(1) - 1)
    def _():
        q = q_ref[0]
        cm = jnp.concatenate([cm_ref[t] for t in range(cm_ref.shape[0])], axis=1)
        kmt = cm[0:DK].astype(BF16)
        vmt = cm[DK:2 * DK].astype(BF16)
        s = _dot(q, kmt)
        blk = lax.broadcasted_iota(I32, s.shape, 1)
        cmask = (blk + 1) * NSA_BLOCK - 1 <= past
        p = _masked_softmax(s, cmask)
        oc_ref[0] = _dot_nt(p, vmt)
        cur = past // NSA_BLOCK
        imp = jnp.sum(p[0:HEADS], axis=0, keepdims=True)
        blk1 = blk[0:1]
        forced = (blk1 == 0) | (blk1 == cur - 1)
        imp = imp + jnp.where(forced, NSA_FORCED_BONUS, 0.0)
        imp_ref[0] = jnp.where(blk1 >= cur, -3.0, imp)


def _nsa_cmp_decode(layer, page_table, q8, cache_t, past):
    BS, npages = page_table.shape
    page = cache_t.shape[3]
    n_pg = min(PAGES_PER_STEP, npages)
    ntiles = 2
    assert page == 2 * NSA_BLOCK and page == LANES and 2 * npages <= ntiles * LANES
    assert npages % n_pg == 0 and LANES % (2 * n_pg) == 0
    steps = npages // n_pg

    def page_spec(i):
        return pl.BlockSpec((None, None, 2 * DK, page),
                            lambda b, j, pt: (layer, pt[b, j * n_pg + i], 0, 0))

    grid_spec = pltpu.PrefetchScalarGridSpec(
        num_scalar_prefetch=1,
        grid=(BS, steps),
        in_specs=[pl.BlockSpec((1, SUBLANES, DK), lambda b, j, pt: (b, 0, 0))]
                 + [page_spec(i) for i in range(n_pg)],
        out_specs=[pl.BlockSpec((1, SUBLANES, DK), lambda b, j, pt: (b, 0, 0)),
                   pl.BlockSpec((1, 1, ntiles * LANES), lambda b, j, pt: (b, 0, 0))],
        scratch_shapes=[pltpu.VMEM((ntiles, 2 * DK, LANES), F32)],
    )
    return pl.pallas_call(
        functools.partial(_nsa_cmp_decode_kernel, n_pg=n_pg, past=past),
        grid_spec=grid_spec,
        out_shape=[jax.ShapeDtypeStruct((BS, SUBLANES, DK), F32),
                   jax.ShapeDtypeStruct((BS, 1, ntiles * LANES), F32)],
        compiler_params=_cparams(("parallel", "arbitrary")),
        name="nsa_cmp_decode",
    )(page_table, q8, *([cache_t] * n_pg))


def _topk_kernel(imp_ref, idx_ref, *, past, k_top):
    BS = imp_ref.shape[0]
    cur = past // NSA_BLOCK
    blk = lax.broadcasted_iota(I32, (BS, 2 * LANES), 1).astype(F32)
    lane_x = lax.broadcasted_iota(I32, (BS, LANES), 1)
    imp = jnp.concatenate([imp_ref[...], jnp.where(lane_x == 0, NSA_FORCED_BONUS, -3.0)], axis=1)
    blk = jnp.concatenate([blk, (cur + lane_x).astype(F32)], axis=1)
    out = jnp.zeros((BS, LANES), F32)
    for k in range(k_top):
        m = jnp.max(imp, axis=-1, keepdims=True)
        pick = jnp.min(jnp.where(imp == m, blk, 1e9), axis=-1, keepdims=True)
        imp = jnp.where(blk == pick, -4.0, imp)
        out = jnp.where(lane_x == k, pick, out)
    idx_ref[...] = out.astype(I32)


def _topk(imp, past, k_top):
    BS = imp.shape[0]
    return pl.pallas_call(
        functools.partial(_topk_kernel, past=past, k_top=k_top),
        out_shape=jax.ShapeDtypeStruct((BS, LANES), I32),
        name="nsa_topk",
    )(imp)


def _nsa_sel_decode_kernel(pt_ref, idx_ref, q_ref, pgn_ref, wn_ref, win_ref, g_ref, oc_ref, *rest,
                           k_top, n_blocks):
    blk_refs = rest[:k_top]
    (o_ref,) = rest[k_top:]
    del pt_ref
    b = pl.program_id(0)
    q = q_ref[0]
    page = blk_refs[0].shape[1]

    def attend(kt, vt, s_mask, k_new, v_new, new_on):
        s = _dot(q, kt)
        if s_mask is not None:
            s = jnp.where(s_mask, s, NEG)
        s_n = jnp.where(new_on, jnp.sum(q * k_new, axis=-1, keepdims=True), NEG)
        m = jnp.maximum(jnp.max(s, axis=-1, keepdims=True), s_n)
        e = jnp.exp(s - m)
        if s_mask is not None:
            e = jnp.where(s_mask, e, 0.0)
        e_n = jnp.where(new_on, jnp.exp(s_n - m), 0.0)
        l = jnp.sum(e, axis=-1, keepdims=True) + e_n
        return (_dot_nt(e, vt) + e_n * v_new) / l

    kt = jnp.concatenate([r[0:DK, :] for r in blk_refs], axis=1).astype(BF16)
    vt = jnp.concatenate([r[DK:2 * DK, :] for r in blk_refs], axis=1).astype(BF16)
    lane = lax.broadcasted_iota(I32, (1, k_top * page), 1)
    slot = lane // page
    half = (lane % page) // NSA_BLOCK
    valid = jnp.zeros((1, k_top * page), I32)
    has_new = jnp.zeros((), jnp.bool_)
    for k in range(k_top):
        ik = idx_ref[b, k]
        ok = jnp.where(half == (ik & 1), (ik < n_blocks).astype(I32), 0)
        valid = jnp.where(slot == k, ok, valid)
        has_new = has_new | (ik == n_blocks)
    pgn = pgn_ref[0]
    o_s = attend(kt, vt, valid > 0, pgn[:, 2 * DK:3 * DK], pgn[:, 3 * DK:4 * DK], has_new)
    wn = wn_ref[0]
    o_w = attend(win_ref[0:DK, :].astype(BF16), win_ref[DK:2 * DK, :].astype(BF16), None,
                 wn[:, 0:DK], wn[:, DK:2 * DK], True)
    o_c = oc_ref[0]
    g = g_ref[0]
    outs = []
    for h in range(HEADS):
        outs.append(g[:, 3 * h:3 * h + 1] * o_c[h:h + 1] + g[:, 3 * h + 1:3 * h + 2] * o_s[h:h + 1]
                    + g[:, 3 * h + 2:3 * h + 3] * o_w[h:h + 1])
    o_ref[pl.ds(b, 1), :] = jnp.concatenate(outs, axis=1)


def _nsa_sel_decode(layer, page_table, idx, q8, pg_new, w_new, win_t, g_new, o_c, cache_t, k_top):
    BS, npages = page_table.shape
    n_blocks = 2 * npages
    page = cache_t.shape[3]
    nwin = win_t.shape[3]
    blk = jnp.minimum(idx[:, :k_top], n_blocks - 1)
    phys = jnp.take_along_axis(page_table, blk // 2, axis=1)

    def blk_spec(k):
        return pl.BlockSpec((None, None, 2 * DK, page), lambda b, ph, ix: (layer, ph[b, k], 1, 0))

    row3 = lambda w: pl.BlockSpec((1, 1, w), lambda b, pt, ix: (b, 0, 0))
    grid_spec = pltpu.PrefetchScalarGridSpec(
        num_scalar_prefetch=2,
        grid=(BS,),
        in_specs=[pl.BlockSpec((1, SUBLANES, DK), lambda b, pt, ix: (b, 0, 0)),
                  row3(256), row3(LANES),
                  pl.BlockSpec((None, None, 2 * DK, nwin), lambda b, pt, ix: (layer, b, 0, 0)),
                  row3(LANES),
                  pl.BlockSpec((1, SUBLANES, DK), lambda b, pt, ix: (b, 0, 0))]
                 + [blk_spec(k) for k in range(k_top)],
        out_specs=pl.BlockSpec((BS, BRANCH), lambda b, pt, ix: (0, 0)),
    )
    return pl.pallas_call(
        functools.partial(_nsa_sel_decode_kernel, k_top=k_top, n_blocks=n_blocks),
        grid_spec=grid_spec,
        out_shape=jax.ShapeDtypeStruct((BS, BRANCH), F32),
        compiler_params=_cparams(("arbitrary",)),
        name="nsa_sel_decode",
    )(phys, idx, q8, pg_new, w_new, win_t, g_new, o_c, *([cache_t] * k_top))


def _column(row, eye):
    return jnp.sum(jnp.where(eye, jnp.broadcast_to(row, eye.shape), 0.0), axis=1, keepdims=True)


def _rec_decode_kernel(xbc_ref, cbuf_ref, dt_ref, z_ref, hg_ref, s_ref, hs_ref,
                       cw_ref, cb_ref, dtb_ref, alog_ref, dsk_ref, sn_ref, lb_ref, hn_ref,
                       os_ref, oh_ref, sn_out, hs_out):
    b = pl.program_id(0)
    eye = lax.broadcasted_iota(I32, (DK, DK), 0) == lax.broadcasted_iota(I32, (DK, DK), 1)

    cbuf = cbuf_ref[...]
    xn = xbc_ref[0]
    conv = cb_ref[...] + cw_ref[SSD_CONV - 1:SSD_CONV, :] * xn
    for j in range(SSD_CONV - 1):
        conv = conv + cw_ref[j:j + 1, :] * cbuf[j:j + 1, :]
    xa = _silu(conv)
    dt = _softplus(dt_ref[0] + dtb_ref[...])
    ea = jnp.exp(dt * (-jnp.exp(alog_ref[...])))
    dsk = dsk_ref[...]
    ys = []
    for h in range(HEADS):
        g = h // (HEADS // SSD_GROUPS)
        xs = xa[:, DK * h:DK * (h + 1)]
        bm = xa[:, BRANCH + DK * g:BRANCH + DK * (g + 1)]
        cm = xa[:, BRANCH + DK * (SSD_GROUPS + g):BRANCH + DK * (SSD_GROUPS + g + 1)]
        xdt_col = _column(xs * dt[:, h:h + 1], eye)
        s_old = s_ref[h]
        ea_h = ea[:, h:h + 1]
        y_col = (jnp.sum(cm * bm, axis=-1, keepdims=True) * xdt_col
                 + jnp.sum(s_old * cm, axis=-1, keepdims=True) * ea_h)
        sn_out[0, h] = s_old * ea_h + xdt_col * bm
        y_row = jnp.sum(jnp.where(eye, jnp.broadcast_to(y_col, (DK, DK)), 0.0), axis=0, keepdims=True)
        ys.append(y_row + dsk[:, h:h + 1] * xs)
    y = jnp.concatenate(ys, axis=1)
    os_ref[pl.ds(b, 1), :] = _rms(y * _silu(z_ref[0]), sn_ref[...])

    W = BRANCH
    hg = hg_ref[0]
    q = hg[:, 0:W]
    v = hg[:, 2 * W:3 * W]
    log_f, kin = _hgrn_gates(hg[:, W:2 * W], lb_ref[...])
    ef = jnp.exp(log_f)
    hn = hn_ref[...]
    outs = []
    for h in range(HEADS):
        r = slice(DK * h, DK * (h + 1))
        s_old = hs_ref[h]
        att = jnp.sum(q[:, r] * kin[:, r], axis=-1, keepdims=True)
        y = att * v[:, r] + jnp.sum(s_old * _column(q[:, r] * ef[:, r], eye), axis=0, keepdims=True)
        hs_out[0, h] = s_old * _column(ef[:, r], eye) + _column(kin[:, r], eye) * v[:, r]
        outs.append(_rms(y, hn[:, r]))
    o = jnp.concatenate(outs, axis=1)
    oh_ref[pl.ds(b, 1), :] = o * _silu(hg[:, 3 * W:4 * W])


def _rec_decode(layer, xbc_new, conv_state, dt_new, z_new, hg_new, state_ssd, state_hgrn,
                cw, cb, dtb, alog, dsk, sn, lb, hn):
    BS = xbc_new.shape[0]
    row3 = lambda w: pl.BlockSpec((1, 1, w), lambda b: (b, 0, 0))
    st_in = pl.BlockSpec((None, None, HEADS, DK, DK), lambda b: (layer, b, 0, 0, 0))
    st_out = pl.BlockSpec((1, HEADS, DK, DK), lambda b: (b, 0, 0, 0))
    o_spec = pl.BlockSpec((BS, BRANCH), lambda b: (0, 0))
    consts = (cw, cb, dtb, alog, dsk, sn, lb, hn)
    return pl.pallas_call(
        _rec_decode_kernel,
        grid=(BS,),
        in_specs=[row3(SSD_CONV_DIM),
                  pl.BlockSpec((None, None, SSD_CONV - 1, SSD_CONV_DIM), lambda b: (layer, b, 0, 0)),
                  row3(LANES), row3(BRANCH), row3(4 * BRANCH), st_in, st_in]
                 + [_const_spec(c.shape) for c in consts],
        out_specs=[o_spec, o_spec, st_out, st_out],
        out_shape=[jax.ShapeDtypeStruct((BS, BRANCH), F32), jax.ShapeDtypeStruct((BS, BRANCH), F32),
                   jax.ShapeDtypeStruct((BS, HEADS, DK, DK), F32),
                   jax.ShapeDtypeStruct((BS, HEADS, DK, DK), F32)],
        compiler_params=_cparams(("arbitrary",)),
        name="rec_decode",
    )(xbc_new, conv_state, dt_new, z_new, hg_new, state_ssd, state_hgrn, *consts)


def _rot_cols(w):
    half = MLA_ROPE // 2
    return jnp.concatenate([-w[..., half:], w[..., :half]], axis=-1)


def _pad_cols(w, width):
    return jnp.pad(w, [(0, 0)] * (w.ndim - 1) + [(0, width - w.shape[-1])])


def _pack_w_in(w):
    sizes = (256, 384, 12, MLA_Q_LORA, MLA_KV_LORA, MLA_ROPE, 256, SSD_CONV_DIM, HEADS, 256, 256, 256, 256)
    offs = np.cumsum((0,) + sizes)
    (nsa_q, nsa_kv, nsa_g, cq, ckv, kpe, z, xbc, dt, hq, hf, hi, hgate) = (
        w[:, offs[i]:offs[i + 1]] for i in range(len(sizes)))
    gate = w[:, offs[-1]:]
    packed = jnp.concatenate([
        nsa_q, nsa_kv[:, :256], nsa_kv[:, 256:], _pad_cols(nsa_g, LANES), cq, ckv,
        _pad_cols(kpe, LANES), _pad_cols(_rot_cols(kpe), LANES), z, xbc, _pad_cols(dt, LANES),
        hq, hf, hi, hgate], axis=1)
    assert packed.shape[1] == PACKED_WIDTH
    return packed.astype(BF16), gate.astype(BF16)


def _pack_mla(wuq, wukv):
    nope = wuq[:, :, :MLA_NOPE].reshape(MLA_Q_LORA, HEADS * MLA_NOPE)
    pe = wuq[:, :, MLA_NOPE:]
    pe_p = _pad_cols(pe, LANES).reshape(MLA_Q_LORA, HEADS * LANES)
    pr_p = _pad_cols(_rot_cols(pe), LANES).reshape(MLA_Q_LORA, HEADS * LANES)
    wuq_p = jnp.concatenate([nope, pe_p, pr_p], axis=1).astype(BF16)
    eye = jnp.eye(HEADS, dtype=wukv.dtype)
    wukn = jnp.einsum('rhn,hg->hngr', wukv[:, :, :MLA_NOPE], eye).reshape(HEADS * MLA_NOPE, HEADS * MLA_KV_LORA)
    wv = jnp.einsum('rhv,hg->hrgv', wukv[:, :, MLA_NOPE:], eye).reshape(HEADS * MLA_KV_LORA, HEADS * MLA_V)
    return wuq_p, wukn.astype(BF16), wv.astype(BF16)


def _rope_table(positions):
    half = MLA_ROPE // 2
    freq = ROPE_THETA ** (-jnp.arange(half, dtype=F32) / half)
    ang = positions.astype(F32)[:, None] * freq
    cos = jnp.concatenate([jnp.cos(ang), jnp.cos(ang)], axis=1)
    sin = jnp.concatenate([jnp.sin(ang), jnp.sin(ang)], axis=1)
    return jnp.concatenate([_pad_cols(cos, LANES), _pad_cols(sin, LANES)], axis=1)


def _pad_row(v, width=LANES):
    return _pad_cols(v.reshape(1, -1).astype(F32), width)


def kernel(x_prompt, x_sample, cache_nsa_kv, cache_mla, cache_nsa_win, state_ssd, state_ssd_conv, state_hgrn, page_table, norm_g, ffn_w_in, ffn_w_out, w_in, mla_q_norm, mla_kv_norm, mla_w_uq, mla_w_ukv, ssd_conv_w, ssd_conv_b, ssd_dt_bias, ssd_a_log, ssd_d, ssd_norm, hg_lb_logits, hg_norm, w_branch, w_out):
    BP, T, D = x_prompt.shape
    BS = x_sample.shape[0]
    depth = w_in.shape[0]
    n_pool, page = cache_nsa_kv.shape[1], cache_nsa_kv.shape[2]
    npages = page_table.shape[1]
    past = npages * page
    n_win = cache_nsa_win.shape[2]
    rows_p = BP * T
    assert D == D_MODEL and x_sample.shape[1] == 1 and BS % SUBLANES == 0
    assert n_win == min(NSA_WINDOW, past) and past % NSA_BLOCK == 0
    k_top = min(NSA_TOPK, past // NSA_BLOCK + 1)

    xp = x_prompt.reshape(rows_p, D)
    xs = x_sample.reshape(BS, D)
    cs_p = _rope_table(jnp.tile(jnp.arange(T, dtype=I32), BP))
    cs_s = _rope_table(jnp.full((BS,), past, I32))
    lbp = jax.nn.softmax(hg_lb_logits.astype(F32), axis=0)
    lower_bounds = jnp.cumsum(lbp, axis=0) - lbp[0]

    nsa_t = jnp.transpose(cache_nsa_kv, (0, 1, 3, 4, 2)).reshape(depth, n_pool, 4 * DK, page)
    mla_t = jnp.transpose(cache_mla, (0, 1, 3, 2))
    win_t = jnp.transpose(cache_nsa_win, (0, 1, 3, 4, 2)).reshape(depth, BS, 2 * DK, n_win)

    def row3(a):
        return a.reshape(BS, 1, a.shape[1])

    outs = [[] for _ in range(12)]
    for l in range(depth):
        g = norm_g[l].astype(F32)
        grow = lambda i: g[i].reshape(1, D)
        wi = ffn_w_in[l].astype(BF16)
        wo = ffn_w_out[l].astype(BF16)
        w_pack, w_gate = _pack_w_in(w_in[l])
        wuq_p, wukn, wv = _pack_mla(mla_w_uq[l], mla_w_ukv[l])
        mqn = mla_q_norm[l].reshape(1, -1)
        mkvn = mla_kv_norm[l].reshape(1, -1)

        xp = _ffn(xp, grow(0), grow(1), wi[0], wo[0])
        xs = _ffn(xs, grow(0), grow(1), wi[0], wo[0])
        (nsa_q, nsa_pg, nsa_w, nsa_g, qlat, qpe, mla_new, ssd_z, ssd_xbc, ssd_dt, hg) = _inproj(
            xp, grow(2), cs_p, w_pack, mqn, mkvn, wuq_p, wukn)
        (nsa_q_s, nsa_pg_s, nsa_w_s, nsa_g_s, qlat_s, qpe_s, mla_new_s, ssd_z_s, ssd_xbc_s, ssd_dt_s,
         hg_s) = _inproj(xs, grow(2), cs_s, w_pack, mqn, mkvn, wuq_p, wukn)

        o_nsa = _nsa_prompt(nsa_q, nsa_pg, nsa_w, nsa_g, BP, T)
        pad8 = lambda a: jnp.pad(a, ((0, 0), (0, SUBLANES - HEADS), (0, 0)))
        q8 = pad8(nsa_q_s.reshape(BS, HEADS, DK))
        o_c, imp = _nsa_cmp_decode(l, page_table, q8, nsa_t, past)
        idx = _topk(imp.reshape(BS, 2 * LANES), past, k_top)
        o_nsa_s = _nsa_sel_decode(l, page_table, idx, q8, row3(nsa_pg_s), row3(nsa_w_s), win_t,
                                  row3(nsa_g_s), o_c, nsa_t, k_top)

        o_mla = _mla_prompt(qlat, qpe, mla_new, wv, BP, T)
        q_cat = jnp.concatenate([qlat_s.reshape(BS, HEADS, LANES),
                                 qpe_s.reshape(BS, HEADS, LANES)[:, :, :MLA_ROPE]], axis=-1)
        o_mla_s = _mla_decode(l, page_table, pad8(q_cat), jnp.transpose(q_cat, (0, 2, 1)),
                              row3(mla_new_s), mla_t, wv)

        cw = ssd_conv_w[l].astype(F32)
        cb = ssd_conv_b[l].reshape(1, -1).astype(F32)
        dtb = _pad_row(ssd_dt_bias[l])
        alog = _pad_row(ssd_a_log[l])
        dsk = _pad_row(ssd_d[l])
        sn = ssd_norm[l].reshape(1, -1).astype(F32)
        lb = lower_bounds[l].reshape(1, -1)
        hn = hg_norm[l].reshape(1, -1).astype(F32)
        o_ssd, ssd_p = _ssd_prompt(ssd_xbc, ssd_dt, ssd_z, cw, cb, dtb, alog, dsk, sn, BP, T)
        o_hg, hgrn_p = _hgrn_prompt(hg, lb, hn, BP, T)
        o_ssd_s, o_hg_s, ssd_s, hgrn_s = _rec_decode(
            l, row3(ssd_xbc_s), state_ssd_conv, row3(ssd_dt_s), row3(ssd_z_s), row3(hg_s), state_ssd,
            state_hgrn, cw, cb, dtb, alog, dsk, sn, lb, hn)

        wbr = w_branch[l].astype(BF16)
        wout = w_out[l].astype(BF16)
        xp = _merge(xp, (o_nsa, o_mla, o_ssd, o_hg), grow(2), grow(3), w_gate, wbr, wout)
        xs = _merge(xs, (o_nsa_s, o_mla_s, o_ssd_s, o_hg_s), grow(2), grow(3), w_gate, wbr, wout)
        xp = _ffn(xp, grow(4), grow(5), wi[1], wo[1])
        xs = _ffn(xs, grow(4), grow(5), wi[1], wo[1])

        n_keep = min(NSA_WINDOW, T)
        layer_out = (
            nsa_pg.reshape(BP, T, 4, DK), nsa_pg_s.reshape(BS, 1, 4, DK),
            mla_new.reshape(BP, T, MLA_CACHE), mla_new_s.reshape(BS, 1, MLA_CACHE),
            nsa_w.reshape(BP, T, 2, DK)[:, T - n_keep:],
            jnp.concatenate([cache_nsa_win[l], nsa_w_s.reshape(BS, 1, 2, DK)], axis=1)[:, 1:],
            ssd_p, ssd_s,
            ssd_xbc.reshape(BP, T, SSD_CONV_DIM)[:, T - (SSD_CONV - 1):],
            jnp.concatenate([state_ssd_conv[l], ssd_xbc_s.reshape(BS, 1, SSD_CONV_DIM)], axis=1)[:, 1:],
            hgrn_p, hgrn_s)
        for acc, o in zip(outs, layer_out):
            acc.append(o)

    stacked = tuple(jnp.stack(o) for o in outs)
    return (xp.reshape(BP, T, D), xs.reshape(BS, 1, D)) + stacked
```

```python
import functools
import math

import numpy as np
import jax
import jax.numpy as jnp
from jax import lax
from jax.experimental import pallas as pl
from jax.experimental.pallas import tpu as pltpu

F32 = jnp.float32
BF16 = jnp.bfloat16
I32 = jnp.int32
HIGHEST = lax.Precision.HIGHEST

EPS = 1e-6
NEG = -1e30
LB_FLOOR = 1e-20
ROPE_THETA = 10000.0

D_MODEL = 1024
D_FF = 2816
HEADS = 4
DK = 64
BRANCH = HEADS * DK
NSA_BLOCK = 64
NSA_TOPK = 16
NSA_WINDOW = 512
NSA_FORCED_BONUS = float(HEADS + 1)
MLA_Q_LORA = 256
MLA_KV_LORA = 128
MLA_NOPE = 64
MLA_ROPE = 32
MLA_V = 64
MLA_CACHE = MLA_KV_LORA + MLA_ROPE
SSD_GROUPS = 2
SSD_CONV = 4
SSD_CONV_DIM = BRANCH + 2 * SSD_GROUPS * DK
CHUNK = 128
QB = 128

LANES = 128
SUBLANES = 8
VMEM_LIMIT_BYTES = 56 * 1024 * 1024

_SEG = {}
_off = 0
for _name, _w in (("nsa_q", 256), ("nsa_pg", 256), ("nsa_w", 128), ("nsa_g", 128),
                  ("mla_cq", 256), ("mla_ckv", 128), ("mla_kpe", 128), ("mla_kpr", 128),
                  ("ssd_z", 256), ("ssd_xbc", 512), ("ssd_dt", 128), ("hg", 1024)):
    _SEG[_name] = (_off, _off + _w)
    _off += _w
PACKED_WIDTH = _off


def _cparams(sem):
    return pltpu.CompilerParams(dimension_semantics=sem, vmem_limit_bytes=VMEM_LIMIT_BYTES)


def _rms(x, g):
    return x * lax.rsqrt(jnp.mean(x * x, axis=-1, keepdims=True) + EPS) * g


def _dot(a, b):
    return jnp.dot(a.astype(BF16), b.astype(BF16), preferred_element_type=F32)


def _dot_nt(a, b):
    return lax.dot_general(a.astype(BF16), b.astype(BF16), (((1,), (1,)), ((), ())),
                           preferred_element_type=F32)


def _dot_f32(a, b):
    return jnp.dot(a, b, precision=HIGHEST, preferred_element_type=F32)


def _dot_nt_f32(a, b):
    return lax.dot_general(a, b, (((1,), (1,)), ((), ())), precision=HIGHEST,
                           preferred_element_type=F32)


def _sigmoid(x):
    return 1.0 / (1.0 + jnp.exp(-x))


def _silu(x):
    return x * _sigmoid(x)


def _softplus(x):
    return jnp.maximum(x, 0.0) + jnp.log1p(jnp.exp(-jnp.abs(x)))


def _masked_softmax(s, mask):
    s = jnp.where(mask, s, NEG)
    m = jnp.max(s, axis=-1, keepdims=True)
    e = jnp.exp(s - m)
    r = 1.0 / jnp.sum(e, axis=-1, keepdims=True)
    return jnp.where(mask, e * r, 0.0)


def _unnormalised_softmax(s, mask):
    s = jnp.where(mask, s, NEG)
    e = jnp.exp(s - jnp.max(s, axis=-1, keepdims=True))
    return e, 1.0 / jnp.sum(e, axis=-1, keepdims=True)


def _row_tile(rows, cap=864):
    best = SUBLANES
    for t in range(SUBLANES, min(rows, cap) + 1, SUBLANES):
        if rows % t == 0:
            best = t
    assert rows % best == 0
    return best


def _const_spec(shape):
    nd = len(shape)
    return pl.BlockSpec(shape, lambda *_: (0,) * nd)


FFN_CHUNKS = 2


def _ffn_kernel(x_ref, ga_ref, gb_ref, wi_ref, wo_ref, o_ref):
    x = x_ref[...]
    h = _rms(x, ga_ref[...]).astype(BF16)
    fc = D_FF // FFN_CHUNKS
    acc = None
    for c in range(FFN_CHUNKS):
        gate = jnp.dot(h, wi_ref[:, fc * c:fc * (c + 1)], preferred_element_type=F32)
        up = jnp.dot(h, wi_ref[:, D_FF + fc * c:D_FF + fc * (c + 1)], preferred_element_type=F32)
        part = _dot(_silu(gate) * up, wo_ref[fc * c:fc * (c + 1), :])
        acc = part if acc is None else acc + part
    o_ref[...] = x + 0.5 * _rms(acc, gb_ref[...])


def _resident_spec(shape):
    nd = len(shape)
    return pl.BlockSpec(shape, lambda *_: (0,) * nd, pipeline_mode=pl.Buffered(1))


def _ffn(x, ga, gb, wi, wo):
    rows = x.shape[0]
    tm = _row_tile(rows, cap=512)
    assert (D_FF // FFN_CHUNKS) % LANES == 0
    return pl.pallas_call(
        _ffn_kernel,
        grid=(rows // tm,),
        in_specs=[
            pl.BlockSpec((tm, D_MODEL), lambda i: (i, 0)),
            _const_spec((1, D_MODEL)),
            _const_spec((1, D_MODEL)),
            _resident_spec(wi.shape),
            _resident_spec(wo.shape),
        ],
        out_specs=pl.BlockSpec((tm, D_MODEL), lambda i: (i, 0)),
        out_shape=jax.ShapeDtypeStruct((rows, D_MODEL), F32),
        compiler_params=_cparams(("parallel",)),
        name="ffn",
    )(x, ga, gb, wi, wo)


def _inproj_kernel(x_ref, g_ref, cs_ref, w_ref, mqn_ref, mkvn_ref, wuq_ref, wukn_ref,
                   q_o, pg_o, w_o, g_o, qlat_o, qpe_o, mla_o, z_o, xbc_o, dt_o, hg_o):
    hn = _rms(x_ref[...], g_ref[...]).astype(BF16)

    def seg(name):
        a, b = _SEG[name]
        return jnp.dot(hn, w_ref[:, a:b], preferred_element_type=F32)

    q_o[...] = seg("nsa_q") * (DK ** -0.5)
    pg_o[...] = seg("nsa_pg")
    w_o[...] = seg("nsa_w")
    g_o[...] = _sigmoid(seg("nsa_g"))
    z_o[...] = seg("ssd_z")
    xbc_o[...] = seg("ssd_xbc")
    dt_o[...] = seg("ssd_dt")
    hg_o[...] = seg("hg")

    cos_p = cs_ref[:, 0:LANES]
    sin_p = cs_ref[:, LANES:2 * LANES]
    cqn = _rms(seg("mla_cq"), mqn_ref[...])
    qf = _dot(cqn, wuq_ref[...])
    nope_w = HEADS * MLA_NOPE
    qlat_o[...] = _dot(qf[:, 0:nope_w], wukn_ref[...])
    for h in range(HEADS):
        pe = qf[:, nope_w + LANES * h: nope_w + LANES * (h + 1)]
        pr = qf[:, nope_w + LANES * (HEADS + h): nope_w + LANES * (HEADS + h + 1)]
        qpe_o[:, LANES * h:LANES * (h + 1)] = pe * cos_p + pr * sin_p
    mla_o[:, 0:MLA_KV_LORA] = _rms(seg("mla_ckv"), mkvn_ref[...])
    kpe = seg("mla_kpe") * cos_p + seg("mla_kpr") * sin_p
    mla_o[:, MLA_KV_LORA:MLA_CACHE] = kpe[:, 0:MLA_ROPE]


_INPROJ_OUT_WIDTHS = (256, 256, 128, 128, 512, 512, MLA_CACHE, 256, 512, 128, 1024)


def _inproj(x, g, cs, w, mqn, mkvn, wuq, wukn):
    rows = x.shape[0]
    tm = _row_tile(rows)
    row_spec = lambda c: pl.BlockSpec((tm, c), lambda i: (i, 0))
    return pl.pallas_call(
        _inproj_kernel,
        grid=(rows // tm,),
        in_specs=[row_spec(D_MODEL), _const_spec((1, D_MODEL)), row_spec(2 * LANES),
                  _const_spec(w.shape), _const_spec(mqn.shape), _const_spec(mkvn.shape),
                  _const_spec(wuq.shape), _const_spec(wukn.shape)],
        out_specs=[row_spec(c) for c in _INPROJ_OUT_WIDTHS],
        out_shape=[jax.ShapeDtypeStruct((rows, c), F32) for c in _INPROJ_OUT_WIDTHS],
        compiler_params=_cparams(("parallel",)),
        name="inproj",
    )(x, g, cs, w, mqn, mkvn, wuq, wukn)


def _merge_kernel(x_ref, b0_ref, b1_ref, b2_ref, b3_ref, g2_ref, g3_ref, wg_ref, wbr_ref, wout_ref,
                  o_ref):
    x = x_ref[...]
    hn = _rms(x, g2_ref[...]).astype(BF16)
    acc = None
    for k, b_ref in enumerate((b0_ref, b1_ref, b2_ref, b3_ref)):
        gate = _sigmoid(jnp.dot(hn, wg_ref[:, D_MODEL * k:D_MODEL * (k + 1)],
                                preferred_element_type=F32))
        u = _dot(b_ref[...], wbr_ref[k])
        acc = gate * u if acc is None else acc + gate * u
    out = _dot(acc, wout_ref[...])
    o_ref[...] = x + _rms(out, g3_ref[...])


def _merge(x, branches, g2, g3, wgate, wbr, wout):
    rows = x.shape[0]
    tm = _row_tile(rows, cap=512)
    row_spec = lambda c: pl.BlockSpec((tm, c), lambda i: (i, 0))
    return pl.pallas_call(
        _merge_kernel,
        grid=(rows // tm,),
        in_specs=[row_spec(D_MODEL)] + [row_spec(BRANCH)] * 4 +
                 [_const_spec((1, D_MODEL)), _const_spec((1, D_MODEL)),
                  _const_spec(wgate.shape), _const_spec(wbr.shape), _const_spec(wout.shape)],
        out_specs=row_spec(D_MODEL),
        out_shape=jax.ShapeDtypeStruct((rows, D_MODEL), F32),
        compiler_params=_cparams(("parallel",)),
        name="merge",
    )(x, *branches, g2, g3, wgate, wbr, wout)


KEY_STEP = 512


def _key_extents(T):
    his = sorted({min(T, KEY_STEP * (c + 1)) for c in range(-(-T // KEY_STEP))})
    return list(zip([0] + his[:-1], his))


def _stack_heads(x, width, take):
    return jnp.concatenate([x[:, width * h: width * h + take] for h in range(HEADS)], axis=0)


def _nsa_prompt_kernel(q_ref, pg_ref, wk_ref, g_ref, e_ref, o_ref, *, T, band):
    nb = T // NSA_BLOCK
    k_top = min(NSA_TOPK, nb)
    q0 = pl.program_id(1) * QB
    q = q_ref[...]
    zeros = jnp.zeros((QB, DK), F32)
    q_rows = jnp.concatenate(
        [jnp.concatenate([q[:, DK * h:DK * (h + 1)], zeros], axis=1) for h in range(HEADS)],
        axis=0).astype(BF16)
    pos = q0 + lax.broadcasted_iota(I32, (QB, 1), 0)
    lane = lax.broadcasted_iota(I32, (QB, LANES), 1)

    def per_head(fn):
        return jnp.concatenate([fn(h) for h in range(HEADS)], axis=0)

    def attend_values(s, mask, kv):
        parts = [_unnormalised_softmax(s[QB * h:QB * (h + 1)], mask) for h in range(HEADS)]
        e = jnp.concatenate([p[0] for p in parts], axis=0)
        r = jnp.concatenate([p[1] for p in parts], axis=0)
        return _dot(e, kv) * r

    cm = jnp.sum(pg_ref[:, 0:LANES].reshape(nb, NSA_BLOCK, LANES), axis=1) * (1.0 / NSA_BLOCK)
    if nb < LANES:
        cm = jnp.concatenate([cm, jnp.zeros((LANES - nb, LANES), F32)], axis=0)
    cm = cm.astype(BF16)
    s_c = _dot_nt(q_rows, cm)
    cmask = (lane + 1) * NSA_BLOCK - 1 <= pos
    p_c = per_head(lambda h: _masked_softmax(s_c[QB * h:QB * (h + 1)], cmask))
    o_c = _dot(p_c, cm)

    cur = pos // NSA_BLOCK
    imp = p_c[0:QB] + p_c[QB:2 * QB] + p_c[2 * QB:3 * QB] + p_c[3 * QB:4 * QB]
    forced = (lane == 0) | (lane == cur) | (lane == cur - 1)
    imp = imp + jnp.where(forced, NSA_FORCED_BONUS, 0.0)
    imp = jnp.where(lane > cur, -1.0, imp)
    rank = jnp.zeros((QB, LANES), F32)
    for n in range(nb):
        c = imp[:, n:n + 1]
        beats = (c > imp) | ((c == imp) & (lane > n))
        rank = rank + jnp.where(beats, 1.0, 0.0)
    sel = jnp.where(rank < float(k_top), 1.0, 0.0)

    def selected(tk):
        in_sel = jnp.dot(sel.astype(BF16), e_ref[:, 0:tk], preferred_element_type=F32)
        kpos = lax.broadcasted_iota(I32, (QB, tk), 1)
        smask = (in_sel > 0.5) & (kpos <= pos)
        kv_s = pg_ref[0:tk, LANES:2 * LANES].astype(BF16)
        s_s = _dot_nt(q_rows, kv_s)
        return attend_values(s_s, smask, kv_s)

    start = pl.multiple_of(jnp.clip(q0 - NSA_WINDOW, 0, T - band), LANES)
    kv_w = wk_ref[pl.ds(start, band), :].astype(BF16)
    kposw = start + lax.broadcasted_iota(I32, (QB, band), 1)
    wmask = (kposw <= pos) & (kposw >= pos - NSA_WINDOW)
    o_w = attend_values(_dot_nt(q_rows, kv_w), wmask, kv_w)

    g = g_ref[...]

    def finish(o_s):
        outs = []
        for h in range(HEADS):
            r = slice(QB * h, QB * (h + 1))
            comb = (g[:, 3 * h:3 * h + 1] * o_c[r] + g[:, 3 * h + 1:3 * h + 2] * o_s[r]
                    + g[:, 3 * h + 2:3 * h + 3] * o_w[r])
            outs.append(comb[:, DK:2 * DK])
        o_ref[...] = jnp.concatenate(outs, axis=1)

    for lo, tk in _key_extents(T):
        @pl.when((q0 + QB > lo) & (q0 + QB <= tk))
        def _(tk=tk):
            finish(selected(tk))


def _nsa_prompt(q, pg, wkv, g, B, T):
    assert T % QB == 0 and T // NSA_BLOCK <= LANES
    rows = B * T
    nq = T // QB
    band = min(NSA_WINDOW + QB, T)
    nb = T // NSA_BLOCK
    expand = (np.arange(LANES)[:, None] == (np.arange(T)[None, :] // NSA_BLOCK)) & (np.arange(LANES)[:, None] < nb)
    expand = jnp.asarray(expand, BF16)
    return pl.pallas_call(
        functools.partial(_nsa_prompt_kernel, T=T, band=band),
        grid=(B, nq),
        in_specs=[pl.BlockSpec((QB, BRANCH), lambda b, i: (b * nq + i, 0)),
                  pl.BlockSpec((T, 256), lambda b, i: (b, 0)),
                  pl.BlockSpec((T, LANES), lambda b, i: (b, 0)),
                  pl.BlockSpec((QB, LANES), lambda b, i: (b * nq + i, 0)),
                  _const_spec((LANES, T))],
        out_specs=pl.BlockSpec((QB, BRANCH), lambda b, i: (b * nq + i, 0)),
        out_shape=jax.ShapeDtypeStruct((rows, BRANCH), F32),
        compiler_params=_cparams(("parallel", "parallel")),
        name="nsa_prompt",
    )(q, pg, wkv, g, expand)


def _mla_prompt_kernel(qlat_ref, qpe_ref, kv_ref, wv_ref, o_ref, *, T):
    q0 = pl.program_id(1) * QB
    ql = _stack_heads(qlat_ref[...], LANES, LANES)
    qp = _stack_heads(qpe_ref[...], LANES, MLA_ROPE)
    scale = (MLA_NOPE + MLA_ROPE) ** -0.5
    pos = q0 + lax.broadcasted_iota(I32, (QB, 1), 0)

    def attend(tk):
        ckv = kv_ref[0:tk, 0:MLA_KV_LORA].astype(BF16)
        kpe = kv_ref[0:tk, MLA_KV_LORA:MLA_CACHE].astype(BF16)
        s = (_dot_nt(ql, ckv) + _dot_nt(qp, kpe)) * scale
        mask = lax.broadcasted_iota(I32, (QB, tk), 1) <= pos
        parts = [_unnormalised_softmax(s[QB * h:QB * (h + 1)], mask) for h in range(HEADS)]
        e = jnp.concatenate([p[0] for p in parts], axis=0)
        r = jnp.concatenate([p[1] for p in parts], axis=0)
        o_lat = _dot(e, ckv) * r
        o_cat = jnp.concatenate([o_lat[QB * h:QB * (h + 1)] for h in range(HEADS)], axis=1)
        o_ref[...] = _dot(o_cat, wv_ref[...])

    for lo, tk in _key_extents(T):
        @pl.when((q0 + QB > lo) & (q0 + QB <= tk))
        def _(tk=tk):
            attend(tk)


def _mla_prompt(qlat, qpe, mla_new, wv, B, T):
    nq = T // QB
    rows = B * T
    return pl.pallas_call(
        functools.partial(_mla_prompt_kernel, T=T),
        grid=(B, nq),
        in_specs=[pl.BlockSpec((QB, 512), lambda b, i: (b * nq + i, 0)),
                  pl.BlockSpec((QB, 512), lambda b, i: (b * nq + i, 0)),
                  pl.BlockSpec((T, MLA_CACHE), lambda b, i: (b, 0)),
                  _const_spec(wv.shape)],
        out_specs=pl.BlockSpec((QB, BRANCH), lambda b, i: (b * nq + i, 0)),
        out_shape=jax.ShapeDtypeStruct((rows, BRANCH), F32),
        compiler_params=_cparams(("parallel", "parallel")),
        name="mla_prompt",
    )(qlat, qpe, mla_new, wv)


def _head_lanes(x_cols, shape):
    lane = lax.broadcasted_iota(I32, shape, 1)
    out = jnp.broadcast_to(x_cols[:, HEADS - 1:HEADS], shape)
    for h in range(HEADS - 2, -1, -1):
        out = jnp.where(lane < DK * (h + 1), jnp.broadcast_to(x_cols[:, h:h + 1], shape), out)
    return out


def _ssd_prompt_kernel(xbc_ref, dt_ref, z_ref, cw_ref, cb_ref, dtb_ref, alog_ref, dsk_ref, sn_ref,
                       o_ref, st_ref, buf_ref, s_ref):
    c = pl.program_id(1)
    C = CHUNK

    @pl.when(c == 0)
    def _():
        buf_ref[0:SUBLANES, :] = jnp.zeros((SUBLANES, SSD_CONV_DIM), F32)
        s_ref[...] = jnp.zeros_like(s_ref)

    x = xbc_ref[...]
    buf_ref[SUBLANES:SUBLANES + C, :] = x
    conv = cb_ref[...] + cw_ref[SSD_CONV - 1:SSD_CONV, :] * x
    for j in range(SSD_CONV - 1):
        conv = conv + cw_ref[j:j + 1, :] * buf_ref[pl.ds(SUBLANES - (SSD_CONV - 1) + j, C), :]
    buf_ref[0:SUBLANES, :] = x[C - SUBLANES:C, :]
    xa = _silu(conv)
    xs = xa[:, 0:BRANCH]
    dt = _softplus(dt_ref[...] + dtb_ref[...])
    a = dt * (-jnp.exp(alog_ref[...]))
    row = lax.broadcasted_iota(I32, (C, C), 0)
    col = lax.broadcasted_iota(I32, (C, C), 1)
    tril = row >= col
    tri = jnp.where(tril, 1.0, 0.0)
    cum_c = _dot_f32(tri, a)
    cum_r = _dot_nt_f32(a.T, tri)
    xdt = xs * _head_lanes(dt, (C, BRANCH))
    xdt_t = xdt.T
    dsk = dsk_ref[...]
    ys = []
    for h in range(HEADS):
        g = h // (HEADS // SSD_GROUPS)
        bm = xa[:, BRANCH + DK * g:BRANCH + DK * (g + 1)]
        cm = xa[:, BRANCH + DK * (SSD_GROUPS + g):BRANCH + DK * (SSD_GROUPS + g + 1)]
        ch = cum_c[:, h:h + 1]
        decay = jnp.exp(jnp.where(tril, ch - cum_r[h:h + 1, :], NEG))
        scores = _dot_nt(cm, bm) * decay
        xdt_h = xdt[:, DK * h:DK * (h + 1)]
        s_old = s_ref[h]
        y = _dot(scores, xdt_h) + _dot_nt(cm, s_old) * jnp.exp(ch)
        last = cum_c[C - 1:C, h:h + 1]
        w = jnp.exp(last - ch)
        s_ref[h] = s_old * jnp.exp(last) + _dot(xdt_t[DK * h:DK * (h + 1), :], bm * w)
        ys.append(y + dsk[:, h:h + 1] * xs[:, DK * h:DK * (h + 1)])
    y = jnp.concatenate(ys, axis=1)
    o_ref[...] = _rms(y * _silu(z_ref[...]), sn_ref[...])

    @pl.when(c == pl.num_programs(1) - 1)
    def _():
        st_ref[0] = s_ref[...]


def _ssd_prompt(xbc, dt, z, cw, cb, dtb, alog, dsk, sn, B, T):
    assert T % CHUNK == 0
    nc = T // CHUNK
    rows = B * T
    row_spec = lambda w: pl.BlockSpec((CHUNK, w), lambda b, c: (b * nc + c, 0))
    return pl.pallas_call(
        _ssd_prompt_kernel,
        grid=(B, nc),
        in_specs=[row_spec(SSD_CONV_DIM), row_spec(LANES), row_spec(BRANCH),
                  _const_spec(cw.shape), _const_spec(cb.shape), _const_spec(dtb.shape),
                  _const_spec(alog.shape), _const_spec(dsk.shape), _const_spec(sn.shape)],
        out_specs=[row_spec(BRANCH), pl.BlockSpec((1, HEADS, DK, DK), lambda b, c: (b, 0, 0, 0))],
        out_shape=[jax.ShapeDtypeStruct((rows, BRANCH), F32),
                   jax.ShapeDtypeStruct((B, HEADS, DK, DK), F32)],
        scratch_shapes=[pltpu.VMEM((SUBLANES + CHUNK, SSD_CONV_DIM), F32),
                        pltpu.VMEM((HEADS, DK, DK), F32)],
        compiler_params=_cparams(("parallel", "arbitrary")),
        name="ssd_prompt",
    )(xbc, dt, z, cw, cb, dtb, alog, dsk, sn)


def _hgrn_gates(fr, lb):
    log_sig = jnp.minimum(fr, 0.0) - jnp.log1p(jnp.exp(-jnp.abs(fr)))
    a = jnp.log(jnp.maximum(lb, LB_FLOOR))
    b = jnp.log1p(-lb) + log_sig
    log_f = jnp.maximum(a, b) + jnp.log1p(jnp.exp(-jnp.abs(a - b)))
    return log_f, (1.0 - lb) * _sigmoid(-fr)


def _block_reference_rows(G, m):
    C = G.shape[0]
    if m >= 4:
        parts = []
        for p in range(C // (2 * m)):
            r = p * 2 * m + m - 1
            parts.append(jnp.broadcast_to(G[r:r + 1, :], (2 * m, G.shape[1])))
        return parts[0] if len(parts) == 1 else jnp.concatenate(parts, axis=0)
    t = lax.broadcasted_iota(I32, G.shape, 0)
    if m == 1:
        return jnp.where((t & 1) == 1, pltpu.roll(G, 1, 0), G)
    r = t & 3
    return jnp.where(r == 0, pltpu.roll(G, C - 1, 0),
                     jnp.where(r == 1, G, jnp.where(r == 2, pltpu.roll(G, 1, 0), pltpu.roll(G, 2, 0))))


def _hgrn_prompt_kernel(hg_ref, lb_ref, hn_ref, seg_ref, o_ref, st_ref, s_ref):
    c = pl.program_id(1)
    C = CHUNK
    W = BRANCH

    @pl.when(c == 0)
    def _():
        s_ref[...] = jnp.zeros_like(s_ref)

    q = hg_ref[:, 0:W]
    v = hg_ref[:, 2 * W:3 * W]
    log_f, kin = _hgrn_gates(hg_ref[:, W:2 * W], lb_ref[...])
    row = lax.broadcasted_iota(I32, (C, C), 0)
    col = lax.broadcasted_iota(I32, (C, C), 1)
    G = _dot_f32(jnp.where(row >= col, 1.0, 0.0), log_f)
    lane_head = lax.broadcasted_iota(I32, (C, W), 1) // DK
    t_idx = lax.broadcasted_iota(I32, (C, W), 0)

    def stack_heads(a):
        return jnp.concatenate([jnp.where(lane_head == h, a, 0.0) for h in range(HEADS)],
                               axis=0).astype(BF16)

    t4 = lax.broadcasted_iota(I32, (HEADS * C, C), 0) & (C - 1)
    s4 = lax.broadcasted_iota(I32, (HEADS * C, C), 1)
    att = jnp.where(t4 == s4, _dot_nt(stack_heads(q), kin), 0.0)
    m = C // 2
    while m >= 1:
        R = _block_reference_rows(G, m)
        upper = (t_idx & m) != 0
        A = jnp.where(upper, q * jnp.exp(jnp.minimum(G - R, 0.0)), 0.0)
        Bm = jnp.where(upper, 0.0, kin * jnp.exp(jnp.minimum(R - G, 0.0)))
        lm = int(math.log2(m))
        pair = ((t4 >> lm) ^ (s4 >> lm)) == 1
        att = att + jnp.where(pair & (t4 > s4), _dot_nt(stack_heads(A), Bm), 0.0)
        m //= 2
    y4 = _dot(att, v)
    y = jnp.zeros((C, W), F32)
    for h in range(HEADS):
        y = y + jnp.where(lane_head == h, y4[C * h:C * (h + 1)], 0.0)
    s_old = s_ref[...]
    y = y + _dot(q * jnp.exp(G), s_old)
    last = G[C - 1:C, :]
    ke_t = (kin * jnp.exp(last - G)).T
    G_t = G.T
    blk = (lax.broadcasted_iota(I32, (W, W), 0) // DK) == (lax.broadcasted_iota(I32, (W, W), 1) // DK)
    s_ref[...] = s_old * jnp.exp(G_t[:, C - 1:C]) + jnp.where(blk, _dot(ke_t, v), 0.0)
    ms = _dot_f32(y * y, seg_ref[...])
    o = y * lax.rsqrt(ms + EPS) * hn_ref[...]
    o_ref[...] = o * _silu(hg_ref[:, 3 * W:4 * W])

    @pl.when(c == pl.num_programs(1) - 1)
    def _():
        for h in range(HEADS):
            st_ref[0, h] = s_ref[DK * h:DK * (h + 1), DK * h:DK * (h + 1)]


def _hgrn_prompt(hg, lb, hn, B, T):
    nc = T // CHUNK
    rows = B * T
    seg = (np.arange(BRANCH)[:, None] // DK == np.arange(BRANCH)[None, :] // DK) / float(DK)
    seg = jnp.asarray(seg, F32)
    row_spec = lambda w: pl.BlockSpec((CHUNK, w), lambda b, c: (b * nc + c, 0))
    return pl.pallas_call(
        _hgrn_prompt_kernel,
        grid=(B, nc),
        in_specs=[row_spec(4 * BRANCH), _const_spec(lb.shape), _const_spec(hn.shape),
                  _const_spec(seg.shape)],
        out_specs=[row_spec(BRANCH), pl.BlockSpec((1, HEADS, DK, DK), lambda b, c: (b, 0, 0, 0))],
        out_shape=[jax.ShapeDtypeStruct((rows, BRANCH), F32),
                   jax.ShapeDtypeStruct((B, HEADS, DK, DK), F32)],
        scratch_shapes=[pltpu.VMEM((BRANCH, BRANCH), F32)],
        compiler_params=_cparams(("parallel", "arbitrary")),
        name="hgrn_prompt",
    )(hg, lb, hn, seg)


PAGES_PER_STEP = 16


def _mla_decode_kernel(pt_ref, q_ref, qt_ref, new_ref, *rest, n_pg):
    page_refs = rest[:n_pg]
    wv_ref, o_ref, qb_ref, m_ref, l_ref, acc_ref = rest[n_pg:]
    del pt_ref
    b = pl.program_id(0)
    j = pl.program_id(1)
    scale = (MLA_NOPE + MLA_ROPE) ** -0.5
    page = page_refs[0].shape[1]

    @pl.when(j == 0)
    def _():
        qt = qt_ref[0]
        for h in range(HEADS):
            qb_ref[h] = jnp.broadcast_to(qt[:, h:h + 1], (MLA_CACHE, LANES))
        m_ref[...] = jnp.full_like(m_ref, NEG)
        l_ref[...] = jnp.zeros_like(l_ref)
        acc_ref[...] = jnp.zeros_like(acc_ref)

    scores = [[] for _ in range(HEADS)]
    for r in page_refs:
        kt = r[...]
        for h in range(HEADS):
            scores[h].append(jnp.sum(kt * qb_ref[h], axis=0, keepdims=True))
    for h in range(HEADS):
        s = jnp.concatenate(scores[h], axis=1) * scale
        m_old = m_ref[h:h + 1, :]
        m_new = jnp.maximum(m_old, jnp.max(s, axis=-1, keepdims=True))
        alpha = jnp.exp(m_old - m_new)
        p = jnp.exp(s - m_new[:, 0:1])
        l_ref[h:h + 1, :] = alpha * l_ref[h:h + 1, :] + jnp.sum(p, axis=-1, keepdims=True)
        m_ref[h:h + 1, :] = m_new
        acc = alpha[:, 0:1] * acc_ref[h]
        for t, r in enumerate(page_refs):
            acc = acc + r[0:MLA_KV_LORA, :] * p[:, t * page:(t + 1) * page]
        acc_ref[h] = acc

    @pl.when(j == pl.num_programs(1) - 1)
    def _():
        new = new_ref[0]
        s_new = jnp.sum(q_ref[0] * new, axis=-1, keepdims=True) * scale
        ones = jnp.ones((SUBLANES, LANES), F32)
        outs = []
        for h in range(HEADS):
            m_old = m_ref[h:h + 1, :]
            m_fin = jnp.maximum(m_old, s_new[h:h + 1, :])
            alpha = jnp.exp(m_old - m_fin)
            p_n = jnp.exp(s_new[h:h + 1, :] - m_fin)
            l = alpha * l_ref[h:h + 1, :] + p_n
            lat = _dot_nt_f32(ones, acc_ref[h])[0:1, :]
            outs.append((alpha * lat + p_n * new[:, 0:MLA_KV_LORA]) / l)
        o_cat = jnp.concatenate(outs, axis=1)
        o = _dot(jnp.broadcast_to(o_cat, (SUBLANES, HEADS * MLA_KV_LORA)), wv_ref[...])
        o_ref[pl.ds(b, 1), :] = o[0:1, :]


def _mla_decode(layer, page_table, q8, qt, new_rows, cache_t, wv):
    BS, npages = page_table.shape
    page = cache_t.shape[3]
    n_pg = min(PAGES_PER_STEP, npages)
    assert npages % n_pg == 0 and page == LANES
    steps = npages // n_pg

    def page_spec(i):
        return pl.BlockSpec((None, None, MLA_CACHE, page),
                            lambda b, j, pt: (layer, pt[b, j * n_pg + i], 0, 0))

    grid_spec = pltpu.PrefetchScalarGridSpec(
        num_scalar_prefetch=1,
        grid=(BS, steps),
        in_specs=[pl.BlockSpec((1, SUBLANES, MLA_CACHE), lambda b, j, pt: (b, 0, 0)),
                  pl.BlockSpec((1, MLA_CACHE, HEADS), lambda b, j, pt: (b, 0, 0)),
                  pl.BlockSpec((1, 1, MLA_CACHE), lambda b, j, pt: (b, 0, 0))]
                 + [page_spec(i) for i in range(n_pg)]
                 + [pl.BlockSpec(wv.shape, lambda b, j, pt: (0, 0))],
        out_specs=pl.BlockSpec((BS, BRANCH), lambda b, j, pt: (0, 0)),
        scratch_shapes=[pltpu.VMEM((HEADS, MLA_CACHE, LANES), F32),
                        pltpu.VMEM((SUBLANES, LANES), F32), pltpu.VMEM((SUBLANES, LANES), F32),
                        pltpu.VMEM((HEADS, MLA_KV_LORA, LANES), F32)],
    )
    return pl.pallas_call(
        functools.partial(_mla_decode_kernel, n_pg=n_pg),
        grid_spec=grid_spec,
        out_shape=jax.ShapeDtypeStruct((BS, BRANCH), F32),
        compiler_params=_cparams(("arbitrary", "arbitrary")),
        name="mla_decode",
    )(page_table, q8, qt, new_rows, *([cache_t] * n_pg), wv)


def _nsa_cmp_decode_kernel(pt_ref, qt_ref, *rest, n_pg, past):
    page_refs = rest[:n_pg]
    oc_ref, imp_ref, qb_ref, ts_ref, vb_ref = rest[n_pg:]
    del pt_ref
    j = pl.program_id(1)
    steps = ts_ref.shape[0]
    page = page_refs[0].shape[1]
    nblk = 2 * LANES

    @pl.when(j == 0)
    def _():
        qt = qt_ref[0]
        for h in range(HEADS):
            qb_ref[h] = jnp.broadcast_to(qt[:, h:h + 1], (DK, LANES))

    for t, r in enumerate(page_refs):
        kt = r[0:DK, :]
        for h in range(HEADS):
            ts_ref[j, h:h + 1, t * page:(t + 1) * page] = jnp.sum(kt * qb_ref[h], axis=0, keepdims=True)
        vb_ref[j, :, t * page:(t + 1) * page] = r[DK:2 * DK, :]

    @pl.when(j == steps - 1)
    def _():
        blk = lax.broadcasted_iota(I32, (HEADS, nblk), 1)
        first = lax.broadcasted_iota(I32, (HEADS, page), 1) < NSA_BLOCK
        s = jnp.zeros((HEADS, nblk), F32)
        for jj in range(steps):
            ts = ts_ref[jj, 0:HEADS, :]
            for t in range(n_pg):
                x = ts[:, t * page:(t + 1) * page]
                b0 = 2 * (jj * n_pg + t)
                s0 = jnp.sum(jnp.where(first, x, 0.0), axis=1, keepdims=True)
                s1 = jnp.sum(jnp.where(first, 0.0, x), axis=1, keepdims=True)
                s = jnp.where(blk == b0, s0, jnp.where(blk == b0 + 1, s1, s))
        s = s * (1.0 / NSA_BLOCK)
        cmask = (blk + 1) * NSA_BLOCK - 1 <= past
        p = _masked_softmax(s, cmask)
        cur = past // NSA_BLOCK
        imp = jnp.sum(p, axis=0, keepdims=True)
        blk1 = blk[0:1]
        forced = (blk1 == 0) | (blk1 == cur - 1)
        imp = imp + jnp.where(forced, NSA_FORCED_BONUS, 0.0)
        imp_ref[0] = jnp.where(blk1 >= cur, -3.0, imp)

        pw = jnp.concatenate([p * (1.0 / NSA_BLOCK), jnp.zeros((SUBLANES - HEADS, nblk), F32)], axis=0)
        nb_step = 2 * n_pg
        expand = jnp.where(lax.broadcasted_iota(I32, (nb_step, n_pg * page), 0)
                           == lax.broadcasted_iota(I32, (nb_step, n_pg * page), 1) // NSA_BLOCK, 1.0, 0.0)
        accs = [jnp.zeros((DK, page), F32) for _ in range(HEADS)]
        for jj in range(steps):
            w = _dot_f32(pw[:, nb_step * jj:nb_step * (jj + 1)], expand)
            for t in range(n_pg):
                v = vb_ref[jj, :, t * page:(t + 1) * page]
                for h in range(HEADS):
                    accs[h] = accs[h] + v * w[h:h + 1, t * page:(t + 1) * page]
        ones = jnp.ones((SUBLANES, LANES), F32)
        oc_ref[0] = jnp.concatenate([_dot_nt_f32(ones, a)[0:1, :] for a in accs], axis=0)


def _nsa_cmp_decode(layer, page_table, qt, cache_t, past):
    BS, npages = page_table.shape
    page = cache_t.shape[3]
    n_pg = min(PAGES_PER_STEP, npages)
    assert page == 2 * NSA_BLOCK and page == LANES and 2 * npages <= 2 * LANES and npages % n_pg == 0
    steps = npages // n_pg

    def page_spec(i):
        return pl.BlockSpec((None, None, 2 * DK, page),
                            lambda b, j, pt: (layer, pt[b, j * n_pg + i], 0, 0))

    grid_spec = pltpu.PrefetchScalarGridSpec(
        num_scalar_prefetch=1,
        grid=(BS, steps),
        in_specs=[pl.BlockSpec((1, DK, HEADS), lambda b, j, pt: (b, 0, 0))]
                 + [page_spec(i) for i in range(n_pg)],
        out_specs=[pl.BlockSpec((1, HEADS, DK), lambda b, j, pt: (b, 0, 0)),
                   pl.BlockSpec((1, 1, 2 * LANES), lambda b, j, pt: (b, 0, 0))],
        scratch_shapes=[pltpu.VMEM((HEADS, DK, LANES), F32),
                        pltpu.VMEM((steps, SUBLANES, n_pg * page), F32),
                        pltpu.VMEM((steps, DK, n_pg * page), F32)],
    )
    return pl.pallas_call(
        functools.partial(_nsa_cmp_decode_kernel, n_pg=n_pg, past=past),
        grid_spec=grid_spec,
        out_shape=[jax.ShapeDtypeStruct((BS, HEADS, DK), F32),
                   jax.ShapeDtypeStruct((BS, 1, 2 * LANES), F32)],
        compiler_params=_cparams(("parallel", "arbitrary")),
        name="nsa_cmp_decode",
    )(page_table, qt, *([cache_t] * n_pg))


def _topk_kernel(imp_ref, idx_ref, *, past, k_top):
    BS = imp_ref.shape[0]
    cur = past // NSA_BLOCK
    blk = lax.broadcasted_iota(I32, (BS, 2 * LANES), 1).astype(F32)
    lane_x = lax.broadcasted_iota(I32, (BS, LANES), 1)
    imp = jnp.concatenate([imp_ref[...], jnp.where(lane_x == 0, NSA_FORCED_BONUS, -3.0)], axis=1)
    blk = jnp.concatenate([blk, (cur + lane_x).astype(F32)], axis=1)
    out = jnp.zeros((BS, LANES), F32)
    for k in range(k_top):
        m = jnp.max(imp, axis=-1, keepdims=True)
        pick = jnp.min(jnp.where(imp == m, blk, 1e9), axis=-1, keepdims=True)
        imp = jnp.where(blk == pick, -4.0, imp)
        out = jnp.where(lane_x == k, pick, out)
    idx_ref[...] = out.astype(I32)


def _topk(imp, past, k_top):
    BS = imp.shape[0]
    return pl.pallas_call(
        functools.partial(_topk_kernel, past=past, k_top=k_top),
        out_shape=jax.ShapeDtypeStruct((BS, LANES), I32),
        name="nsa_topk",
    )(imp)


def _nsa_sel_decode_kernel(pt_ref, idx_ref, q_ref, pgn_ref, wn_ref, win_ref, g_ref, oc_ref, *rest,
                           k_top, n_blocks):
    blk_refs = rest[:k_top]
    (o_ref,) = rest[k_top:]
    del pt_ref
    b = pl.program_id(0)
    q = q_ref[0]
    page = blk_refs[0].shape[1]

    def attend(kt, vt, s_mask, k_new, v_new, new_on):
        s = _dot(q, kt)
        if s_mask is not None:
            s = jnp.where(s_mask, s, NEG)
        s_n = jnp.where(new_on, jnp.sum(q * k_new, axis=-1, keepdims=True), NEG)
        m = jnp.maximum(jnp.max(s, axis=-1, keepdims=True), s_n)
        e = jnp.exp(s - m)
        if s_mask is not None:
            e = jnp.where(s_mask, e, 0.0)
        e_n = jnp.where(new_on, jnp.exp(s_n - m), 0.0)
        l = jnp.sum(e, axis=-1, keepdims=True) + e_n
        return (_dot_nt(e, vt) + e_n * v_new) / l

    kt = jnp.concatenate([r[0:DK, :] for r in blk_refs], axis=1).astype(BF16)
    vt = jnp.concatenate([r[DK:2 * DK, :] for r in blk_refs], axis=1).astype(BF16)
    lane = lax.broadcasted_iota(I32, (1, k_top * page), 1)
    slot = lane // page
    half = (lane % page) // NSA_BLOCK
    valid = jnp.zeros((1, k_top * page), I32)
    has_new = jnp.zeros((), jnp.bool_)
    for k in range(k_top):
        ik = idx_ref[b, k]
        ok = jnp.where(half == (ik & 1), (ik < n_blocks).astype(I32), 0)
        valid = jnp.where(slot == k, ok, valid)
        has_new = has_new | (ik == n_blocks)
    pgn = pgn_ref[0]
    o_s = attend(kt, vt, valid > 0, pgn[:, 2 * DK:3 * DK], pgn[:, 3 * DK:4 * DK], has_new)
    wn = wn_ref[0]
    o_w = attend(win_ref[0:DK, :].astype(BF16), win_ref[DK:2 * DK, :].astype(BF16), None,
                 wn[:, 0:DK], wn[:, DK:2 * DK], True)
    o_c = oc_ref[0]
    g = g_ref[0]
    outs = []
    for h in range(HEADS):
        outs.append(g[:, 3 * h:3 * h + 1] * o_c[h:h + 1] + g[:, 3 * h + 1:3 * h + 2] * o_s[h:h + 1]
                    + g[:, 3 * h + 2:3 * h + 3] * o_w[h:h + 1])
    o_ref[pl.ds(b, 1), :] = jnp.concatenate(outs, axis=1)


def _nsa_sel_decode(layer, page_table, idx, q8, pg_new, w_new, win_t, g_new, o_c, cache_t, k_top):
    BS, npages = page_table.shape
    n_blocks = 2 * npages
    page = cache_t.shape[3]
    nwin = win_t.shape[3]
    blk = jnp.minimum(idx[:, :k_top], n_blocks - 1)
    phys = jnp.take_along_axis(page_table, blk // 2, axis=1)

    def blk_spec(k):
        return pl.BlockSpec((None, None, 2 * DK, page), lambda b, ph, ix: (layer, ph[b, k], 1, 0))

    row3 = lambda w: pl.BlockSpec((1, 1, w), lambda b, pt, ix: (b, 0, 0))
    grid_spec = pltpu.PrefetchScalarGridSpec(
        num_scalar_prefetch=2,
        grid=(BS,),
        in_specs=[pl.BlockSpec((1, SUBLANES, DK), lambda b, pt, ix: (b, 0, 0)),
                  row3(256), row3(LANES),
                  pl.BlockSpec((None, None, 2 * DK, nwin), lambda b, pt, ix: (layer, b, 0, 0)),
                  row3(LANES),
                  pl.BlockSpec((1, HEADS, DK), lambda b, pt, ix: (b, 0, 0))]
                 + [blk_spec(k) for k in range(k_top)],
        out_specs=pl.BlockSpec((BS, BRANCH), lambda b, pt, ix: (0, 0)),
    )
    return pl.pallas_call(
        functools.partial(_nsa_sel_decode_kernel, k_top=k_top, n_blocks=n_blocks),
        grid_spec=grid_spec,
        out_shape=jax.ShapeDtypeStruct((BS, BRANCH), F32),
        compiler_params=_cparams(("arbitrary",)),
        name="nsa_sel_decode",
    )(phys, idx, q8, pg_new, w_new, win_t, g_new, o_c, *([cache_t] * k_top))


def _column(row, eye):
    return jnp.sum(jnp.where(eye, jnp.broadcast_to(row, eye.shape), 0.0), axis=1, keepdims=True)


def _rec_decode_kernel(xbc_ref, cbuf_ref, dt_ref, z_ref, hg_ref, s_ref, hs_ref,
                       cw_ref, cb_ref, dtb_ref, alog_ref, dsk_ref, sn_ref, lb_ref, hn_ref,
                       os_ref, oh_ref, sn_out, hs_out):
    b = pl.program_id(0)
    eye = lax.broadcasted_iota(I32, (DK, DK), 0) == lax.broadcasted_iota(I32, (DK, DK), 1)

    cbuf = cbuf_ref[...]
    xn = xbc_ref[0]
    conv = cb_ref[...] + cw_ref[SSD_CONV - 1:SSD_CONV, :] * xn
    for j in range(SSD_CONV - 1):
        conv = conv + cw_ref[j:j + 1, :] * cbuf[j:j + 1, :]
    xa = _silu(conv)
    dt = _softplus(dt_ref[0] + dtb_ref[...])
    ea = jnp.exp(dt * (-jnp.exp(alog_ref[...])))
    dsk = dsk_ref[...]
    ys = []
    for h in range(HEADS):
        g = h // (HEADS // SSD_GROUPS)
        xs = xa[:, DK * h:DK * (h + 1)]
        bm = xa[:, BRANCH + DK * g:BRANCH + DK * (g + 1)]
        cm = xa[:, BRANCH + DK * (SSD_GROUPS + g):BRANCH + DK * (SSD_GROUPS + g + 1)]
        xdt_col = _column(xs * dt[:, h:h + 1], eye)
        s_old = s_ref[h]
        ea_h = ea[:, h:h + 1]
        y_col = (jnp.sum(cm * bm, axis=-1, keepdims=True) * xdt_col
                 + jnp.sum(s_old * cm, axis=-1, keepdims=True) * ea_h)
        sn_out[0, h] = s_old * ea_h + xdt_col * bm
        y_row = jnp.sum(jnp.where(eye, jnp.broadcast_to(y_col, (DK, DK)), 0.0), axis=0, keepdims=True)
        ys.append(y_row + dsk[:, h:h + 1] * xs)
    y = jnp.concatenate(ys, axis=1)
    os_ref[pl.ds(b, 1), :] = _rms(y * _silu(z_ref[0]), sn_ref[...])

    W = BRANCH
    hg = hg_ref[0]
    q = hg[:, 0:W]
    v = hg[:, 2 * W:3 * W]
    log_f, kin = _hgrn_gates(hg[:, W:2 * W], lb_ref[...])
    ef = jnp.exp(log_f)
    hn = hn_ref[...]
    outs = []
    for h in range(HEADS):
        r = slice(DK * h, DK * (h + 1))
        s_old = hs_ref[h]
        att = jnp.sum(q[:, r] * kin[:, r], axis=-1, keepdims=True)
        y = att * v[:, r] + jnp.sum(s_old * _column(q[:, r] * ef[:, r], eye), axis=0, keepdims=True)
        hs_out[0, h] = s_old * _column(ef[:, r], eye) + _column(kin[:, r], eye) * v[:, r]
        outs.append(_rms(y, hn[:, r]))
    o = jnp.concatenate(outs, axis=1)
    oh_ref[pl.ds(b, 1), :] = o * _silu(hg[:, 3 * W:4 * W])


def _rec_decode(layer, xbc_new, conv_state, dt_new, z_new, hg_new, state_ssd, state_hgrn,
                cw, cb, dtb, alog, dsk, sn, lb, hn):
    BS = xbc_new.shape[0]
    row3 = lambda w: pl.BlockSpec((1, 1, w), lambda b: (b, 0, 0))
    st_in = pl.BlockSpec((None, None, HEADS, DK, DK), lambda b: (layer, b, 0, 0, 0))
    st_out = pl.BlockSpec((1, HEADS, DK, DK), lambda b: (b, 0, 0, 0))
    o_spec = pl.BlockSpec((BS, BRANCH), lambda b: (0, 0))
    consts = (cw, cb, dtb, alog, dsk, sn, lb, hn)
    return pl.pallas_call(
        _rec_decode_kernel,
        grid=(BS,),
        in_specs=[row3(SSD_CONV_DIM),
                  pl.BlockSpec((None, None, SSD_CONV - 1, SSD_CONV_DIM), lambda b: (layer, b, 0, 0)),
                  row3(LANES), row3(BRANCH), row3(4 * BRANCH), st_in, st_in]
                 + [_const_spec(c.shape) for c in consts],
        out_specs=[o_spec, o_spec, st_out, st_out],
        out_shape=[jax.ShapeDtypeStruct((BS, BRANCH), F32), jax.ShapeDtypeStruct((BS, BRANCH), F32),
                   jax.ShapeDtypeStruct((BS, HEADS, DK, DK), F32),
                   jax.ShapeDtypeStruct((BS, HEADS, DK, DK), F32)],
        compiler_params=_cparams(("arbitrary",)),
        name="rec_decode",
    )(xbc_new, conv_state, dt_new, z_new, hg_new, state_ssd, state_hgrn, *consts)


def _rot_cols(w):
    half = MLA_ROPE // 2
    return jnp.concatenate([-w[..., half:], w[..., :half]], axis=-1)


def _pad_cols(w, width):
    return jnp.pad(w, [(0, 0)] * (w.ndim - 1) + [(0, width - w.shape[-1])])


def _pack_w_in(w):
    sizes = (256, 384, 12, MLA_Q_LORA, MLA_KV_LORA, MLA_ROPE, 256, SSD_CONV_DIM, HEADS, 256, 256, 256, 256)
    offs = np.cumsum((0,) + sizes)
    (nsa_q, nsa_kv, nsa_g, cq, ckv, kpe, z, xbc, dt, hq, hf, hi, hgate) = (
        w[:, offs[i]:offs[i + 1]] for i in range(len(sizes)))
    gate = w[:, offs[-1]:]
    packed = jnp.concatenate([
        nsa_q, nsa_kv[:, :256], nsa_kv[:, 256:], _pad_cols(nsa_g, LANES), cq, ckv,
        _pad_cols(kpe, LANES), _pad_cols(_rot_cols(kpe), LANES), z, xbc, _pad_cols(dt, LANES),
        hq, hf, hi, hgate], axis=1)
    assert packed.shape[1] == PACKED_WIDTH
    return packed.astype(BF16), gate.astype(BF16)


def _pack_mla(wuq, wukv):
    nope = wuq[:, :, :MLA_NOPE].reshape(MLA_Q_LORA, HEADS * MLA_NOPE)
    pe = wuq[:, :, MLA_NOPE:]
    pe_p = _pad_cols(pe, LANES).reshape(MLA_Q_LORA, HEADS * LANES)
    pr_p = _pad_cols(_rot_cols(pe), LANES).reshape(MLA_Q_LORA, HEADS * LANES)
    wuq_p = jnp.concatenate([nope, pe_p, pr_p], axis=1).astype(BF16)
    eye = jnp.eye(HEADS, dtype=wukv.dtype)
    wukn = jnp.einsum('rhn,hg->hngr', wukv[:, :, :MLA_NOPE], eye).reshape(HEADS * MLA_NOPE, HEADS * MLA_KV_LORA)
    wv = jnp.einsum('rhv,hg->hrgv', wukv[:, :, MLA_NOPE:], eye).reshape(HEADS * MLA_KV_LORA, HEADS * MLA_V)
    return wuq_p, wukn.astype(BF16), wv.astype(BF16)


def _rope_table(positions):
    half = MLA_ROPE // 2
    freq = ROPE_THETA ** (-jnp.arange(half, dtype=F32) / half)
    ang = positions.astype(F32)[:, None] * freq
    cos = jnp.concatenate([jnp.cos(ang), jnp.cos(ang)], axis=1)
    sin = jnp.concatenate([jnp.sin(ang), jnp.sin(ang)], axis=1)
    return jnp.concatenate([_pad_cols(cos, LANES), _pad_cols(sin, LANES)], axis=1)


def _pad_row(v, width=LANES):
    return _pad_cols(v.reshape(1, -1).astype(F32), width)


def kernel(x_prompt, x_sample, cache_nsa_kv, cache_mla, cache_nsa_win, state_ssd, state_ssd_conv, state_hgrn, page_table, norm_g, ffn_w_in, ffn_w_out, w_in, mla_q_norm, mla_kv_norm, mla_w_uq, mla_w_ukv, ssd_conv_w, ssd_conv_b, ssd_dt_bias, ssd_a_log, ssd_d, ssd_norm, hg_lb_logits, hg_norm, w_branch, w_out):
    BP, T, D = x_prompt.shape
    BS = x_sample.shape[0]
    depth = w_in.shape[0]
    n_pool, page = cache_nsa_kv.shape[1], cache_nsa_kv.shape[2]
    npages = page_table.shape[1]
    past = npages * page
    n_win = cache_nsa_win.shape[2]
    rows_p = BP * T
    assert D == D_MODEL and x_sample.shape[1] == 1 and BS % SUBLANES == 0
    assert n_win == min(NSA_WINDOW, past) and past % NSA_BLOCK == 0
    k_top = min(NSA_TOPK, past // NSA_BLOCK + 1)

    xp = x_prompt.reshape(rows_p, D)
    xs = x_sample.reshape(BS, D)
    cs_p = _rope_table(jnp.tile(jnp.arange(T, dtype=I32), BP))
    cs_s = _rope_table(jnp.full((BS,), past, I32))
    lbp = jax.nn.softmax(hg_lb_logits.astype(F32), axis=0)
    lower_bounds = jnp.cumsum(lbp, axis=0) - lbp[0]

    nsa_t = jnp.transpose(cache_nsa_kv, (0, 1, 3, 4, 2)).reshape(depth, n_pool, 4 * DK, page)
    mla_t = jnp.transpose(cache_mla, (0, 1, 3, 2))
    win_t = jnp.transpose(cache_nsa_win, (0, 1, 3, 4, 2)).reshape(depth, BS, 2 * DK, n_win)

    def row3(a):
        return a.reshape(BS, 1, a.shape[1])

    outs = [[] for _ in range(12)]
    for l in range(depth):
        g = norm_g[l].astype(F32)
        grow = lambda i: g[i].reshape(1, D)
        wi = ffn_w_in[l].astype(BF16)
        wo = ffn_w_out[l].astype(BF16)
        w_pack, w_gate = _pack_w_in(w_in[l])
        wuq_p, wukn, wv = _pack_mla(mla_w_uq[l], mla_w_ukv[l])
        mqn = mla_q_norm[l].reshape(1, -1)
        mkvn = mla_kv_norm[l].reshape(1, -1)

        xp = _ffn(xp, grow(0), grow(1), wi[0], wo[0])
        xs = _ffn(xs, grow(0), grow(1), wi[0], wo[0])
        (nsa_q, nsa_pg, nsa_w, nsa_g, qlat, qpe, mla_new, ssd_z, ssd_xbc, ssd_dt, hg) = _inproj(
            xp, grow(2), cs_p, w_pack, mqn, mkvn, wuq_p, wukn)
        (nsa_q_s, nsa_pg_s, nsa_w_s, nsa_g_s, qlat_s, qpe_s, mla_new_s, ssd_z_s, ssd_xbc_s, ssd_dt_s,
         hg_s) = _inproj(xs, grow(2), cs_s, w_pack, mqn, mkvn, wuq_p, wukn)

        o_nsa = _nsa_prompt(nsa_q, nsa_pg, nsa_w, nsa_g, BP, T)
        pad8 = lambda a: jnp.pad(a, ((0, 0), (0, SUBLANES - HEADS), (0, 0)))
        q4 = nsa_q_s.reshape(BS, HEADS, DK)
        q8 = pad8(q4)
        o_c, imp = _nsa_cmp_decode(l, page_table, jnp.transpose(q4, (0, 2, 1)), nsa_t, past)
        idx = _topk(imp.reshape(BS, 2 * LANES), past, k_top)
        o_nsa_s = _nsa_sel_decode(l, page_table, idx, q8, row3(nsa_pg_s), row3(nsa_w_s), win_t,
                                  row3(nsa_g_s), o_c, nsa_t, k_top)

        o_mla = _mla_prompt(qlat, qpe, mla_new, wv, BP, T)
        q_cat = jnp.concatenate([qlat_s.reshape(BS, HEADS, LANES),
                                 qpe_s.reshape(BS, HEADS, LANES)[:, :, :MLA_ROPE]], axis=-1)
        o_mla_s = _mla_decode(l, page_table, pad8(q_cat), jnp.transpose(q_cat, (0, 2, 1)),
                              row3(mla_new_s), mla_t, wv)

        cw = ssd_conv_w[l].astype(F32)
        cb = ssd_conv_b[l].reshape(1, -1).astype(F32)
        dtb = _pad_row(ssd_dt_bias[l])
        alog = _pad_row(ssd_a_log[l])
        dsk = _pad_row(ssd_d[l])
        sn = ssd_norm[l].reshape(1, -1).astype(F32)
        lb = lower_bounds[l].reshape(1, -1)
        hn = hg_norm[l].reshape(1, -1).astype(F32)
        o_ssd, ssd_p = _ssd_prompt(ssd_xbc, ssd_dt, ssd_z, cw, cb, dtb, alog, dsk, sn, BP, T)
        o_hg, hgrn_p = _hgrn_prompt(hg, lb, hn, BP, T)
        o_ssd_s, o_hg_s, ssd_s, hgrn_s = _rec_decode(
            l, row3(ssd_xbc_s), state_ssd_conv, row3(ssd_dt_s), row3(ssd_z_s), row3(hg_s), state_ssd,
            state_hgrn, cw, cb, dtb, alog, dsk, sn, lb, hn)

        wbr = w_branch[l].astype(BF16)
        wout = w_out[l].astype(BF16)
        xp = _merge(xp, (o_nsa, o_mla, o_ssd, o_hg), grow(2), grow(3), w_gate, wbr, wout)
        xs = _merge(xs, (o_nsa_s, o_mla_s, o_ssd_s, o_hg_s), grow(2), grow(3), w_gate, wbr, wout)
        xp = _ffn(xp, grow(4), grow(5), wi[1], wo[1])
        xs = _ffn(xs, grow(4), grow(5), wi[1], wo[1])

        n_keep = min(NSA_WINDOW, T)
        layer_out = (
            nsa_pg.reshape(BP, T, 4, DK), nsa_pg_s.reshape(BS, 1, 4, DK),
            mla_new.reshape(BP, T, MLA_CACHE), mla_new_s.reshape(BS, 1, MLA_CACHE),
            nsa_w.reshape(BP, T, 2, DK)[:, T - n_keep:],
            jnp.concatenate([cache_nsa_win[l], nsa_w_s.reshape(BS, 1, 2, DK)], axis=1)[:, 1:],
            ssd_p, ssd_s,
            ssd_xbc.reshape(BP, T, SSD_CONV_DIM)[:, T - (SSD_CONV - 1):],
            jnp.concatenate([state_ssd_conv[l], ssd_xbc_s.reshape(BS, 1, SSD_CONV_DIM)], axis=1)[:, 1:],
            hgrn_p, hgrn_s)
        for acc, o in zip(outs, layer_out):
            acc.append(o)

    stacked = tuple(jnp.stack(o) for o in outs)
    return (xp.reshape(BP, T, D), xs.reshape(BS, 1, D)) + stacked
```

```python
import functools
import math

import numpy as np
import jax
import jax.numpy as jnp
from jax import lax
from jax.experimental import pallas as pl
from jax.experimental.pallas import tpu as pltpu

F32 = jnp.float32
BF16 = jnp.bfloat16
I32 = jnp.int32
HIGHEST = lax.Precision.HIGHEST

EPS = 1e-6
NEG = -1e30
LB_FLOOR = 1e-20
ROPE_THETA = 10000.0

D_MODEL = 1024
D_FF = 2816
HEADS = 4
DK = 64
BRANCH = HEADS * DK
NSA_BLOCK = 64
NSA_TOPK = 16
NSA_WINDOW = 512
NSA_FORCED_BONUS = float(HEADS + 1)
MLA_Q_LORA = 256
MLA_KV_LORA = 128
MLA_NOPE = 64
MLA_ROPE = 32
MLA_V = 64
MLA_CACHE = MLA_KV_LORA + MLA_ROPE
SSD_GROUPS = 2
SSD_CONV = 4
SSD_CONV_DIM = BRANCH + 2 * SSD_GROUPS * DK
CHUNK = 128
QB = 128

LANES = 128
SUBLANES = 8
VMEM_LIMIT_BYTES = 56 * 1024 * 1024

_SEG = {}
_off = 0
for _name, _w in (("nsa_q", 256), ("nsa_pg", 256), ("nsa_w", 128), ("nsa_g", 128),
                  ("mla_cq", 256), ("mla_ckv", 128), ("mla_kpe", 128), ("mla_kpr", 128),
                  ("ssd_z", 256), ("ssd_xbc", 512), ("ssd_dt", 128), ("hg", 1024)):
    _SEG[_name] = (_off, _off + _w)
    _off += _w
PACKED_WIDTH = _off


def _cparams(sem):
    return pltpu.CompilerParams(dimension_semantics=sem, vmem_limit_bytes=VMEM_LIMIT_BYTES)


def _rms(x, g):
    return x * lax.rsqrt(jnp.mean(x * x, axis=-1, keepdims=True) + EPS) * g


def _dot(a, b):
    return jnp.dot(a.astype(BF16), b.astype(BF16), preferred_element_type=F32)


def _dot_nt(a, b):
    return lax.dot_general(a.astype(BF16), b.astype(BF16), (((1,), (1,)), ((), ())),
                           preferred_element_type=F32)


def _dot_f32(a, b):
    return jnp.dot(a, b, precision=HIGHEST, preferred_element_type=F32)


def _dot_nt_f32(a, b):
    return lax.dot_general(a, b, (((1,), (1,)), ((), ())), precision=HIGHEST,
                           preferred_element_type=F32)


def _sigmoid(x):
    return 1.0 / (1.0 + jnp.exp(-x))


def _silu(x):
    return x * _sigmoid(x)


def _softplus(x):
    return jnp.maximum(x, 0.0) + jnp.log1p(jnp.exp(-jnp.abs(x)))


def _masked_softmax(s, mask):
    s = jnp.where(mask, s, NEG)
    m = jnp.max(s, axis=-1, keepdims=True)
    e = jnp.exp(s - m)
    r = 1.0 / jnp.sum(e, axis=-1, keepdims=True)
    return jnp.where(mask, e * r, 0.0)


def _unnormalised_softmax(s, mask):
    s = jnp.where(mask, s, NEG)
    e = jnp.exp(s - jnp.max(s, axis=-1, keepdims=True))
    return e, 1.0 / jnp.sum(e, axis=-1, keepdims=True)


def _row_tile(rows, cap=864):
    best = SUBLANES
    for t in range(SUBLANES, min(rows, cap) + 1, SUBLANES):
        if rows % t == 0:
            best = t
    assert rows % best == 0
    return best


def _const_spec(shape):
    nd = len(shape)
    return pl.BlockSpec(shape, lambda *_: (0,) * nd)


FFN_CHUNKS = 2


def _ffn_kernel(x_ref, ga_ref, gb_ref, wi_ref, wo_ref, o_ref):
    x = x_ref[...]
    h = _rms(x, ga_ref[...]).astype(BF16)
    fc = D_FF // FFN_CHUNKS
    acc = None
    for c in range(FFN_CHUNKS):
        gate = jnp.dot(h, wi_ref[:, fc * c:fc * (c + 1)], preferred_element_type=F32)
        up = jnp.dot(h, wi_ref[:, D_FF + fc * c:D_FF + fc * (c + 1)], preferred_element_type=F32)
        part = _dot(_silu(gate) * up, wo_ref[fc * c:fc * (c + 1), :])
        acc = part if acc is None else acc + part
    o_ref[...] = x + 0.5 * _rms(acc, gb_ref[...])


def _resident_spec(shape):
    nd = len(shape)
    return pl.BlockSpec(shape, lambda *_: (0,) * nd, pipeline_mode=pl.Buffered(1))


def _ffn(x, ga, gb, wi, wo):
    rows = x.shape[0]
    tm = _row_tile(rows, cap=512)
    assert (D_FF // FFN_CHUNKS) % LANES == 0
    return pl.pallas_call(
        _ffn_kernel,
        grid=(rows // tm,),
        in_specs=[
            pl.BlockSpec((tm, D_MODEL), lambda i: (i, 0)),
            _const_spec((1, D_MODEL)),
            _const_spec((1, D_MODEL)),
            _resident_spec(wi.shape),
            _resident_spec(wo.shape),
        ],
        out_specs=pl.BlockSpec((tm, D_MODEL), lambda i: (i, 0)),
        out_shape=jax.ShapeDtypeStruct((rows, D_MODEL), F32),
        compiler_params=_cparams(("parallel",)),
        name="ffn",
    )(x, ga, gb, wi, wo)


def _inproj_kernel(x_ref, g_ref, cs_ref, w_ref, mqn_ref, mkvn_ref, wuq_ref, wukn_ref,
                   q_o, pg_o, w_o, g_o, qlat_o, qpe_o, mla_o, z_o, xbc_o, dt_o, hg_o):
    hn = _rms(x_ref[...], g_ref[...]).astype(BF16)

    def seg(name):
        a, b = _SEG[name]
        return jnp.dot(hn, w_ref[:, a:b], preferred_element_type=F32)

    q_o[...] = seg("nsa_q") * (DK ** -0.5)
    pg_o[...] = seg("nsa_pg")
    w_o[...] = seg("nsa_w")
    g_o[...] = _sigmoid(seg("nsa_g"))
    z_o[...] = seg("ssd_z")
    xbc_o[...] = seg("ssd_xbc")
    dt_o[...] = seg("ssd_dt")
    hg_o[...] = seg("hg")

    cos_p = cs_ref[:, 0:LANES]
    sin_p = cs_ref[:, LANES:2 * LANES]
    cqn = _rms(seg("mla_cq"), mqn_ref[...])
    qf = _dot(cqn, wuq_ref[...])
    nope_w = HEADS * MLA_NOPE
    qlat_o[...] = _dot(qf[:, 0:nope_w], wukn_ref[...])
    for h in range(HEADS):
        pe = qf[:, nope_w + LANES * h: nope_w + LANES * (h + 1)]
        pr = qf[:, nope_w + LANES * (HEADS + h): nope_w + LANES * (HEADS + h + 1)]
        qpe_o[:, LANES * h:LANES * (h + 1)] = pe * cos_p + pr * sin_p
    mla_o[:, 0:MLA_KV_LORA] = _rms(seg("mla_ckv"), mkvn_ref[...])
    kpe = seg("mla_kpe") * cos_p + seg("mla_kpr") * sin_p
    mla_o[:, MLA_KV_LORA:MLA_CACHE] = kpe[:, 0:MLA_ROPE]


_INPROJ_OUT_WIDTHS = (256, 256, 128, 128, 512, 512, MLA_CACHE, 256, 512, 128, 1024)


def _inproj(x, g, cs, w, mqn, mkvn, wuq, wukn):
    rows = x.shape[0]
    tm = _row_tile(rows)
    row_spec = lambda c: pl.BlockSpec((tm, c), lambda i: (i, 0))
    return pl.pallas_call(
        _inproj_kernel,
        grid=(rows // tm,),
        in_specs=[row_spec(D_MODEL), _const_spec((1, D_MODEL)), row_spec(2 * LANES),
                  _const_spec(w.shape), _const_spec(mqn.shape), _const_spec(mkvn.shape),
                  _const_spec(wuq.shape), _const_spec(wukn.shape)],
        out_specs=[row_spec(c) for c in _INPROJ_OUT_WIDTHS],
        out_shape=[jax.ShapeDtypeStruct((rows, c), F32) for c in _INPROJ_OUT_WIDTHS],
        compiler_params=_cparams(("parallel",)),
        name="inproj",
    )(x, g, cs, w, mqn, mkvn, wuq, wukn)


def _merge_kernel(x_ref, b0_ref, b1_ref, b2_ref, b3_ref, g2_ref, g3_ref, wg_ref, wbr_ref, wout_ref,
                  o_ref):
    x = x_ref[...]
    hn = _rms(x, g2_ref[...]).astype(BF16)
    acc = None
    for k, b_ref in enumerate((b0_ref, b1_ref, b2_ref, b3_ref)):
        gate = _sigmoid(jnp.dot(hn, wg_ref[:, D_MODEL * k:D_MODEL * (k + 1)],
                                preferred_element_type=F32))
        u = _dot(b_ref[...], wbr_ref[k])
        acc = gate * u if acc is None else acc + gate * u
    out = _dot(acc, wout_ref[...])
    o_ref[...] = x + _rms(out, g3_ref[...])


def _merge(x, branches, g2, g3, wgate, wbr, wout):
    rows = x.shape[0]
    tm = _row_tile(rows, cap=512)
    row_spec = lambda c: pl.BlockSpec((tm, c), lambda i: (i, 0))
    return pl.pallas_call(
        _merge_kernel,
        grid=(rows // tm,),
        in_specs=[row_spec(D_MODEL)] + [row_spec(BRANCH)] * 4 +
                 [_const_spec((1, D_MODEL)), _const_spec((1, D_MODEL)),
                  _const_spec(wgate.shape), _const_spec(wbr.shape), _const_spec(wout.shape)],
        out_specs=row_spec(D_MODEL),
        out_shape=jax.ShapeDtypeStruct((rows, D_MODEL), F32),
        compiler_params=_cparams(("parallel",)),
        name="merge",
    )(x, *branches, g2, g3, wgate, wbr, wout)


KEY_STEP = 512


def _key_extents(T):
    his = sorted({min(T, KEY_STEP * (c + 1)) for c in range(-(-T // KEY_STEP))})
    return list(zip([0] + his[:-1], his))


def _stack_heads(x, width, take):
    return jnp.concatenate([x[:, width * h: width * h + take] for h in range(HEADS)], axis=0)


def _nsa_prompt_kernel(q_ref, pg_ref, wk_ref, g_ref, e_ref, o_ref, *, T, band):
    nb = T // NSA_BLOCK
    k_top = min(NSA_TOPK, nb)
    q0 = pl.program_id(1) * QB
    q = q_ref[...]
    zeros = jnp.zeros((QB, DK), F32)
    q_rows = jnp.concatenate(
        [jnp.concatenate([q[:, DK * h:DK * (h + 1)], zeros], axis=1) for h in range(HEADS)],
        axis=0).astype(BF16)
    pos = q0 + lax.broadcasted_iota(I32, (QB, 1), 0)
    lane = lax.broadcasted_iota(I32, (QB, LANES), 1)

    def per_head(fn):
        return jnp.concatenate([fn(h) for h in range(HEADS)], axis=0)

    def attend_values(s, mask, kv):
        parts = [_unnormalised_softmax(s[QB * h:QB * (h + 1)], mask) for h in range(HEADS)]
        e = jnp.concatenate([p[0] for p in parts], axis=0)
        r = jnp.concatenate([p[1] for p in parts], axis=0)
        return _dot(e, kv) * r

    cm = jnp.sum(pg_ref[:, 0:LANES].reshape(nb, NSA_BLOCK, LANES), axis=1) * (1.0 / NSA_BLOCK)
    if nb < LANES:
        cm = jnp.concatenate([cm, jnp.zeros((LANES - nb, LANES), F32)], axis=0)
    cm = cm.astype(BF16)
    s_c = _dot_nt(q_rows, cm)
    cmask = (lane + 1) * NSA_BLOCK - 1 <= pos
    p_c = per_head(lambda h: _masked_softmax(s_c[QB * h:QB * (h + 1)], cmask))
    o_c = _dot(p_c, cm)

    cur = pos // NSA_BLOCK
    imp = p_c[0:QB] + p_c[QB:2 * QB] + p_c[2 * QB:3 * QB] + p_c[3 * QB:4 * QB]
    forced = (lane == 0) | (lane == cur) | (lane == cur - 1)
    imp = imp + jnp.where(forced, NSA_FORCED_BONUS, 0.0)
    imp = jnp.where(lane > cur, -1.0, imp)
    def selected(tk):
        rank = jnp.zeros((QB, LANES), F32)
        for n in range(tk // NSA_BLOCK):
            c = imp[:, n:n + 1]
            beats = (c > imp) | ((c == imp) & (lane > n))
            rank = rank + jnp.where(beats, 1.0, 0.0)
        sel = jnp.where(rank < float(k_top), 1.0, 0.0)
        in_sel = jnp.dot(sel.astype(BF16), e_ref[:, 0:tk], preferred_element_type=F32)
        kpos = lax.broadcasted_iota(I32, (QB, tk), 1)
        smask = (in_sel > 0.5) & (kpos <= pos)
        kv_s = pg_ref[0:tk, LANES:2 * LANES].astype(BF16)
        s_s = _dot_nt(q_rows, kv_s)
        return attend_values(s_s, smask, kv_s)

    start = pl.multiple_of(jnp.clip(q0 - NSA_WINDOW, 0, T - band), LANES)
    kv_w = wk_ref[pl.ds(start, band), :].astype(BF16)
    kposw = start + lax.broadcasted_iota(I32, (QB, band), 1)
    wmask = (kposw <= pos) & (kposw >= pos - NSA_WINDOW)
    o_w = attend_values(_dot_nt(q_rows, kv_w), wmask, kv_w)

    g = g_ref[...]

    def finish(o_s):
        outs = []
        for h in range(HEADS):
            r = slice(QB * h, QB * (h + 1))
            comb = (g[:, 3 * h:3 * h + 1] * o_c[r] + g[:, 3 * h + 1:3 * h + 2] * o_s[r]
                    + g[:, 3 * h + 2:3 * h + 3] * o_w[r])
            outs.append(comb[:, DK:2 * DK])
        o_ref[...] = jnp.concatenate(outs, axis=1)

    for lo, tk in _key_extents(T):
        @pl.when((q0 + QB > lo) & (q0 + QB <= tk))
        def _(tk=tk):
            finish(selected(tk))


def _nsa_prompt(q, pg, wkv, g, B, T):
    assert T % QB == 0 and T // NSA_BLOCK <= LANES
    rows = B * T
    nq = T // QB
    band = min(NSA_WINDOW + QB, T)
    nb = T // NSA_BLOCK
    expand = (np.arange(LANES)[:, None] == (np.arange(T)[None, :] // NSA_BLOCK)) & (np.arange(LANES)[:, None] < nb)
    expand = jnp.asarray(expand, BF16)
    return pl.pallas_call(
        functools.partial(_nsa_prompt_kernel, T=T, band=band),
        grid=(B, nq),
        in_specs=[pl.BlockSpec((QB, BRANCH), lambda b, i: (b * nq + i, 0)),
                  pl.BlockSpec((T, 256), lambda b, i: (b, 0)),
                  pl.BlockSpec((T, LANES), lambda b, i: (b, 0)),
                  pl.BlockSpec((QB, LANES), lambda b, i: (b * nq + i, 0)),
                  _const_spec((LANES, T))],
        out_specs=pl.BlockSpec((QB, BRANCH), lambda b, i: (b * nq + i, 0)),
        out_shape=jax.ShapeDtypeStruct((rows, BRANCH), F32),
        compiler_params=_cparams(("parallel", "parallel")),
        name="nsa_prompt",
    )(q, pg, wkv, g, expand)


def _mla_prompt_kernel(qlat_ref, qpe_ref, kv_ref, wv_ref, o_ref, *, T):
    q0 = pl.program_id(1) * QB
    q_cat = jnp.concatenate([_stack_heads(qlat_ref[...], LANES, LANES),
                             _stack_heads(qpe_ref[...], LANES, MLA_ROPE)], axis=1).astype(BF16)
    scale = (MLA_NOPE + MLA_ROPE) ** -0.5
    pos = q0 + lax.broadcasted_iota(I32, (QB, 1), 0)

    def attend(tk):
        kv = kv_ref[0:tk, :].astype(BF16)
        ckv = kv[:, 0:MLA_KV_LORA]
        s = _dot_nt(q_cat, kv) * scale
        mask = lax.broadcasted_iota(I32, (QB, tk), 1) <= pos
        parts = [_unnormalised_softmax(s[QB * h:QB * (h + 1)], mask) for h in range(HEADS)]
        e = jnp.concatenate([p[0] for p in parts], axis=0)
        r = jnp.concatenate([p[1] for p in parts], axis=0)
        o_lat = _dot(e, ckv) * r
        o_cat = jnp.concatenate([o_lat[QB * h:QB * (h + 1)] for h in range(HEADS)], axis=1)
        o_ref[...] = _dot(o_cat, wv_ref[...])

    for lo, tk in _key_extents(T):
        @pl.when((q0 + QB > lo) & (q0 + QB <= tk))
        def _(tk=tk):
            attend(tk)


def _mla_prompt(qlat, qpe, mla_new, wv, B, T):
    nq = T // QB
    rows = B * T
    return pl.pallas_call(
        functools.partial(_mla_prompt_kernel, T=T),
        grid=(B, nq),
        in_specs=[pl.BlockSpec((QB, 512), lambda b, i: (b * nq + i, 0)),
                  pl.BlockSpec((QB, 512), lambda b, i: (b * nq + i, 0)),
                  pl.BlockSpec((T, MLA_CACHE), lambda b, i: (b, 0)),
                  _const_spec(wv.shape)],
        out_specs=pl.BlockSpec((QB, BRANCH), lambda b, i: (b * nq + i, 0)),
        out_shape=jax.ShapeDtypeStruct((rows, BRANCH), F32),
        compiler_params=_cparams(("parallel", "parallel")),
        name="mla_prompt",
    )(qlat, qpe, mla_new, wv)


def _head_lanes(x_cols, shape):
    lane = lax.broadcasted_iota(I32, shape, 1)
    out = jnp.broadcast_to(x_cols[:, HEADS - 1:HEADS], shape)
    for h in range(HEADS - 2, -1, -1):
        out = jnp.where(lane < DK * (h + 1), jnp.broadcast_to(x_cols[:, h:h + 1], shape), out)
    return out


def _ssd_prompt_kernel(xbc_ref, dt_ref, z_ref, cw_ref, cb_ref, dtb_ref, alog_ref, dsk_ref, sn_ref,
                       o_ref, st_ref, buf_ref, s_ref):
    c = pl.program_id(1)
    C = CHUNK

    @pl.when(c == 0)
    def _():
        buf_ref[0:SUBLANES, :] = jnp.zeros((SUBLANES, SSD_CONV_DIM), F32)
        s_ref[...] = jnp.zeros_like(s_ref)

    x = xbc_ref[...]
    buf_ref[SUBLANES:SUBLANES + C, :] = x
    conv = cb_ref[...] + cw_ref[SSD_CONV - 1:SSD_CONV, :] * x
    for j in range(SSD_CONV - 1):
        conv = conv + cw_ref[j:j + 1, :] * buf_ref[pl.ds(SUBLANES - (SSD_CONV - 1) + j, C), :]
    buf_ref[0:SUBLANES, :] = x[C - SUBLANES:C, :]
    xa = _silu(conv)
    xs = xa[:, 0:BRANCH]
    dt = _softplus(dt_ref[...] + dtb_ref[...])
    a = dt * (-jnp.exp(alog_ref[...]))
    row = lax.broadcasted_iota(I32, (C, C), 0)
    col = lax.broadcasted_iota(I32, (C, C), 1)
    tril = row >= col
    tri = jnp.where(tril, 1.0, 0.0)
    cum_c = _dot_f32(tri, a)
    cum_r = _dot_nt_f32(a.T, tri)
    xdt = xs * _head_lanes(dt, (C, BRANCH))
    xdt_t = xdt.T
    dsk = dsk_ref[...]
    ys = []
    for h in range(HEADS):
        g = h // (HEADS // SSD_GROUPS)
        bm = xa[:, BRANCH + DK * g:BRANCH + DK * (g + 1)]
        cm = xa[:, BRANCH + DK * (SSD_GROUPS + g):BRANCH + DK * (SSD_GROUPS + g + 1)]
        ch = cum_c[:, h:h + 1]
        decay = jnp.exp(jnp.where(tril, ch - cum_r[h:h + 1, :], NEG))
        scores = _dot_nt(cm, bm) * decay
        xdt_h = xdt[:, DK * h:DK * (h + 1)]
        s_old = s_ref[h]
        y = _dot(scores, xdt_h) + _dot_nt(cm, s_old) * jnp.exp(ch)
        last = cum_c[C - 1:C, h:h + 1]
        w = jnp.exp(last - ch)
        s_ref[h] = s_old * jnp.exp(last) + _dot(xdt_t[DK * h:DK * (h + 1), :], bm * w)
        ys.append(y + dsk[:, h:h + 1] * xs[:, DK * h:DK * (h + 1)])
    y = jnp.concatenate(ys, axis=1)
    o_ref[...] = _rms(y * _silu(z_ref[...]), sn_ref[...])

    @pl.when(c == pl.num_programs(1) - 1)
    def _():
        st_ref[0] = s_ref[...]


def _ssd_prompt(xbc, dt, z, cw, cb, dtb, alog, dsk, sn, B, T):
    assert T % CHUNK == 0
    nc = T // CHUNK
    rows = B * T
    row_spec = lambda w: pl.BlockSpec((CHUNK, w), lambda b, c: (b * nc + c, 0))
    return pl.pallas_call(
        _ssd_prompt_kernel,
        grid=(B, nc),
        in_specs=[row_spec(SSD_CONV_DIM), row_spec(LANES), row_spec(BRANCH),
                  _const_spec(cw.shape), _const_spec(cb.shape), _const_spec(dtb.shape),
                  _const_spec(alog.shape), _const_spec(dsk.shape), _const_spec(sn.shape)],
        out_specs=[row_spec(BRANCH), pl.BlockSpec((1, HEADS, DK, DK), lambda b, c: (b, 0, 0, 0))],
        out_shape=[jax.ShapeDtypeStruct((rows, BRANCH), F32),
                   jax.ShapeDtypeStruct((B, HEADS, DK, DK), F32)],
        scratch_shapes=[pltpu.VMEM((SUBLANES + CHUNK, SSD_CONV_DIM), F32),
                        pltpu.VMEM((HEADS, DK, DK), F32)],
        compiler_params=_cparams(("parallel", "arbitrary")),
        name="ssd_prompt",
    )(xbc, dt, z, cw, cb, dtb, alog, dsk, sn)


def _hgrn_gates(fr, lb):
    log_sig = jnp.minimum(fr, 0.0) - jnp.log1p(jnp.exp(-jnp.abs(fr)))
    a = jnp.log(jnp.maximum(lb, LB_FLOOR))
    b = jnp.log1p(-lb) + log_sig
    log_f = jnp.maximum(a, b) + jnp.log1p(jnp.exp(-jnp.abs(a - b)))
    return log_f, (1.0 - lb) * _sigmoid(-fr)


def _block_reference_rows(G, m):
    C = G.shape[0]
    if m >= 4:
        parts = []
        for p in range(C // (2 * m)):
            r = p * 2 * m + m - 1
            parts.append(jnp.broadcast_to(G[r:r + 1, :], (2 * m, G.shape[1])))
        return parts[0] if len(parts) == 1 else jnp.concatenate(parts, axis=0)
    t = lax.broadcasted_iota(I32, G.shape, 0)
    if m == 1:
        return jnp.where((t & 1) == 1, pltpu.roll(G, 1, 0), G)
    r = t & 3
    return jnp.where(r == 0, pltpu.roll(G, C - 1, 0),
                     jnp.where(r == 1, G, jnp.where(r == 2, pltpu.roll(G, 1, 0), pltpu.roll(G, 2, 0))))


def _hgrn_prompt_kernel(hg_ref, lb_ref, hn_ref, seg_ref, o_ref, st_ref, s_ref):
    c = pl.program_id(1)
    C = CHUNK
    W = BRANCH

    @pl.when(c == 0)
    def _():
        s_ref[...] = jnp.zeros_like(s_ref)

    q = hg_ref[:, 0:W]
    v = hg_ref[:, 2 * W:3 * W]
    log_f, kin = _hgrn_gates(hg_ref[:, W:2 * W], lb_ref[...])
    row = lax.broadcasted_iota(I32, (C, C), 0)
    col = lax.broadcasted_iota(I32, (C, C), 1)
    G = _dot_f32(jnp.where(row >= col, 1.0, 0.0), log_f)
    lane_head = lax.broadcasted_iota(I32, (C, W), 1) // DK
    t_idx = lax.broadcasted_iota(I32, (C, W), 0)

    def stack_heads(a):
        return jnp.concatenate([jnp.where(lane_head == h, a, 0.0) for h in range(HEADS)],
                               axis=0).astype(BF16)

    t4 = lax.broadcasted_iota(I32, (HEADS * C, C), 0) & (C - 1)
    s4 = lax.broadcasted_iota(I32, (HEADS * C, C), 1)
    att = jnp.where(t4 == s4, _dot_nt(stack_heads(q), kin), 0.0)
    m = C // 2
    while m >= 1:
        R = _block_reference_rows(G, m)
        upper = (t_idx & m) != 0
        A = jnp.where(upper, q * jnp.exp(jnp.minimum(G - R, 0.0)), 0.0)
        Bm = jnp.where(upper, 0.0, kin * jnp.exp(jnp.minimum(R - G, 0.0)))
        lm = int(math.log2(m))
        pair = ((t4 >> lm) ^ (s4 >> lm)) == 1
        att = att + jnp.where(pair & (t4 > s4), _dot_nt(stack_heads(A), Bm), 0.0)
        m //= 2
    y4 = _dot(att, v)
    y = jnp.zeros((C, W), F32)
    for h in range(HEADS):
        y = y + jnp.where(lane_head == h, y4[C * h:C * (h + 1)], 0.0)
    s_old = s_ref[...]
    y = y + _dot(q * jnp.exp(G), s_old)
    last = G[C - 1:C, :]
    ke_t = (kin * jnp.exp(last - G)).T
    G_t = G.T
    blk = (lax.broadcasted_iota(I32, (W, W), 0) // DK) == (lax.broadcasted_iota(I32, (W, W), 1) // DK)
    s_ref[...] = s_old * jnp.exp(G_t[:, C - 1:C]) + jnp.where(blk, _dot(ke_t, v), 0.0)
    ms = _dot_f32(y * y, seg_ref[...])
    o = y * lax.rsqrt(ms + EPS) * hn_ref[...]
    o_ref[...] = o * _silu(hg_ref[:, 3 * W:4 * W])

    @pl.when(c == pl.num_programs(1) - 1)
    def _():
        for h in range(HEADS):
            st_ref[0, h] = s_ref[DK * h:DK * (h + 1), DK * h:DK * (h + 1)]


def _hgrn_prompt(hg, lb, hn, B, T):
    nc = T // CHUNK
    rows = B * T
    seg = (np.arange(BRANCH)[:, None] // DK == np.arange(BRANCH)[None, :] // DK) / float(DK)
    seg = jnp.asarray(seg, F32)
    row_spec = lambda w: pl.BlockSpec((CHUNK, w), lambda b, c: (b * nc + c, 0))
    return pl.pallas_call(
        _hgrn_prompt_kernel,
        grid=(B, nc),
        in_specs=[row_spec(4 * BRANCH), _const_spec(lb.shape), _const_spec(hn.shape),
                  _const_spec(seg.shape)],
        out_specs=[row_spec(BRANCH), pl.BlockSpec((1, HEADS, DK, DK), lambda b, c: (b, 0, 0, 0))],
        out_shape=[jax.ShapeDtypeStruct((rows, BRANCH), F32),
                   jax.ShapeDtypeStruct((B, HEADS, DK, DK), F32)],
        scratch_shapes=[pltpu.VMEM((BRANCH, BRANCH), F32)],
        compiler_params=_cparams(("parallel", "arbitrary")),
        name="hgrn_prompt",
    )(hg, lb, hn, seg)


PAGES_PER_STEP = 32


def _mla_decode_kernel(pt_ref, q_ref, qt_ref, new_ref, *rest, n_pg):
    page_refs = rest[:n_pg]
    wv_ref, o_ref, qb_ref, m_ref, l_ref, acc_ref = rest[n_pg:]
    del pt_ref
    b = pl.program_id(0)
    j = pl.program_id(1)
    scale = (MLA_NOPE + MLA_ROPE) ** -0.5
    page = page_refs[0].shape[1]

    @pl.when(j == 0)
    def _():
        qt = qt_ref[0]
        for h in range(HEADS):
            qb_ref[h] = jnp.broadcast_to(qt[:, h:h + 1], (MLA_CACHE, LANES))
        m_ref[...] = jnp.full_like(m_ref, NEG)
        l_ref[...] = jnp.zeros_like(l_ref)
        acc_ref[...] = jnp.zeros_like(acc_ref)

    scores = [[] for _ in range(HEADS)]
    for r in page_refs:
        kt = r[...]
        for h in range(HEADS):
            scores[h].append(jnp.sum(kt * qb_ref[h], axis=0, keepdims=True))
    s = jnp.concatenate([jnp.concatenate(scores[h], axis=1) for h in range(HEADS)], axis=0) * scale
    m_old = m_ref[0:HEADS, :]
    m_new = jnp.maximum(m_old, jnp.max(s, axis=-1, keepdims=True))
    alpha = jnp.exp(m_old - m_new)
    p = jnp.exp(s - m_new[:, 0:1])
    l_ref[0:HEADS, :] = alpha * l_ref[0:HEADS, :] + jnp.sum(p, axis=-1, keepdims=True)
    m_ref[0:HEADS, :] = m_new
    for h in range(HEADS):
        acc = alpha[h:h + 1, 0:1] * acc_ref[h]
        for t, r in enumerate(page_refs):
            acc = acc + r[0:MLA_KV_LORA, :] * p[h:h + 1, t * page:(t + 1) * page]
        acc_ref[h] = acc

    @pl.when(j == pl.num_programs(1) - 1)
    def _():
        new = new_ref[0]
        s_new = jnp.sum(q_ref[0] * new, axis=-1, keepdims=True) * scale
        ones = jnp.ones((SUBLANES, LANES), F32)
        outs = []
        for h in range(HEADS):
            m_old = m_ref[h:h + 1, :]
            m_fin = jnp.maximum(m_old, s_new[h:h + 1, :])
            alpha = jnp.exp(m_old - m_fin)
            p_n = jnp.exp(s_new[h:h + 1, :] - m_fin)
            l = alpha * l_ref[h:h + 1, :] + p_n
            lat = _dot_nt_f32(ones, acc_ref[h])[0:1, :]
            outs.append((alpha * lat + p_n * new[:, 0:MLA_KV_LORA]) / l)
        o_cat = jnp.concatenate(outs, axis=1)
        o = _dot(jnp.broadcast_to(o_cat, (SUBLANES, HEADS * MLA_KV_LORA)), wv_ref[...])
        o_ref[pl.ds(b, 1), :] = o[0:1, :]


def _mla_decode(layer, page_table, q8, qt, new_rows, cache_t, wv):
    BS, npages = page_table.shape
    page = cache_t.shape[3]
    n_pg = min(PAGES_PER_STEP, npages)
    assert npages % n_pg == 0 and page == LANES
    steps = npages // n_pg

    def page_spec(i):
        return pl.BlockSpec((None, None, MLA_CACHE, page),
                            lambda b, j, pt: (layer, pt[b, j * n_pg + i], 0, 0))

    grid_spec = pltpu.PrefetchScalarGridSpec(
        num_scalar_prefetch=1,
        grid=(BS, steps),
        in_specs=[pl.BlockSpec((1, SUBLANES, MLA_CACHE), lambda b, j, pt: (b, 0, 0)),
                  pl.BlockSpec((1, MLA_CACHE, HEADS), lambda b, j, pt: (b, 0, 0)),
                  pl.BlockSpec((1, 1, MLA_CACHE), lambda b, j, pt: (b, 0, 0))]
                 + [page_spec(i) for i in range(n_pg)]
                 + [pl.BlockSpec(wv.shape, lambda b, j, pt: (0, 0))],
        out_specs=pl.BlockSpec((BS, BRANCH), lambda b, j, pt: (0, 0)),
        scratch_shapes=[pltpu.VMEM((HEADS, MLA_CACHE, LANES), F32),
                        pltpu.VMEM((SUBLANES, LANES), F32), pltpu.VMEM((SUBLANES, LANES), F32),
                        pltpu.VMEM((HEADS, MLA_KV_LORA, LANES), F32)],
    )
    return pl.pallas_call(
        functools.partial(_mla_decode_kernel, n_pg=n_pg),
        grid_spec=grid_spec,
        out_shape=jax.ShapeDtypeStruct((BS, BRANCH), F32),
        compiler_params=_cparams(("arbitrary", "arbitrary")),
        name="mla_decode",
    )(page_table, q8, qt, new_rows, *([cache_t] * n_pg), wv)


def _nsa_cmp_decode_kernel(pt_ref, qt_ref, *rest, n_pg, past):
    page_refs = rest[:n_pg]
    oc_ref, imp_ref, qb_ref, ts_ref, vb_ref = rest[n_pg:]
    del pt_ref
    j = pl.program_id(1)
    steps = ts_ref.shape[0]
    page = page_refs[0].shape[1]
    nblk = 2 * LANES

    @pl.when(j == 0)
    def _():
        qt = qt_ref[0]
        for h in range(HEADS):
            qb_ref[h] = jnp.broadcast_to(qt[:, h:h + 1], (DK, LANES))

    for t, r in enumerate(page_refs):
        kt = r[0:DK, :]
        for h in range(HEADS):
            ts_ref[j, h:h + 1, t * page:(t + 1) * page] = jnp.sum(kt * qb_ref[h], axis=0, keepdims=True)
        vb_ref[j, :, t * page:(t + 1) * page] = r[DK:2 * DK, :]

    @pl.when(j == steps - 1)
    def _():
        blk = lax.broadcasted_iota(I32, (HEADS, nblk), 1)
        first = lax.broadcasted_iota(I32, (HEADS, page), 1) < NSA_BLOCK
        s = jnp.zeros((HEADS, nblk), F32)
        for jj in range(steps):
            ts = ts_ref[jj, 0:HEADS, :]
            for t in range(n_pg):
                x = ts[:, t * page:(t + 1) * page]
                b0 = 2 * (jj * n_pg + t)
                s0 = jnp.sum(jnp.where(first, x, 0.0), axis=1, keepdims=True)
                s1 = jnp.sum(jnp.where(first, 0.0, x), axis=1, keepdims=True)
                s = jnp.where(blk == b0, s0, jnp.where(blk == b0 + 1, s1, s))
        s = s * (1.0 / NSA_BLOCK)
        cmask = (blk + 1) * NSA_BLOCK - 1 <= past
        p = _masked_softmax(s, cmask)
        cur = past // NSA_BLOCK
        imp = jnp.sum(p, axis=0, keepdims=True)
        blk1 = blk[0:1]
        forced = (blk1 == 0) | (blk1 == cur - 1)
        imp = imp + jnp.where(forced, NSA_FORCED_BONUS, 0.0)
        imp_ref[0] = jnp.where(blk1 >= cur, -3.0, imp)

        pw = jnp.concatenate([p * (1.0 / NSA_BLOCK), jnp.zeros((SUBLANES - HEADS, nblk), F32)], axis=0)
        nb_step = 2 * n_pg
        expand = jnp.where(lax.broadcasted_iota(I32, (nb_step, n_pg * page), 0)
                           == lax.broadcasted_iota(I32, (nb_step, n_pg * page), 1) // NSA_BLOCK, 1.0, 0.0)
        accs = [jnp.zeros((DK, page), F32) for _ in range(HEADS)]
        for jj in range(steps):
            w = _dot_f32(pw[:, nb_step * jj:nb_step * (jj + 1)], expand)
            for t in range(n_pg):
                v = vb_ref[jj, :, t * page:(t + 1) * page]
                for h in range(HEADS):
                    accs[h] = accs[h] + v * w[h:h + 1, t * page:(t + 1) * page]
        ones = jnp.ones((SUBLANES, LANES), F32)
        oc_ref[0] = jnp.concatenate([_dot_nt_f32(ones, a)[0:1, :] for a in accs], axis=0)


def _nsa_cmp_decode(layer, page_table, qt, cache_t, past):
    BS, npages = page_table.shape
    page = cache_t.shape[3]
    n_pg = min(PAGES_PER_STEP, npages)
    assert page == 2 * NSA_BLOCK and page == LANES and 2 * npages <= 2 * LANES and npages % n_pg == 0
    steps = npages // n_pg

    def page_spec(i):
        return pl.BlockSpec((None, None, 2 * DK, page),
                            lambda b, j, pt: (layer, pt[b, j * n_pg + i], 0, 0))

    grid_spec = pltpu.PrefetchScalarGridSpec(
        num_scalar_prefetch=1,
        grid=(BS, steps),
        in_specs=[pl.BlockSpec((1, DK, HEADS), lambda b, j, pt: (b, 0, 0))]
                 + [page_spec(i) for i in range(n_pg)],
        out_specs=[pl.BlockSpec((1, HEADS, DK), lambda b, j, pt: (b, 0, 0)),
                   pl.BlockSpec((1, 1, 2 * LANES), lambda b, j, pt: (b, 0, 0))],
        scratch_shapes=[pltpu.VMEM((HEADS, DK, LANES), F32),
                        pltpu.VMEM((steps, SUBLANES, n_pg * page), F32),
                        pltpu.VMEM((steps, DK, n_pg * page), F32)],
    )
    return pl.pallas_call(
        functools.partial(_nsa_cmp_decode_kernel, n_pg=n_pg, past=past),
        grid_spec=grid_spec,
        out_shape=[jax.ShapeDtypeStruct((BS, HEADS, DK), F32),
                   jax.ShapeDtypeStruct((BS, 1, 2 * LANES), F32)],
        compiler_params=_cparams(("parallel", "arbitrary")),
        name="nsa_cmp_decode",
    )(page_table, qt, *([cache_t] * n_pg))


def _topk_kernel(imp_ref, idx_ref, *, past, k_top):
    BS = imp_ref.shape[0]
    cur = past // NSA_BLOCK
    blk = lax.broadcasted_iota(I32, (BS, 2 * LANES), 1).astype(F32)
    lane_x = lax.broadcasted_iota(I32, (BS, LANES), 1)
    imp = jnp.concatenate([imp_ref[...], jnp.where(lane_x == 0, NSA_FORCED_BONUS, -3.0)], axis=1)
    blk = jnp.concatenate([blk, (cur + lane_x).astype(F32)], axis=1)
    out = jnp.zeros((BS, LANES), F32)
    for k in range(k_top):
        m = jnp.max(imp, axis=-1, keepdims=True)
        pick = jnp.min(jnp.where(imp == m, blk, 1e9), axis=-1, keepdims=True)
        imp = jnp.where(blk == pick, -4.0, imp)
        out = jnp.where(lane_x == k, pick, out)
    idx_ref[...] = out.astype(I32)


def _topk(imp, past, k_top):
    BS = imp.shape[0]
    return pl.pallas_call(
        functools.partial(_topk_kernel, past=past, k_top=k_top),
        out_shape=jax.ShapeDtypeStruct((BS, LANES), I32),
        name="nsa_topk",
    )(imp)


def _nsa_sel_decode_kernel(pt_ref, idx_ref, q_ref, pgn_ref, wn_ref, win_ref, g_ref, oc_ref, *rest,
                           k_top, n_blocks):
    blk_refs = rest[:k_top]
    (o_ref,) = rest[k_top:]
    del pt_ref
    b = pl.program_id(0)
    q = q_ref[0]
    page = blk_refs[0].shape[1]

    def attend(kt, vt, s_mask, k_new, v_new, new_on):
        s = _dot(q, kt)
        if s_mask is not None:
            s = jnp.where(s_mask, s, NEG)
        s_n = jnp.where(new_on, jnp.sum(q * k_new, axis=-1, keepdims=True), NEG)
        m = jnp.maximum(jnp.max(s, axis=-1, keepdims=True), s_n)
        e = jnp.exp(s - m)
        if s_mask is not None:
            e = jnp.where(s_mask, e, 0.0)
        e_n = jnp.where(new_on, jnp.exp(s_n - m), 0.0)
        l = jnp.sum(e, axis=-1, keepdims=True) + e_n
        return (_dot_nt(e, vt) + e_n * v_new) / l

    kt = jnp.concatenate([r[0:DK, :] for r in blk_refs], axis=1).astype(BF16)
    vt = jnp.concatenate([r[DK:2 * DK, :] for r in blk_refs], axis=1).astype(BF16)
    lane = lax.broadcasted_iota(I32, (1, k_top * page), 1)
    slot = lane // page
    half = (lane % page) // NSA_BLOCK
    valid = jnp.zeros((1, k_top * page), I32)
    has_new = jnp.zeros((), jnp.bool_)
    for k in range(k_top):
        ik = idx_ref[b, k]
        ok = jnp.where(half == (ik & 1), (ik < n_blocks).astype(I32), 0)
        valid = jnp.where(slot == k, ok, valid)
        has_new = has_new | (ik == n_blocks)
    pgn = pgn_ref[0]
    o_s = attend(kt, vt, valid > 0, pgn[:, 2 * DK:3 * DK], pgn[:, 3 * DK:4 * DK], has_new)
    wn = wn_ref[0]
    o_w = attend(win_ref[0:DK, :].astype(BF16), win_ref[DK:2 * DK, :].astype(BF16), None,
                 wn[:, 0:DK], wn[:, DK:2 * DK], True)
    o_c = oc_ref[0]
    g = g_ref[0]
    outs = []
    for h in range(HEADS):
        outs.append(g[:, 3 * h:3 * h + 1] * o_c[h:h + 1] + g[:, 3 * h + 1:3 * h + 2] * o_s[h:h + 1]
                    + g[:, 3 * h + 2:3 * h + 3] * o_w[h:h + 1])
    o_ref[pl.ds(b, 1), :] = jnp.concatenate(outs, axis=1)


def _nsa_sel_decode(layer, page_table, idx, q8, pg_new, w_new, win_t, g_new, o_c, cache_t, k_top):
    BS, npages = page_table.shape
    n_blocks = 2 * npages
    page = cache_t.shape[3]
    nwin = win_t.shape[3]
    blk = jnp.minimum(idx[:, :k_top], n_blocks - 1)
    phys = jnp.take_along_axis(page_table, blk // 2, axis=1)

    def blk_spec(k):
        return pl.BlockSpec((None, None, 2 * DK, page), lambda b, ph, ix: (layer, ph[b, k], 1, 0))

    row3 = lambda w: pl.BlockSpec((1, 1, w), lambda b, pt, ix: (b, 0, 0))
    grid_spec = pltpu.PrefetchScalarGridSpec(
        num_scalar_prefetch=2,
        grid=(BS,),
        in_specs=[pl.BlockSpec((1, SUBLANES, DK), lambda b, pt, ix: (b, 0, 0)),
                  row3(256), row3(LANES),
                  pl.BlockSpec((None, None, 2 * DK, nwin), lambda b, pt, ix: (layer, b, 0, 0)),
                  row3(LANES),
                  pl.BlockSpec((1, HEADS, DK), lambda b, pt, ix: (b, 0, 0))]
                 + [blk_spec(k) for k in range(k_top)],
        out_specs=pl.BlockSpec((BS, BRANCH), lambda b, pt, ix: (0, 0)),
    )
    return pl.pallas_call(
        functools.partial(_nsa_sel_decode_kernel, k_top=k_top, n_blocks=n_blocks),
        grid_spec=grid_spec,
        out_shape=jax.ShapeDtypeStruct((BS, BRANCH), F32),
        compiler_params=_cparams(("arbitrary",)),
        name="nsa_sel_decode",
    )(phys, idx, q8, pg_new, w_new, win_t, g_new, o_c, *([cache_t] * k_top))


def _column(row, eye):
    return jnp.sum(jnp.where(eye, jnp.broadcast_to(row, eye.shape), 0.0), axis=1, keepdims=True)


def _rec_decode_kernel(xbc_ref, cbuf_ref, dt_ref, z_ref, hg_ref, s_ref, hs_ref,
                       cw_ref, cb_ref, dtb_ref, alog_ref, dsk_ref, sn_ref, lb_ref, hn_ref,
                       os_ref, oh_ref, sn_out, hs_out):
    b = pl.program_id(0)
    eye = lax.broadcasted_iota(I32, (DK, DK), 0) == lax.broadcasted_iota(I32, (DK, DK), 1)

    cbuf = cbuf_ref[...]
    xn = xbc_ref[0]
    conv = cb_ref[...] + cw_ref[SSD_CONV - 1:SSD_CONV, :] * xn
    for j in range(SSD_CONV - 1):
        conv = conv + cw_ref[j:j + 1, :] * cbuf[j:j + 1, :]
    xa = _silu(conv)
    dt = _softplus(dt_ref[0] + dtb_ref[...])
    ea = jnp.exp(dt * (-jnp.exp(alog_ref[...])))
    dsk = dsk_ref[...]
    ys = []
    for h in range(HEADS):
        g = h // (HEADS // SSD_GROUPS)
        xs = xa[:, DK * h:DK * (h + 1)]
        bm = xa[:, BRANCH + DK * g:BRANCH + DK * (g + 1)]
        cm = xa[:, BRANCH + DK * (SSD_GROUPS + g):BRANCH + DK * (SSD_GROUPS + g + 1)]
        xdt_col = _column(xs * dt[:, h:h + 1], eye)
        s_old = s_ref[h]
        ea_h = ea[:, h:h + 1]
        y_col = (jnp.sum(cm * bm, axis=-1, keepdims=True) * xdt_col
                 + jnp.sum(s_old * cm, axis=-1, keepdims=True) * ea_h)
        sn_out[0, h] = s_old * ea_h + xdt_col * bm
        y_row = jnp.sum(jnp.where(eye, jnp.broadcast_to(y_col, (DK, DK)), 0.0), axis=0, keepdims=True)
        ys.append(y_row + dsk[:, h:h + 1] * xs)
    y = jnp.concatenate(ys, axis=1)
    os_ref[pl.ds(b, 1), :] = _rms(y * _silu(z_ref[0]), sn_ref[...])

    W = BRANCH
    hg = hg_ref[0]
    q = hg[:, 0:W]
    v = hg[:, 2 * W:3 * W]
    log_f, kin = _hgrn_gates(hg[:, W:2 * W], lb_ref[...])
    ef = jnp.exp(log_f)
    hn = hn_ref[...]
    outs = []
    for h in range(HEADS):
        r = slice(DK * h, DK * (h + 1))
        s_old = hs_ref[h]
        att = jnp.sum(q[:, r] * kin[:, r], axis=-1, keepdims=True)
        y = att * v[:, r] + jnp.sum(s_old * _column(q[:, r] * ef[:, r], eye), axis=0, keepdims=True)
        hs_out[0, h] = s_old * _column(ef[:, r], eye) + _column(kin[:, r], eye) * v[:, r]
        outs.append(_rms(y, hn[:, r]))
    o = jnp.concatenate(outs, axis=1)
    oh_ref[pl.ds(b, 1), :] = o * _silu(hg[:, 3 * W:4 * W])


def _rec_decode(layer, xbc_new, conv_state, dt_new, z_new, hg_new, state_ssd, state_hgrn,
                cw, cb, dtb, alog, dsk, sn, lb, hn):
    BS = xbc_new.shape[0]
    row3 = lambda w: pl.BlockSpec((1, 1, w), lambda b: (b, 0, 0))
    st_in = pl.BlockSpec((None, None, HEADS, DK, DK), lambda b: (layer, b, 0, 0, 0))
    st_out = pl.BlockSpec((1, HEADS, DK, DK), lambda b: (b, 0, 0, 0))
    o_spec = pl.BlockSpec((BS, BRANCH), lambda b: (0, 0))
    consts = (cw, cb, dtb, alog, dsk, sn, lb, hn)
    return pl.pallas_call(
        _rec_decode_kernel,
        grid=(BS,),
        in_specs=[row3(SSD_CONV_DIM),
                  pl.BlockSpec((None, None, SSD_CONV - 1, SSD_CONV_DIM), lambda b: (layer, b, 0, 0)),
                  row3(LANES), row3(BRANCH), row3(4 * BRANCH), st_in, st_in]
                 + [_const_spec(c.shape) for c in consts],
        out_specs=[o_spec, o_spec, st_out, st_out],
        out_shape=[jax.ShapeDtypeStruct((BS, BRANCH), F32), jax.ShapeDtypeStruct((BS, BRANCH), F32),
                   jax.ShapeDtypeStruct((BS, HEADS, DK, DK), F32),
                   jax.ShapeDtypeStruct((BS, HEADS, DK, DK), F32)],
        compiler_params=_cparams(("arbitrary",)),
        name="rec_decode",
    )(xbc_new, conv_state, dt_new, z_new, hg_new, state_ssd, state_hgrn, *consts)


def _rot_cols(w):
    half = MLA_ROPE // 2
    return jnp.concatenate([-w[..., half:], w[..., :half]], axis=-1)


def _pad_cols(w, width):
    return jnp.pad(w, [(0, 0)] * (w.ndim - 1) + [(0, width - w.shape[-1])])


def _pack_w_in(w):
    sizes = (256, 384, 12, MLA_Q_LORA, MLA_KV_LORA, MLA_ROPE, 256, SSD_CONV_DIM, HEADS, 256, 256, 256, 256)
    offs = np.cumsum((0,) + sizes)
    (nsa_q, nsa_kv, nsa_g, cq, ckv, kpe, z, xbc, dt, hq, hf, hi, hgate) = (
        w[:, offs[i]:offs[i + 1]] for i in range(len(sizes)))
    gate = w[:, offs[-1]:]
    packed = jnp.concatenate([
        nsa_q, nsa_kv[:, :256], nsa_kv[:, 256:], _pad_cols(nsa_g, LANES), cq, ckv,
        _pad_cols(kpe, LANES), _pad_cols(_rot_cols(kpe), LANES), z, xbc, _pad_cols(dt, LANES),
        hq, hf, hi, hgate], axis=1)
    assert packed.shape[1] == PACKED_WIDTH
    return packed.astype(BF16), gate.astype(BF16)


def _pack_mla(wuq, wukv):
    nope = wuq[:, :, :MLA_NOPE].reshape(MLA_Q_LORA, HEADS * MLA_NOPE)
    pe = wuq[:, :, MLA_NOPE:]
    pe_p = _pad_cols(pe, LANES).reshape(MLA_Q_LORA, HEADS * LANES)
    pr_p = _pad_cols(_rot_cols(pe), LANES).reshape(MLA_Q_LORA, HEADS * LANES)
    wuq_p = jnp.concatenate([nope, pe_p, pr_p], axis=1).astype(BF16)
    eye = jnp.eye(HEADS, dtype=wukv.dtype)
    wukn = jnp.einsum('rhn,hg->hngr', wukv[:, :, :MLA_NOPE], eye).reshape(HEADS * MLA_NOPE, HEADS * MLA_KV_LORA)
    wv = jnp.einsum('rhv,hg->hrgv', wukv[:, :, MLA_NOPE:], eye).reshape(HEADS * MLA_KV_LORA, HEADS * MLA_V)
    return wuq_p, wukn.astype(BF16), wv.astype(BF16)


def _rope_table(positions):
    half = MLA_ROPE // 2
    freq = ROPE_THETA ** (-jnp.arange(half, dtype=F32) / half)
    ang = positions.astype(F32)[:, None] * freq
    cos = jnp.concatenate([jnp.cos(ang), jnp.cos(ang)], axis=1)
    sin = jnp.concatenate([jnp.sin(ang), jnp.sin(ang)], axis=1)
    return jnp.concatenate([_pad_cols(cos, LANES), _pad_cols(sin, LANES)], axis=1)


def _pad_row(v, width=LANES):
    return _pad_cols(v.reshape(1, -1).astype(F32), width)


def kernel(x_prompt, x_sample, cache_nsa_kv, cache_mla, cache_nsa_win, state_ssd, state_ssd_conv, state_hgrn, page_table, norm_g, ffn_w_in, ffn_w_out, w_in, mla_q_norm, mla_kv_norm, mla_w_uq, mla_w_ukv, ssd_conv_w, ssd_conv_b, ssd_dt_bias, ssd_a_log, ssd_d, ssd_norm, hg_lb_logits, hg_norm, w_branch, w_out):
    BP, T, D = x_prompt.shape
    BS = x_sample.shape[0]
    depth = w_in.shape[0]
    n_pool, page = cache_nsa_kv.shape[1], cache_nsa_kv.shape[2]
    npages = page_table.shape[1]
    past = npages * page
    n_win = cache_nsa_win.shape[2]
    rows_p = BP * T
    assert D == D_MODEL and x_sample.shape[1] == 1 and BS % SUBLANES == 0
    assert n_win == min(NSA_WINDOW, past) and past % NSA_BLOCK == 0
    k_top = min(NSA_TOPK, past // NSA_BLOCK + 1)

    xp = x_prompt.reshape(rows_p, D)
    xs = x_sample.reshape(BS, D)
    cs_p = _rope_table(jnp.tile(jnp.arange(T, dtype=I32), BP))
    cs_s = _rope_table(jnp.full((BS,), past, I32))
    lbp = jax.nn.softmax(hg_lb_logits.astype(F32), axis=0)
    lower_bounds = jnp.cumsum(lbp, axis=0) - lbp[0]

    nsa_t = jnp.transpose(cache_nsa_kv, (0, 1, 3, 4, 2)).reshape(depth, n_pool, 4 * DK, page)
    mla_t = jnp.transpose(cache_mla, (0, 1, 3, 2))
    win_t = jnp.transpose(cache_nsa_win, (0, 1, 3, 4, 2)).reshape(depth, BS, 2 * DK, n_win)

    def row3(a):
        return a.reshape(BS, 1, a.shape[1])

    outs = [[] for _ in range(12)]
    for l in range(depth):
        g = norm_g[l].astype(F32)
        grow = lambda i: g[i].reshape(1, D)
        wi = ffn_w_in[l].astype(BF16)
        wo = ffn_w_out[l].astype(BF16)
        w_pack, w_gate = _pack_w_in(w_in[l])
        wuq_p, wukn, wv = _pack_mla(mla_w_uq[l], mla_w_ukv[l])
        mqn = mla_q_norm[l].reshape(1, -1)
        mkvn = mla_kv_norm[l].reshape(1, -1)

        xp = _ffn(xp, grow(0), grow(1), wi[0], wo[0])
        xs = _ffn(xs, grow(0), grow(1), wi[0], wo[0])
        (nsa_q, nsa_pg, nsa_w, nsa_g, qlat, qpe, mla_new, ssd_z, ssd_xbc, ssd_dt, hg) = _inproj(
            xp, grow(2), cs_p, w_pack, mqn, mkvn, wuq_p, wukn)
        (nsa_q_s, nsa_pg_s, nsa_w_s, nsa_g_s, qlat_s, qpe_s, mla_new_s, ssd_z_s, ssd_xbc_s, ssd_dt_s,
         hg_s) = _inproj(xs, grow(2), cs_s, w_pack, mqn, mkvn, wuq_p, wukn)

        o_nsa = _nsa_prompt(nsa_q, nsa_pg, nsa_w, nsa_g, BP, T)
        pad8 = lambda a: jnp.pad(a, ((0, 0), (0, SUBLANES - HEADS), (0, 0)))
        q4 = nsa_q_s.reshape(BS, HEADS, DK)
        q8 = pad8(q4)
        o_c, imp = _nsa_cmp_decode(l, page_table, jnp.transpose(q4, (0, 2, 1)), nsa_t, past)
        idx = _topk(imp.reshape(BS, 2 * LANES), past, k_top)
        o_nsa_s = _nsa_sel_decode(l, page_table, idx, q8, row3(nsa_pg_s), row3(nsa_w_s), win_t,
                                  row3(nsa_g_s), o_c, nsa_t, k_top)

        o_mla = _mla_prompt(qlat, qpe, mla_new, wv, BP, T)
        q_cat = jnp.concatenate([qlat_s.reshape(BS, HEADS, LANES),
                                 qpe_s.reshape(BS, HEADS, LANES)[:, :, :MLA_ROPE]], axis=-1)
        o_mla_s = _mla_decode(l, page_table, pad8(q_cat), jnp.transpose(q_cat, (0, 2, 1)),
                              row3(mla_new_s), mla_t, wv)

        cw = ssd_conv_w[l].astype(F32)
        cb = ssd_conv_b[l].reshape(1, -1).astype(F32)
        dtb = _pad_row(ssd_dt_bias[l])
        alog = _pad_row(ssd_a_log[l])
        dsk = _pad_row(ssd_d[l])
        sn = ssd_norm[l].reshape(1, -1).astype(F32)
        lb = lower_bounds[l].reshape(1, -1)
        hn = hg_norm[l].reshape(1, -1).astype(F32)
        o_ssd, ssd_p = _ssd_prompt(ssd_xbc, ssd_dt, ssd_z, cw, cb, dtb, alog, dsk, sn, BP, T)
        o_hg, hgrn_p = _hgrn_prompt(hg, lb, hn, BP, T)
        o_ssd_s, o_hg_s, ssd_s, hgrn_s = _rec_decode(
            l, row3(ssd_xbc_s), state_ssd_conv, row3(ssd_dt_s), row3(ssd_z_s), row3(hg_s), state_ssd,
            state_hgrn, cw, cb, dtb, alog, dsk, sn, lb, hn)

        wbr = w_branch[l].astype(BF16)
        wout = w_out[l].astype(BF16)
        xp = _merge(xp, (o_nsa, o_mla, o_ssd, o_hg), grow(2), grow(3), w_gate, wbr, wout)
        xs = _merge(xs, (o_nsa_s, o_mla_s, o_ssd_s, o_hg_s), grow(2), grow(3), w_gate, wbr, wout)
        xp = _ffn(xp, grow(4), grow(5), wi[1], wo[1])
        xs = _ffn(xs, grow(4), grow(5), wi[1], wo[1])

        n_keep = min(NSA_WINDOW, T)
        layer_out = (
            nsa_pg.reshape(BP, T, 4, DK), nsa_pg_s.reshape(BS, 1, 4, DK),
            mla_new.reshape(BP, T, MLA_CACHE), mla_new_s.reshape(BS, 1, MLA_CACHE),
            nsa_w.reshape(BP, T, 2, DK)[:, T - n_keep:],
            jnp.concatenate([cache_nsa_win[l], nsa_w_s.reshape(BS, 1, 2, DK)], axis=1)[:, 1:],
            ssd_p, ssd_s,
            ssd_xbc.reshape(BP, T, SSD_CONV_DIM)[:, T - (SSD_CONV - 1):],
            jnp.concatenate([state_ssd_conv[l], ssd_xbc_s.reshape(BS, 1, SSD_CONV_DIM)], axis=1)[:, 1:],
            hgrn_p, hgrn_s)
        for acc, o in zip(outs, layer_out):
            acc.append(o)

    stacked = tuple(jnp.stack(o) for o in outs)
    return (xp.reshape(BP, T, D), xs.reshape(BS, 1, D)) + stacked
```

```python
import functools
import math

import numpy as np
import jax
import jax.numpy as jnp
from jax import lax
from jax.experimental import pallas as pl
from jax.experimental.pallas import tpu as pltpu

F32 = jnp.float32
BF16 = jnp.bfloat16
I32 = jnp.int32
HIGHEST = lax.Precision.HIGHEST

EPS = 1e-6
NEG = -1e30
LB_FLOOR = 1e-20
ROPE_THETA = 10000.0

D_MODEL = 1024
D_FF = 2816
HEADS = 4
DK = 64
BRANCH = HEADS * DK
NSA_BLOCK = 64
NSA_TOPK = 16
NSA_WINDOW = 512
NSA_FORCED_BONUS = float(HEADS + 1)
MLA_Q_LORA = 256
MLA_KV_LORA = 128
MLA_NOPE = 64
MLA_ROPE = 32
MLA_V = 64
MLA_CACHE = MLA_KV_LORA + MLA_ROPE
SSD_GROUPS = 2
SSD_CONV = 4
SSD_CONV_DIM = BRANCH + 2 * SSD_GROUPS * DK
CHUNK = 128
QB = 128

LANES = 128
SUBLANES = 8
VMEM_LIMIT_BYTES = 56 * 1024 * 1024

_SEG = {}
_off = 0
for _name, _w in (("nsa_q", 256), ("nsa_pg", 256), ("nsa_w", 128), ("nsa_g", 128),
                  ("mla_cq", 256), ("mla_ckv", 128), ("mla_kpe", 128), ("mla_kpr", 128),
                  ("ssd_dt", 128), ("ssd_z", 256), ("ssd_xbc", 512), ("hg", 1024)):
    _SEG[_name] = (_off, _off + _w)
    _off += _w
PACKED_WIDTH = _off


def _cparams(sem):
    return pltpu.CompilerParams(dimension_semantics=sem, vmem_limit_bytes=VMEM_LIMIT_BYTES)


def _rms(x, g):
    return x * lax.rsqrt(jnp.mean(x * x, axis=-1, keepdims=True) + EPS) * g


def _dot(a, b):
    return jnp.dot(a.astype(BF16), b.astype(BF16), preferred_element_type=F32)


def _dot_nt(a, b):
    return lax.dot_general(a.astype(BF16), b.astype(BF16), (((1,), (1,)), ((), ())),
                           preferred_element_type=F32)


def _dot_rows_split(a, b):
    half = a.shape[0] // 2
    return jnp.concatenate([_dot(a[:half], b), _dot(a[half:], b)], axis=0)


def _dot_nt_rows_split(a, b):
    half = a.shape[0] // 2
    return jnp.concatenate([_dot_nt(a[:half], b), _dot_nt(a[half:], b)], axis=0)


def _dot_f32(a, b):
    return jnp.dot(a, b, precision=HIGHEST, preferred_element_type=F32)


def _dot_nt_f32(a, b):
    return lax.dot_general(a, b, (((1,), (1,)), ((), ())), precision=HIGHEST,
                           preferred_element_type=F32)


def _sigmoid(x):
    return 1.0 / (1.0 + jnp.exp(-x))


def _silu(x):
    return x * _sigmoid(x)


def _softplus(x):
    return jnp.maximum(x, 0.0) + jnp.log1p(jnp.exp(-jnp.abs(x)))


def _masked_softmax(s, mask):
    s = jnp.where(mask, s, NEG)
    m = jnp.max(s, axis=-1, keepdims=True)
    e = jnp.exp(s - m)
    r = 1.0 / jnp.sum(e, axis=-1, keepdims=True)
    return jnp.where(mask, e * r, 0.0)


def _unnormalised_softmax(s, mask):
    s = jnp.where(mask, s, NEG)
    e = jnp.exp(s - jnp.max(s, axis=-1, keepdims=True))
    return e, 1.0 / jnp.sum(e, axis=-1, keepdims=True)


def _row_tile(rows, cap=864):
    best = SUBLANES
    for t in range(SUBLANES, min(rows, cap) + 1, SUBLANES):
        if rows % t == 0:
            best = t
    assert rows % best == 0
    return best


def _const_spec(shape):
    nd = len(shape)
    return pl.BlockSpec(shape, lambda *_: (0,) * nd)


FFN_CHUNKS = 2


def _ffn_kernel(x_ref, ga_ref, gb_ref, wi_ref, wo_ref, o_ref):
    x = x_ref[...]
    h = _rms(x, ga_ref[...]).astype(BF16)
    fc = D_FF // FFN_CHUNKS
    acc = None
    for c in range(FFN_CHUNKS):
        gate = jnp.dot(h, wi_ref[:, fc * c:fc * (c + 1)], preferred_element_type=F32)
        up = jnp.dot(h, wi_ref[:, D_FF + fc * c:D_FF + fc * (c + 1)], preferred_element_type=F32)
        part = _dot(_silu(gate) * up, wo_ref[fc * c:fc * (c + 1), :])
        acc = part if acc is None else acc + part
    o_ref[...] = x + 0.5 * _rms(acc, gb_ref[...])


def _resident_spec(shape):
    nd = len(shape)
    return pl.BlockSpec(shape, lambda *_: (0,) * nd, pipeline_mode=pl.Buffered(1))


def _ffn(x, ga, gb, wi, wo):
    rows = x.shape[0]
    tm = _row_tile(rows, cap=512)
    assert (D_FF // FFN_CHUNKS) % LANES == 0
    return pl.pallas_call(
        _ffn_kernel,
        grid=(rows // tm,),
        in_specs=[
            pl.BlockSpec((tm, D_MODEL), lambda i: (i, 0)),
            _const_spec((1, D_MODEL)),
            _const_spec((1, D_MODEL)),
            _resident_spec(wi.shape),
            _resident_spec(wo.shape),
        ],
        out_specs=pl.BlockSpec((tm, D_MODEL), lambda i: (i, 0)),
        out_shape=jax.ShapeDtypeStruct((rows, D_MODEL), F32),
        compiler_params=_cparams(("parallel",)),
        name="ffn",
    )(x, ga, gb, wi, wo)


def _inproj_kernel(x_ref, g_ref, cs_ref, w_ref, mqn_ref, mkvn_ref, wuq_ref, wukn_ref,
                   q_o, pg_o, w_o, g_o, qlat_o, qpe_o, mla_o, z_o, xbc_o, dt_o, hg_o):
    hn = _rms(x_ref[...], g_ref[...]).astype(BF16)

    def seg(first, last=None):
        a, b = _SEG[first][0], _SEG[last or first][1]
        return jnp.dot(hn, w_ref[:, a:b], preferred_element_type=F32)

    q_o[...] = seg("nsa_q") * (DK ** -0.5)
    pg_o[...] = seg("nsa_pg")
    w_g = seg("nsa_w", "nsa_g")
    w_o[...] = w_g[:, 0:LANES]
    g_o[...] = _sigmoid(w_g[:, LANES:2 * LANES])
    z_o[...] = seg("ssd_z")
    xbc_o[...] = seg("ssd_xbc")
    hg_o[...] = seg("hg")
    ckv_kpe = seg("mla_ckv", "mla_kpe")
    kpr_dt = seg("mla_kpr", "ssd_dt")
    dt_o[...] = kpr_dt[:, LANES:2 * LANES]

    cos_p = cs_ref[:, 0:LANES]
    sin_p = cs_ref[:, LANES:2 * LANES]
    cqn = _rms(seg("mla_cq"), mqn_ref[...])
    qf = _dot(cqn, wuq_ref[...])
    nope_w = HEADS * MLA_NOPE
    qlat_o[...] = _dot(qf[:, 0:nope_w], wukn_ref[...])
    for h in range(HEADS):
        pe = qf[:, nope_w + LANES * h: nope_w + LANES * (h + 1)]
        pr = qf[:, nope_w + LANES * (HEADS + h): nope_w + LANES * (HEADS + h + 1)]
        qpe_o[:, LANES * h:LANES * (h + 1)] = pe * cos_p + pr * sin_p
    mla_o[:, 0:MLA_KV_LORA] = _rms(ckv_kpe[:, 0:LANES], mkvn_ref[...])
    kpe = ckv_kpe[:, LANES:2 * LANES] * cos_p + kpr_dt[:, 0:LANES] * sin_p
    mla_o[:, MLA_KV_LORA:MLA_CACHE] = kpe[:, 0:MLA_ROPE]


_INPROJ_OUT_WIDTHS = (256, 256, 128, 128, 512, 512, MLA_CACHE, 256, 512, 128, 1024)


def _inproj(x, g, cs, w, mqn, mkvn, wuq, wukn):
    rows = x.shape[0]
    tm = _row_tile(rows)
    row_spec = lambda c: pl.BlockSpec((tm, c), lambda i: (i, 0))
    return pl.pallas_call(
        _inproj_kernel,
        grid=(rows // tm,),
        in_specs=[row_spec(D_MODEL), _const_spec((1, D_MODEL)), row_spec(2 * LANES),
                  _const_spec(w.shape), _const_spec(mqn.shape), _const_spec(mkvn.shape),
                  _const_spec(wuq.shape), _const_spec(wukn.shape)],
        out_specs=[row_spec(c) for c in _INPROJ_OUT_WIDTHS],
        out_shape=[jax.ShapeDtypeStruct((rows, c), F32) for c in _INPROJ_OUT_WIDTHS],
        compiler_params=_cparams(("parallel",)),
        name="inproj",
    )(x, g, cs, w, mqn, mkvn, wuq, wukn)


def _merge_kernel(x_ref, b0_ref, b1_ref, b2_ref, b3_ref, g2_ref, g3_ref, wg_ref, wbr_ref, wout_ref,
                  o_ref):
    x = x_ref[...]
    hn = _rms(x, g2_ref[...]).astype(BF16)
    acc = None
    for k, b_ref in enumerate((b0_ref, b1_ref, b2_ref, b3_ref)):
        gate = _sigmoid(jnp.dot(hn, wg_ref[:, D_MODEL * k:D_MODEL * (k + 1)],
                                preferred_element_type=F32))
        u = _dot(b_ref[...], wbr_ref[k])
        acc = gate * u if acc is None else acc + gate * u
    out = _dot(acc, wout_ref[...])
    o_ref[...] = x + _rms(out, g3_ref[...])


def _merge(x, branches, g2, g3, wgate, wbr, wout):
    rows = x.shape[0]
    tm = _row_tile(rows, cap=512)
    row_spec = lambda c: pl.BlockSpec((tm, c), lambda i: (i, 0))
    return pl.pallas_call(
        _merge_kernel,
        grid=(rows // tm,),
        in_specs=[row_spec(D_MODEL)] + [row_spec(BRANCH)] * 4 +
                 [_const_spec((1, D_MODEL)), _const_spec((1, D_MODEL)),
                  _const_spec(wgate.shape), _const_spec(wbr.shape), _const_spec(wout.shape)],
        out_specs=row_spec(D_MODEL),
        out_shape=jax.ShapeDtypeStruct((rows, D_MODEL), F32),
        compiler_params=_cparams(("parallel",)),
        name="merge",
    )(x, *branches, g2, g3, wgate, wbr, wout)


KEY_STEP = 512


def _key_extents(T):
    his = sorted({min(T, KEY_STEP * (c + 1)) for c in range(-(-T // KEY_STEP))})
    return list(zip([0] + his[:-1], his))


def _stack_heads(x, width, take):
    return jnp.concatenate([x[:, width * h: width * h + take] for h in range(HEADS)], axis=0)


def _nsa_prompt_kernel(q_ref, pg_ref, wk_ref, g_ref, e_ref, o_ref, *, T, band):
    nb = T // NSA_BLOCK
    k_top = min(NSA_TOPK, nb)
    q0 = pl.program_id(1) * QB
    q = q_ref[...]
    zeros = jnp.zeros((QB, DK), F32)
    q_rows = jnp.concatenate(
        [jnp.concatenate([q[:, DK * h:DK * (h + 1)], zeros], axis=1) for h in range(HEADS)],
        axis=0).astype(BF16)
    pos = q0 + lax.broadcasted_iota(I32, (QB, 1), 0)
    lane = lax.broadcasted_iota(I32, (QB, LANES), 1)

    def per_head(fn):
        return jnp.concatenate([fn(h) for h in range(HEADS)], axis=0)

    def attend_values(s, mask, kv):
        parts = [_unnormalised_softmax(s[QB * h:QB * (h + 1)], mask) for h in range(HEADS)]
        e = jnp.concatenate([p[0] for p in parts], axis=0)
        r = jnp.concatenate([p[1] for p in parts], axis=0)
        return _dot_rows_split(e, kv) * r

    cm = jnp.sum(pg_ref[:, 0:LANES].reshape(nb, NSA_BLOCK, LANES), axis=1) * (1.0 / NSA_BLOCK)
    if nb < LANES:
        cm = jnp.concatenate([cm, jnp.zeros((LANES - nb, LANES), F32)], axis=0)
    cm = cm.astype(BF16)
    s_c = _dot_nt_rows_split(q_rows, cm)
    cmask = (lane + 1) * NSA_BLOCK - 1 <= pos
    p_c = per_head(lambda h: _masked_softmax(s_c[QB * h:QB * (h + 1)], cmask))
    o_c = _dot_rows_split(p_c, cm)

    cur = pos // NSA_BLOCK
    imp = p_c[0:QB] + p_c[QB:2 * QB] + p_c[2 * QB:3 * QB] + p_c[3 * QB:4 * QB]
    forced = (lane == 0) | (lane == cur) | (lane == cur - 1)
    imp = imp + jnp.where(forced, NSA_FORCED_BONUS, 0.0)
    imp = jnp.where(lane > cur, -1.0, imp)
    rank = jnp.zeros((QB, LANES), F32)
    for n in range(nb):
        c = imp[:, n:n + 1]
        beats = (c > imp) | ((c == imp) & (lane > n))
        rank = rank + jnp.where(beats, 1.0, 0.0)
    sel = jnp.where(rank < float(k_top), 1.0, 0.0)

    def selected(tk):
        in_sel = jnp.dot(sel.astype(BF16), e_ref[:, 0:tk], preferred_element_type=F32)
        kpos = lax.broadcasted_iota(I32, (QB, tk), 1)
        smask = (in_sel > 0.5) & (kpos <= pos)
        kv_s = pg_ref[0:tk, LANES:2 * LANES].astype(BF16)
        s_s = _dot_nt(q_rows, kv_s)
        return attend_values(s_s, smask, kv_s)

    start = pl.multiple_of(jnp.clip(q0 - NSA_WINDOW, 0, T - band), LANES)
    kv_w = wk_ref[pl.ds(start, band), :].astype(BF16)
    kposw = start + lax.broadcasted_iota(I32, (QB, band), 1)
    wmask = (kposw <= pos) & (kposw >= pos - NSA_WINDOW)
    o_w = attend_values(_dot_nt(q_rows, kv_w), wmask, kv_w)

    g = g_ref[...]

    def finish(o_s):
        outs = []
        for h in range(HEADS):
            r = slice(QB * h, QB * (h + 1))
            comb = (g[:, 3 * h:3 * h + 1] * o_c[r] + g[:, 3 * h + 1:3 * h + 2] * o_s[r]
                    + g[:, 3 * h + 2:3 * h + 3] * o_w[r])
            outs.append(comb[:, DK:2 * DK])
        o_ref[...] = jnp.concatenate(outs, axis=1)

    for lo, tk in _key_extents(T):
        @pl.when((q0 + QB > lo) & (q0 + QB <= tk))
        def _(tk=tk):
            finish(selected(tk))


def _nsa_prompt(q, pg, wkv, g, B, T):
    assert T % QB == 0 and T // NSA_BLOCK <= LANES
    rows = B * T
    nq = T // QB
    band = min(NSA_WINDOW + QB, T)
    nb = T // NSA_BLOCK
    expand = (np.arange(LANES)[:, None] == (np.arange(T)[None, :] // NSA_BLOCK)) & (np.arange(LANES)[:, None] < nb)
    expand = jnp.asarray(expand, BF16)
    return pl.pallas_call(
        functools.partial(_nsa_prompt_kernel, T=T, band=band),
        grid=(B, nq),
        in_specs=[pl.BlockSpec((QB, BRANCH), lambda b, i: (b * nq + i, 0)),
                  pl.BlockSpec((T, 256), lambda b, i: (b, 0)),
                  pl.BlockSpec((T, LANES), lambda b, i: (b, 0)),
                  pl.BlockSpec((QB, LANES), lambda b, i: (b * nq + i, 0)),
                  _const_spec((LANES, T))],
        out_specs=pl.BlockSpec((QB, BRANCH), lambda b, i: (b * nq + i, 0)),
        out_shape=jax.ShapeDtypeStruct((rows, BRANCH), F32),
        compiler_params=_cparams(("parallel", "parallel")),
        name="nsa_prompt",
    )(q, pg, wkv, g, expand)


def _mla_prompt_kernel(qlat_ref, qpe_ref, kv_ref, wv_ref, o_ref, *, T):
    q0 = pl.program_id(1) * QB
    q_cat = jnp.concatenate([_stack_heads(qlat_ref[...], LANES, LANES),
                             _stack_heads(qpe_ref[...], LANES, MLA_ROPE)], axis=1).astype(BF16)
    scale = (MLA_NOPE + MLA_ROPE) ** -0.5
    pos = q0 + lax.broadcasted_iota(I32, (QB, 1), 0)

    def attend(tk):
        kv = kv_ref[0:tk, :].astype(BF16)
        ckv = kv[:, 0:MLA_KV_LORA]
        s = _dot_nt(q_cat, kv) * scale
        mask = lax.broadcasted_iota(I32, (QB, tk), 1) <= pos
        parts = [_unnormalised_softmax(s[QB * h:QB * (h + 1)], mask) for h in range(HEADS)]
        e = jnp.concatenate([p[0] for p in parts], axis=0)
        r = jnp.concatenate([p[1] for p in parts], axis=0)
        o_lat = _dot_rows_split(e, ckv) * r
        o_cat = jnp.concatenate([o_lat[QB * h:QB * (h + 1)] for h in range(HEADS)], axis=1)
        o_ref[...] = _dot_rows_split(o_cat, wv_ref[...])

    for lo, tk in _key_extents(T):
        @pl.when((q0 + QB > lo) & (q0 + QB <= tk))
        def _(tk=tk):
            attend(tk)


def _mla_prompt(qlat, qpe, mla_new, wv, B, T):
    nq = T // QB
    rows = B * T
    return pl.pallas_call(
        functools.partial(_mla_prompt_kernel, T=T),
        grid=(B, nq),
        in_specs=[pl.BlockSpec((QB, 512), lambda b, i: (b * nq + i, 0)),
                  pl.BlockSpec((QB, 512), lambda b, i: (b * nq + i, 0)),
                  pl.BlockSpec((T, MLA_CACHE), lambda b, i: (b, 0)),
                  _const_spec(wv.shape)],
        out_specs=pl.BlockSpec((QB, BRANCH), lambda b, i: (b * nq + i, 0)),
        out_shape=jax.ShapeDtypeStruct((rows, BRANCH), F32),
        compiler_params=_cparams(("parallel", "parallel")),
        name="mla_prompt",
    )(qlat, qpe, mla_new, wv)


def _head_lanes(x_cols, shape):
    lane = lax.broadcasted_iota(I32, shape, 1)
    out = jnp.broadcast_to(x_cols[:, HEADS - 1:HEADS], shape)
    for h in range(HEADS - 2, -1, -1):
        out = jnp.where(lane < DK * (h + 1), jnp.broadcast_to(x_cols[:, h:h + 1], shape), out)
    return out


def _ssd_prompt_kernel(xbc_ref, dt_ref, z_ref, cw_ref, cb_ref, dtb_ref, alog_ref, dsk_ref, sn_ref,
                       o_ref, st_ref, buf_ref, s_ref):
    c = pl.program_id(1)
    C = CHUNK

    @pl.when(c == 0)
    def _():
        buf_ref[0:SUBLANES, :] = jnp.zeros((SUBLANES, SSD_CONV_DIM), F32)
        s_ref[...] = jnp.zeros_like(s_ref)

    x = xbc_ref[...]
    buf_ref[SUBLANES:SUBLANES + C, :] = x
    conv = cb_ref[...] + cw_ref[SSD_CONV - 1:SSD_CONV, :] * x
    for j in range(SSD_CONV - 1):
        conv = conv + cw_ref[j:j + 1, :] * buf_ref[pl.ds(SUBLANES - (SSD_CONV - 1) + j, C), :]
    buf_ref[0:SUBLANES, :] = x[C - SUBLANES:C, :]
    xa = _silu(conv)
    xs = xa[:, 0:BRANCH]
    dt = _softplus(dt_ref[...] + dtb_ref[...])
    a = dt * (-jnp.exp(alog_ref[...]))
    row = lax.broadcasted_iota(I32, (C, C), 0)
    col = lax.broadcasted_iota(I32, (C, C), 1)
    tril = row >= col
    tri = jnp.where(tril, 1.0, 0.0)
    cum_c = _dot_f32(tri, a)
    cum_r = _dot_nt_f32(a.T, tri)
    xdt = xs * _head_lanes(dt, (C, BRANCH))
    xdt_t = xdt.T
    dsk = dsk_ref[...]
    ys = []
    for h in range(HEADS):
        g = h // (HEADS // SSD_GROUPS)
        bm = xa[:, BRANCH + DK * g:BRANCH + DK * (g + 1)]
        cm = xa[:, BRANCH + DK * (SSD_GROUPS + g):BRANCH + DK * (SSD_GROUPS + g + 1)]
        ch = cum_c[:, h:h + 1]
        decay = jnp.exp(jnp.where(tril, ch - cum_r[h:h + 1, :], NEG))
        scores = _dot_nt(cm, bm) * decay
        xdt_h = xdt[:, DK * h:DK * (h + 1)]
        s_old = s_ref[h]
        y = _dot(scores, xdt_h) + _dot_nt(cm, s_old) * jnp.exp(ch)
        last = cum_c[C - 1:C, h:h + 1]
        w = jnp.exp(last - ch)
        s_ref[h] = s_old * jnp.exp(last) + _dot(xdt_t[DK * h:DK * (h + 1), :], bm * w)
        ys.append(y + dsk[:, h:h + 1] * xs[:, DK * h:DK * (h + 1)])
    y = jnp.concatenate(ys, axis=1)
    o_ref[...] = _rms(y * _silu(z_ref[...]), sn_ref[...])

    @pl.when(c == pl.num_programs(1) - 1)
    def _():
        st_ref[0] = s_ref[...]


def _ssd_prompt(xbc, dt, z, cw, cb, dtb, alog, dsk, sn, B, T):
    assert T % CHUNK == 0
    nc = T // CHUNK
    rows = B * T
    row_spec = lambda w: pl.BlockSpec((CHUNK, w), lambda b, c: (b * nc + c, 0))
    return pl.pallas_call(
        _ssd_prompt_kernel,
        grid=(B, nc),
        in_specs=[row_spec(SSD_CONV_DIM), row_spec(LANES), row_spec(BRANCH),
                  _const_spec(cw.shape), _const_spec(cb.shape), _const_spec(dtb.shape),
                  _const_spec(alog.shape), _const_spec(dsk.shape), _const_spec(sn.shape)],
        out_specs=[row_spec(BRANCH), pl.BlockSpec((1, HEADS, DK, DK), lambda b, c: (b, 0, 0, 0))],
        out_shape=[jax.ShapeDtypeStruct((rows, BRANCH), F32),
                   jax.ShapeDtypeStruct((B, HEADS, DK, DK), F32)],
        scratch_shapes=[pltpu.VMEM((SUBLANES + CHUNK, SSD_CONV_DIM), F32),
                        pltpu.VMEM((HEADS, DK, DK), F32)],
        compiler_params=_cparams(("parallel", "arbitrary")),
        name="ssd_prompt",
    )(xbc, dt, z, cw, cb, dtb, alog, dsk, sn)


def _hgrn_gates(fr, lb):
    log_sig = jnp.minimum(fr, 0.0) - jnp.log1p(jnp.exp(-jnp.abs(fr)))
    a = jnp.log(jnp.maximum(lb, LB_FLOOR))
    b = jnp.log1p(-lb) + log_sig
    log_f = jnp.maximum(a, b) + jnp.log1p(jnp.exp(-jnp.abs(a - b)))
    return log_f, (1.0 - lb) * _sigmoid(-fr)


def _block_reference_rows(G, m):
    C = G.shape[0]
    if m >= 4:
        parts = []
        for p in range(C // (2 * m)):
            r = p * 2 * m + m - 1
            parts.append(jnp.broadcast_to(G[r:r + 1, :], (2 * m, G.shape[1])))
        return parts[0] if len(parts) == 1 else jnp.concatenate(parts, axis=0)
    t = lax.broadcasted_iota(I32, G.shape, 0)
    if m == 1:
        return jnp.where((t & 1) == 1, pltpu.roll(G, 1, 0), G)
    r = t & 3
    return jnp.where(r == 0, pltpu.roll(G, C - 1, 0),
                     jnp.where(r == 1, G, jnp.where(r == 2, pltpu.roll(G, 1, 0), pltpu.roll(G, 2, 0))))


def _hgrn_prompt_kernel(hg_ref, lb_ref, hn_ref, seg_ref, o_ref, st_ref, s_ref):
    c = pl.program_id(1)
    C = CHUNK
    W = BRANCH

    @pl.when(c == 0)
    def _():
        s_ref[...] = jnp.zeros_like(s_ref)

    q = hg_ref[:, 0:W]
    v = hg_ref[:, 2 * W:3 * W]
    log_f, kin = _hgrn_gates(hg_ref[:, W:2 * W], lb_ref[...])
    row = lax.broadcasted_iota(I32, (C, C), 0)
    col = lax.broadcasted_iota(I32, (C, C), 1)
    G = _dot_f32(jnp.where(row >= col, 1.0, 0.0), log_f)
    lane_head = lax.broadcasted_iota(I32, (C, W), 1) // DK
    t_idx = lax.broadcasted_iota(I32, (C, W), 0)

    def stack_heads(a):
        return jnp.concatenate([jnp.where(lane_head == h, a, 0.0) for h in range(HEADS)],
                               axis=0).astype(BF16)

    t4 = lax.broadcasted_iota(I32, (HEADS * C, C), 0) & (C - 1)
    s4 = lax.broadcasted_iota(I32, (HEADS * C, C), 1)
    att = jnp.where(t4 == s4, _dot_nt_rows_split(stack_heads(q), kin), 0.0)
    m = C // 2
    while m >= 1:
        R = _block_reference_rows(G, m)
        upper = (t_idx & m) != 0
        A = jnp.where(upper, q * jnp.exp(jnp.minimum(G - R, 0.0)), 0.0)
        Bm = jnp.where(upper, 0.0, kin * jnp.exp(jnp.minimum(R - G, 0.0)))
        lm = int(math.log2(m))
        pair = ((t4 >> lm) ^ (s4 >> lm)) == 1
        att = att + jnp.where(pair & (t4 > s4), _dot_nt_rows_split(stack_heads(A), Bm), 0.0)
        m //= 2
    y4 = _dot_rows_split(att, v)
    y = jnp.zeros((C, W), F32)
    for h in range(HEADS):
        y = y + jnp.where(lane_head == h, y4[C * h:C * (h + 1)], 0.0)
    s_old = s_ref[...]
    y = y + _dot(q * jnp.exp(G), s_old)
    last = G[C - 1:C, :]
    ke_t = (kin * jnp.exp(last - G)).T
    G_t = G.T
    blk = (lax.broadcasted_iota(I32, (W, W), 0) // DK) == (lax.broadcasted_iota(I32, (W, W), 1) // DK)
    s_ref[...] = s_old * jnp.exp(G_t[:, C - 1:C]) + jnp.where(blk, _dot(ke_t, v), 0.0)
    ms = _dot_f32(y * y, seg_ref[...])
    o = y * lax.rsqrt(ms + EPS) * hn_ref[...]
    o_ref[...] = o * _silu(hg_ref[:, 3 * W:4 * W])

    @pl.when(c == pl.num_programs(1) - 1)
    def _():
        for h in range(HEADS):
            st_ref[0, h] = s_ref[DK * h:DK * (h + 1), DK * h:DK * (h + 1)]


def _hgrn_prompt(hg, lb, hn, B, T):
    nc = T // CHUNK
    rows = B * T
    seg = (np.arange(BRANCH)[:, None] // DK == np.arange(BRANCH)[None, :] // DK) / float(DK)
    seg = jnp.asarray(seg, F32)
    row_spec = lambda w: pl.BlockSpec((CHUNK, w), lambda b, c: (b * nc + c, 0))
    return pl.pallas_call(
        _hgrn_prompt_kernel,
        grid=(B, nc),
        in_specs=[row_spec(4 * BRANCH), _const_spec(lb.shape), _const_spec(hn.shape),
                  _const_spec(seg.shape)],
        out_specs=[row_spec(BRANCH), pl.BlockSpec((1, HEADS, DK, DK), lambda b, c: (b, 0, 0, 0))],
        out_shape=[jax.ShapeDtypeStruct((rows, BRANCH), F32),
                   jax.ShapeDtypeStruct((B, HEADS, DK, DK), F32)],
        scratch_shapes=[pltpu.VMEM((BRANCH, BRANCH), F32)],
        compiler_params=_cparams(("parallel", "arbitrary")),
        name="hgrn_prompt",
    )(hg, lb, hn, seg)


PAGES_PER_STEP = 32


def _mla_decode_kernel(pt_ref, q_ref, qt_ref, new_ref, *rest, n_pg):
    page_refs = rest[:n_pg]
    wv_ref, o_ref, qb_ref, m_ref, l_ref, acc_ref = rest[n_pg:]
    del pt_ref
    b = pl.program_id(0)
    j = pl.program_id(1)
    scale = (MLA_NOPE + MLA_ROPE) ** -0.5
    page = page_refs[0].shape[1]

    @pl.when(j == 0)
    def _():
        qt = qt_ref[0]
        for h in range(HEADS):
            qb_ref[h] = jnp.broadcast_to(qt[:, h:h + 1], (MLA_CACHE, LANES))
        m_ref[...] = jnp.full_like(m_ref, NEG)
        l_ref[...] = jnp.zeros_like(l_ref)
        acc_ref[...] = jnp.zeros_like(acc_ref)

    scores = [[] for _ in range(HEADS)]
    for r in page_refs:
        kt = r[...]
        for h in range(HEADS):
            scores[h].append(jnp.sum(kt * qb_ref[h], axis=0, keepdims=True))
    s = jnp.concatenate([jnp.concatenate(scores[h], axis=1) for h in range(HEADS)], axis=0) * scale
    m_old = m_ref[0:HEADS, :]
    m_new = jnp.maximum(m_old, jnp.max(s, axis=-1, keepdims=True))
    alpha = jnp.exp(m_old - m_new)
    p = jnp.exp(s - m_new[:, 0:1])
    l_ref[0:HEADS, :] = alpha * l_ref[0:HEADS, :] + jnp.sum(p, axis=-1, keepdims=True)
    m_ref[0:HEADS, :] = m_new
    for h in range(HEADS):
        acc = alpha[h:h + 1, 0:1] * acc_ref[h]
        for t, r in enumerate(page_refs):
            acc = acc + r[0:MLA_KV_LORA, :] * p[h:h + 1, t * page:(t + 1) * page]
        acc_ref[h] = acc

    @pl.when(j == pl.num_programs(1) - 1)
    def _():
        new = new_ref[0]
        s_new = jnp.sum(q_ref[0] * new, axis=-1, keepdims=True) * scale
        ones = jnp.ones((SUBLANES, LANES), F32)
        outs = []
        for h in range(HEADS):
            m_old = m_ref[h:h + 1, :]
            m_fin = jnp.maximum(m_old, s_new[h:h + 1, :])
            alpha = jnp.exp(m_old - m_fin)
            p_n = jnp.exp(s_new[h:h + 1, :] - m_fin)
            l = alpha * l_ref[h:h + 1, :] + p_n
            lat = _dot_nt_f32(ones, acc_ref[h])[0:1, :]
            outs.append((alpha * lat + p_n * new[:, 0:MLA_KV_LORA]) / l)
        o_cat = jnp.concatenate(outs, axis=1)
        o = _dot(jnp.broadcast_to(o_cat, (SUBLANES, HEADS * MLA_KV_LORA)), wv_ref[...])
        o_ref[pl.ds(b, 1), :] = o[0:1, :]


def _mla_decode(layer, page_table, q8, qt, new_rows, cache_t, wv):
    BS, npages = page_table.shape
    page = cache_t.shape[3]
    n_pg = min(PAGES_PER_STEP, npages)
    assert npages % n_pg == 0 and page == LANES
    steps = npages // n_pg

    def page_spec(i):
        return pl.BlockSpec((None, None, MLA_CACHE, page),
                            lambda b, j, pt: (layer, pt[b, j * n_pg + i], 0, 0))

    grid_spec = pltpu.PrefetchScalarGridSpec(
        num_scalar_prefetch=1,
        grid=(BS, steps),
        in_specs=[pl.BlockSpec((1, SUBLANES, MLA_CACHE), lambda b, j, pt: (b, 0, 0)),
                  pl.BlockSpec((1, MLA_CACHE, HEADS), lambda b, j, pt: (b, 0, 0)),
                  pl.BlockSpec((1, 1, MLA_CACHE), lambda b, j, pt: (b, 0, 0))]
                 + [page_spec(i) for i in range(n_pg)]
                 + [pl.BlockSpec(wv.shape, lambda b, j, pt: (0, 0))],
        out_specs=pl.BlockSpec((BS, BRANCH), lambda b, j, pt: (0, 0)),
        scratch_shapes=[pltpu.VMEM((HEADS, MLA_CACHE, LANES), F32),
                        pltpu.VMEM((SUBLANES, LANES), F32), pltpu.VMEM((SUBLANES, LANES), F32),
                        pltpu.VMEM((HEADS, MLA_KV_LORA, LANES), F32)],
    )
    return pl.pallas_call(
        functools.partial(_mla_decode_kernel, n_pg=n_pg),
        grid_spec=grid_spec,
        out_shape=jax.ShapeDtypeStruct((BS, BRANCH), F32),
        compiler_params=_cparams(("arbitrary", "arbitrary")),
        name="mla_decode",
    )(page_table, q8, qt, new_rows, *([cache_t] * n_pg), wv)


def _nsa_cmp_decode_kernel(pt_ref, qt_ref, *rest, n_pg, past):
    page_refs = rest[:n_pg]
    oc_ref, imp_ref, qb_ref, ts_ref, vb_ref = rest[n_pg:]
    del pt_ref
    j = pl.program_id(1)
    steps = ts_ref.shape[0]
    page = page_refs[0].shape[1]
    nblk = 2 * LANES

    @pl.when(j == 0)
    def _():
        qt = qt_ref[0]
        for h in range(HEADS):
            qb_ref[h] = jnp.broadcast_to(qt[:, h:h + 1], (DK, LANES))

    for t, r in enumerate(page_refs):
        kt = r[0:DK, :]
        for h in range(HEADS):
            ts_ref[j, h:h + 1, t * page:(t + 1) * page] = jnp.sum(kt * qb_ref[h], axis=0, keepdims=True)
        vb_ref[j, :, t * page:(t + 1) * page] = r[DK:2 * DK, :]

    @pl.when(j == steps - 1)
    def _():
        blk = lax.broadcasted_iota(I32, (HEADS, nblk), 1)
        first = lax.broadcasted_iota(I32, (HEADS, page), 1) < NSA_BLOCK
        s = jnp.zeros((HEADS, nblk), F32)
        for jj in range(steps):
            ts = ts_ref[jj, 0:HEADS, :]
            for t in range(n_pg):
                x = ts[:, t * page:(t + 1) * page]
                b0 = 2 * (jj * n_pg + t)
                s0 = jnp.sum(jnp.where(first, x, 0.0), axis=1, keepdims=True)
                s1 = jnp.sum(jnp.where(first, 0.0, x), axis=1, keepdims=True)
                s = jnp.where(blk == b0, s0, jnp.where(blk == b0 + 1, s1, s))
        s = s * (1.0 / NSA_BLOCK)
        cmask = (blk + 1) * NSA_BLOCK - 1 <= past
        p = _masked_softmax(s, cmask)
        cur = past // NSA_BLOCK
        imp = jnp.sum(p, axis=0, keepdims=True)
        blk1 = blk[0:1]
        forced = (blk1 == 0) | (blk1 == cur - 1)
        imp = imp + jnp.where(forced, NSA_FORCED_BONUS, 0.0)
        imp_ref[0] = jnp.where(blk1 >= cur, -3.0, imp)

        pw = jnp.concatenate([p * (1.0 / NSA_BLOCK), jnp.zeros((SUBLANES - HEADS, nblk), F32)], axis=0)
        nb_step = 2 * n_pg
        expand = jnp.where(lax.broadcasted_iota(I32, (nb_step, n_pg * page), 0)
                           == lax.broadcasted_iota(I32, (nb_step, n_pg * page), 1) // NSA_BLOCK, 1.0, 0.0)
        accs = [jnp.zeros((DK, page), F32) for _ in range(HEADS)]
        for jj in range(steps):
            w = _dot_f32(pw[:, nb_step * jj:nb_step * (jj + 1)], expand)
            for t in range(n_pg):
                v = vb_ref[jj, :, t * page:(t + 1) * page]
                for h in range(HEADS):
                    accs[h] = accs[h] + v * w[h:h + 1, t * page:(t + 1) * page]
        ones = jnp.ones((SUBLANES, LANES), F32)
        oc_ref[0] = jnp.concatenate([_dot_nt_f32(ones, a)[0:1, :] for a in accs], axis=0)


def _nsa_cmp_decode(layer, page_table, qt, cache_t, past):
    BS, npages = page_table.shape
    page = cache_t.shape[3]
    n_pg = min(PAGES_PER_STEP, npages)
    assert page == 2 * NSA_BLOCK and page == LANES and 2 * npages <= 2 * LANES and npages % n_pg == 0
    steps = npages // n_pg

    def page_spec(i):
        return pl.BlockSpec((None, None, 2 * DK, page),
                            lambda b, j, pt: (layer, pt[b, j * n_pg + i], 0, 0))

    grid_spec = pltpu.PrefetchScalarGridSpec(
        num_scalar_prefetch=1,
        grid=(BS, steps),
        in_specs=[pl.BlockSpec((1, DK, HEADS), lambda b, j, pt: (b, 0, 0))]
                 + [page_spec(i) for i in range(n_pg)],
        out_specs=[pl.BlockSpec((1, HEADS, DK), lambda b, j, pt: (b, 0, 0)),
                   pl.BlockSpec((1, 1, 2 * LANES), lambda b, j, pt: (b, 0, 0))],
        scratch_shapes=[pltpu.VMEM((HEADS, DK, LANES), F32),
                        pltpu.VMEM((steps, SUBLANES, n_pg * page), F32),
                        pltpu.VMEM((steps, DK, n_pg * page), F32)],
    )
    return pl.pallas_call(
        functools.partial(_nsa_cmp_decode_kernel, n_pg=n_pg, past=past),
        grid_spec=grid_spec,
        out_shape=[jax.ShapeDtypeStruct((BS, HEADS, DK), F32),
                   jax.ShapeDtypeStruct((BS, 1, 2 * LANES), F32)],
        compiler_params=_cparams(("parallel", "arbitrary")),
        name="nsa_cmp_decode",
    )(page_table, qt, *([cache_t] * n_pg))


def _topk_kernel(imp_ref, idx_ref, *, past, k_top):
    BS = imp_ref.shape[0]
    cur = past // NSA_BLOCK
    blk = lax.broadcasted_iota(I32, (BS, 2 * LANES), 1).astype(F32)
    lane_x = lax.broadcasted_iota(I32, (BS, LANES), 1)
    imp = jnp.concatenate([imp_ref[...], jnp.where(lane_x == 0, NSA_FORCED_BONUS, -3.0)], axis=1)
    blk = jnp.concatenate([blk, (cur + lane_x).astype(F32)], axis=1)
    out = jnp.zeros((BS, LANES), F32)
    for k in range(k_top):
        m = jnp.max(imp, axis=-1, keepdims=True)
        pick = jnp.min(jnp.where(imp == m, blk, 1e9), axis=-1, keepdims=True)
        imp = jnp.where(blk == pick, -4.0, imp)
        out = jnp.where(lane_x == k, pick, out)
    idx_ref[...] = out.astype(I32)


def _topk(imp, past, k_top):
    BS = imp.shape[0]
    return pl.pallas_call(
        functools.partial(_topk_kernel, past=past, k_top=k_top),
        out_shape=jax.ShapeDtypeStruct((BS, LANES), I32),
        name="nsa_topk",
    )(imp)


def _nsa_sel_decode_kernel(pt_ref, idx_ref, q_ref, pgn_ref, wn_ref, win_ref, g_ref, oc_ref, *rest,
                           k_top, n_blocks):
    blk_refs = rest[:k_top]
    (o_ref,) = rest[k_top:]
    del pt_ref
    b = pl.program_id(0)
    q = q_ref[0]
    page = blk_refs[0].shape[1]

    def attend(kt, vt, s_mask, k_new, v_new, new_on):
        s = _dot(q, kt)
        if s_mask is not None:
            s = jnp.where(s_mask, s, NEG)
        s_n = jnp.where(new_on, jnp.sum(q * k_new, axis=-1, keepdims=True), NEG)
        m = jnp.maximum(jnp.max(s, axis=-1, keepdims=True), s_n)
        e = jnp.exp(s - m)
        if s_mask is not None:
            e = jnp.where(s_mask, e, 0.0)
        e_n = jnp.where(new_on, jnp.exp(s_n - m), 0.0)
        l = jnp.sum(e, axis=-1, keepdims=True) + e_n
        return (_dot_nt(e, vt) + e_n * v_new) / l

    kt = jnp.concatenate([r[0:DK, :] for r in blk_refs], axis=1).astype(BF16)
    vt = jnp.concatenate([r[DK:2 * DK, :] for r in blk_refs], axis=1).astype(BF16)
    lane = lax.broadcasted_iota(I32, (1, k_top * page), 1)
    slot = lane // page
    half = (lane % page) // NSA_BLOCK
    valid = jnp.zeros((1, k_top * page), I32)
    has_new = jnp.zeros((), jnp.bool_)
    for k in range(k_top):
        ik = idx_ref[b, k]
        ok = jnp.where(half == (ik & 1), (ik < n_blocks).astype(I32), 0)
        valid = jnp.where(slot == k, ok, valid)
        has_new = has_new | (ik == n_blocks)
    pgn = pgn_ref[0]
    o_s = attend(kt, vt, valid > 0, pgn[:, 2 * DK:3 * DK], pgn[:, 3 * DK:4 * DK], has_new)
    wn = wn_ref[0]
    o_w = attend(win_ref[0:DK, :].astype(BF16), win_ref[DK:2 * DK, :].astype(BF16), None,
                 wn[:, 0:DK], wn[:, DK:2 * DK], True)
    o_c = oc_ref[0]
    g = g_ref[0]
    outs = []
    for h in range(HEADS):
        outs.append(g[:, 3 * h:3 * h + 1] * o_c[h:h + 1] + g[:, 3 * h + 1:3 * h + 2] * o_s[h:h + 1]
                    + g[:, 3 * h + 2:3 * h + 3] * o_w[h:h + 1])
    o_ref[pl.ds(b, 1), :] = jnp.concatenate(outs, axis=1)


def _nsa_sel_decode(layer, page_table, idx, q8, pg_new, w_new, win_t, g_new, o_c, cache_t, k_top):
    BS, npages = page_table.shape
    n_blocks = 2 * npages
    page = cache_t.shape[3]
    nwin = win_t.shape[3]
    blk = jnp.minimum(idx[:, :k_top], n_blocks - 1)
    phys = jnp.take_along_axis(page_table, blk // 2, axis=1)

    def blk_spec(k):
        return pl.BlockSpec((None, None, 2 * DK, page), lambda b, ph, ix: (layer, ph[b, k], 1, 0))

    row3 = lambda w: pl.BlockSpec((1, 1, w), lambda b, pt, ix: (b, 0, 0))
    grid_spec = pltpu.PrefetchScalarGridSpec(
        num_scalar_prefetch=2,
        grid=(BS,),
        in_specs=[pl.BlockSpec((1, SUBLANES, DK), lambda b, pt, ix: (b, 0, 0)),
                  row3(256), row3(LANES),
                  pl.BlockSpec((None, None, 2 * DK, nwin), lambda b, pt, ix: (layer, b, 0, 0)),
                  row3(LANES),
                  pl.BlockSpec((1, HEADS, DK), lambda b, pt, ix: (b, 0, 0))]
                 + [blk_spec(k) for k in range(k_top)],
        out_specs=pl.BlockSpec((BS, BRANCH), lambda b, pt, ix: (0, 0)),
    )
    return pl.pallas_call(
        functools.partial(_nsa_sel_decode_kernel, k_top=k_top, n_blocks=n_blocks),
        grid_spec=grid_spec,
        out_shape=jax.ShapeDtypeStruct((BS, BRANCH), F32),
        compiler_params=_cparams(("arbitrary",)),
        name="nsa_sel_decode",
    )(phys, idx, q8, pg_new, w_new, win_t, g_new, o_c, *([cache_t] * k_top))


def _column(row, eye):
    return jnp.sum(jnp.where(eye, jnp.broadcast_to(row, eye.shape), 0.0), axis=1, keepdims=True)


def _rec_decode_kernel(xbc_ref, cbuf_ref, dt_ref, z_ref, hg_ref, s_ref, hs_ref,
                       cw_ref, cb_ref, dtb_ref, alog_ref, dsk_ref, sn_ref, lb_ref, hn_ref,
                       os_ref, oh_ref, sn_out, hs_out):
    b = pl.program_id(0)
    eye = lax.broadcasted_iota(I32, (DK, DK), 0) == lax.broadcasted_iota(I32, (DK, DK), 1)

    cbuf = cbuf_ref[...]
    xn = xbc_ref[0]
    conv = cb_ref[...] + cw_ref[SSD_CONV - 1:SSD_CONV, :] * xn
    for j in range(SSD_CONV - 1):
        conv = conv + cw_ref[j:j + 1, :] * cbuf[j:j + 1, :]
    xa = _silu(conv)
    dt = _softplus(dt_ref[0] + dtb_ref[...])
    ea = jnp.exp(dt * (-jnp.exp(alog_ref[...])))
    dsk = dsk_ref[...]
    ys = []
    for h in range(HEADS):
        g = h // (HEADS // SSD_GROUPS)
        xs = xa[:, DK * h:DK * (h + 1)]
        bm = xa[:, BRANCH + DK * g:BRANCH + DK * (g + 1)]
        cm = xa[:, BRANCH + DK * (SSD_GROUPS + g):BRANCH + DK * (SSD_GROUPS + g + 1)]
        xdt_col = _column(xs * dt[:, h:h + 1], eye)
        s_old = s_ref[h]
        ea_h = ea[:, h:h + 1]
        y_col = (jnp.sum(cm * bm, axis=-1, keepdims=True) * xdt_col
                 + jnp.sum(s_old * cm, axis=-1, keepdims=True) * ea_h)
        sn_out[0, h] = s_old * ea_h + xdt_col * bm
        y_row = jnp.sum(jnp.where(eye, jnp.broadcast_to(y_col, (DK, DK)), 0.0), axis=0, keepdims=True)
        ys.append(y_row + dsk[:, h:h + 1] * xs)
    y = jnp.concatenate(ys, axis=1)
    os_ref[pl.ds(b, 1), :] = _rms(y * _silu(z_ref[0]), sn_ref[...])

    W = BRANCH
    hg = hg_ref[0]
    q = hg[:, 0:W]
    v = hg[:, 2 * W:3 * W]
    log_f, kin = _hgrn_gates(hg[:, W:2 * W], lb_ref[...])
    ef = jnp.exp(log_f)
    hn = hn_ref[...]
    outs = []
    for h in range(HEADS):
        r = slice(DK * h, DK * (h + 1))
        s_old = hs_ref[h]
        att = jnp.sum(q[:, r] * kin[:, r], axis=-1, keepdims=True)
        y = att * v[:, r] + jnp.sum(s_old * _column(q[:, r] * ef[:, r], eye), axis=0, keepdims=True)
        hs_out[0, h] = s_old * _column(ef[:, r], eye) + _column(kin[:, r], eye) * v[:, r]
        outs.append(_rms(y, hn[:, r]))
    o = jnp.concatenate(outs, axis=1)
    oh_ref[pl.ds(b, 1), :] = o * _silu(hg[:, 3 * W:4 * W])


def _rec_decode(layer, xbc_new, conv_state, dt_new, z_new, hg_new, state_ssd, state_hgrn,
                cw, cb, dtb, alog, dsk, sn, lb, hn):
    BS = xbc_new.shape[0]
    row3 = lambda w: pl.BlockSpec((1, 1, w), lambda b: (b, 0, 0))
    st_in = pl.BlockSpec((None, None, HEADS, DK, DK), lambda b: (layer, b, 0, 0, 0))
    st_out = pl.BlockSpec((1, HEADS, DK, DK), lambda b: (b, 0, 0, 0))
    o_spec = pl.BlockSpec((BS, BRANCH), lambda b: (0, 0))
    consts = (cw, cb, dtb, alog, dsk, sn, lb, hn)
    return pl.pallas_call(
        _rec_decode_kernel,
        grid=(BS,),
        in_specs=[row3(SSD_CONV_DIM),
                  pl.BlockSpec((None, None, SSD_CONV - 1, SSD_CONV_DIM), lambda b: (layer, b, 0, 0)),
                  row3(LANES), row3(BRANCH), row3(4 * BRANCH), st_in, st_in]
                 + [_const_spec(c.shape) for c in consts],
        out_specs=[o_spec, o_spec, st_out, st_out],
        out_shape=[jax.ShapeDtypeStruct((BS, BRANCH), F32), jax.ShapeDtypeStruct((BS, BRANCH), F32),
                   jax.ShapeDtypeStruct((BS, HEADS, DK, DK), F32),
                   jax.ShapeDtypeStruct((BS, HEADS, DK, DK), F32)],
        compiler_params=_cparams(("arbitrary",)),
        name="rec_decode",
    )(xbc_new, conv_state, dt_new, z_new, hg_new, state_ssd, state_hgrn, *consts)


def _rot_cols(w):
    half = MLA_ROPE // 2
    return jnp.concatenate([-w[..., half:], w[..., :half]], axis=-1)


def _pad_cols(w, width):
    return jnp.pad(w, [(0, 0)] * (w.ndim - 1) + [(0, width - w.shape[-1])])


def _pack_w_in(w):
    sizes = (256, 384, 12, MLA_Q_LORA, MLA_KV_LORA, MLA_ROPE, 256, SSD_CONV_DIM, HEADS, 256, 256, 256, 256)
    offs = np.cumsum((0,) + sizes)
    (nsa_q, nsa_kv, nsa_g, cq, ckv, kpe, z, xbc, dt, hq, hf, hi, hgate) = (
        w[:, offs[i]:offs[i + 1]] for i in range(len(sizes)))
    gate = w[:, offs[-1]:]
    packed = jnp.concatenate([
        nsa_q, nsa_kv[:, :256], nsa_kv[:, 256:], _pad_cols(nsa_g, LANES), cq, ckv,
        _pad_cols(kpe, LANES), _pad_cols(_rot_cols(kpe), LANES), _pad_cols(dt, LANES), z, xbc,
        hq, hf, hi, hgate], axis=1)
    assert packed.shape[1] == PACKED_WIDTH
    return packed.astype(BF16), gate.astype(BF16)


def _pack_mla(wuq, wukv):
    nope = wuq[:, :, :MLA_NOPE].reshape(MLA_Q_LORA, HEADS * MLA_NOPE)
    pe = wuq[:, :, MLA_NOPE:]
    pe_p = _pad_cols(pe, LANES).reshape(MLA_Q_LORA, HEADS * LANES)
    pr_p = _pad_cols(_rot_cols(pe), LANES).reshape(MLA_Q_LORA, HEADS * LANES)
    wuq_p = jnp.concatenate([nope, pe_p, pr_p], axis=1).astype(BF16)
    eye = jnp.eye(HEADS, dtype=wukv.dtype)
    wukn = jnp.einsum('rhn,hg->hngr', wukv[:, :, :MLA_NOPE], eye).reshape(HEADS * MLA_NOPE, HEADS * MLA_KV_LORA)
    wv = jnp.einsum('rhv,hg->hrgv', wukv[:, :, MLA_NOPE:], eye).reshape(HEADS * MLA_KV_LORA, HEADS * MLA_V)
    return wuq_p, wukn.astype(BF16), wv.astype(BF16)


def _rope_table(positions):
    half = MLA_ROPE // 2
    freq = ROPE_THETA ** (-jnp.arange(half, dtype=F32) / half)
    ang = positions.astype(F32)[:, None] * freq
    cos = jnp.concatenate([jnp.cos(ang), jnp.cos(ang)], axis=1)
    sin = jnp.concatenate([jnp.sin(ang), jnp.sin(ang)], axis=1)
    return jnp.concatenate([_pad_cols(cos, LANES), _pad_cols(sin, LANES)], axis=1)


def _pad_row(v, width=LANES):
    return _pad_cols(v.reshape(1, -1).astype(F32), width)


def kernel(x_prompt, x_sample, cache_nsa_kv, cache_mla, cache_nsa_win, state_ssd, state_ssd_conv, state_hgrn, page_table, norm_g, ffn_w_in, ffn_w_out, w_in, mla_q_norm, mla_kv_norm, mla_w_uq, mla_w_ukv, ssd_conv_w, ssd_conv_b, ssd_dt_bias, ssd_a_log, ssd_d, ssd_norm, hg_lb_logits, hg_norm, w_branch, w_out):
    BP, T, D = x_prompt.shape
    BS = x_sample.shape[0]
    depth = w_in.shape[0]
    n_pool, page = cache_nsa_kv.shape[1], cache_nsa_kv.shape[2]
    npages = page_table.shape[1]
    past = npages * page
    n_win = cache_nsa_win.shape[2]
    rows_p = BP * T
    assert D == D_MODEL and x_sample.shape[1] == 1 and BS % SUBLANES == 0
    assert n_win == min(NSA_WINDOW, past) and past % NSA_BLOCK == 0
    k_top = min(NSA_TOPK, past // NSA_BLOCK + 1)

    xp = x_prompt.reshape(rows_p, D)
    xs = x_sample.reshape(BS, D)
    cs_p = _rope_table(jnp.tile(jnp.arange(T, dtype=I32), BP))
    cs_s = _rope_table(jnp.full((BS,), past, I32))
    lbp = jax.nn.softmax(hg_lb_logits.astype(F32), axis=0)
    lower_bounds = jnp.cumsum(lbp, axis=0) - lbp[0]

    nsa_t = jnp.transpose(cache_nsa_kv, (0, 1, 3, 4, 2)).reshape(depth, n_pool, 4 * DK, page)
    mla_t = jnp.transpose(cache_mla, (0, 1, 3, 2))
    win_t = jnp.transpose(cache_nsa_win, (0, 1, 3, 4, 2)).reshape(depth, BS, 2 * DK, n_win)

    def row3(a):
        return a.reshape(BS, 1, a.shape[1])

    outs = [[] for _ in range(12)]
    for l in range(depth):
        g = norm_g[l].astype(F32)
        grow = lambda i: g[i].reshape(1, D)
        wi = ffn_w_in[l].astype(BF16)
        wo = ffn_w_out[l].astype(BF16)
        w_pack, w_gate = _pack_w_in(w_in[l])
        wuq_p, wukn, wv = _pack_mla(mla_w_uq[l], mla_w_ukv[l])
        mqn = mla_q_norm[l].reshape(1, -1)
        mkvn = mla_kv_norm[l].reshape(1, -1)

        xp = _ffn(xp, grow(0), grow(1), wi[0], wo[0])
        xs = _ffn(xs, grow(0), grow(1), wi[0], wo[0])
        (nsa_q, nsa_pg, nsa_w, nsa_g, qlat, qpe, mla_new, ssd_z, ssd_xbc, ssd_dt, hg) = _inproj(
            xp, grow(2), cs_p, w_pack, mqn, mkvn, wuq_p, wukn)
        (nsa_q_s, nsa_pg_s, nsa_w_s, nsa_g_s, qlat_s, qpe_s, mla_new_s, ssd_z_s, ssd_xbc_s, ssd_dt_s,
         hg_s) = _inproj(xs, grow(2), cs_s, w_pack, mqn, mkvn, wuq_p, wukn)

        o_nsa = _nsa_prompt(nsa_q, nsa_pg, nsa_w, nsa_g, BP, T)
        pad8 = lambda a: jnp.pad(a, ((0, 0), (0, SUBLANES - HEADS), (0, 0)))
        q4 = nsa_q_s.reshape(BS, HEADS, DK)
        q8 = pad8(q4)
        o_c, imp = _nsa_cmp_decode(l, page_table, jnp.transpose(q4, (0, 2, 1)), nsa_t, past)
        idx = _topk(imp.reshape(BS, 2 * LANES), past, k_top)
        o_nsa_s = _nsa_sel_decode(l, page_table, idx, q8, row3(nsa_pg_s), row3(nsa_w_s), win_t,
                                  row3(nsa_g_s), o_c, nsa_t, k_top)

        o_mla = _mla_prompt(qlat, qpe, mla_new, wv, BP, T)
        q_cat = jnp.concatenate([qlat_s.reshape(BS, HEADS, LANES),
                                 qpe_s.reshape(BS, HEADS, LANES)[:, :, :MLA_ROPE]], axis=-1)
        o_mla_s = _mla_decode(l, page_table, pad8(q_cat), jnp.transpose(q_cat, (0, 2, 1)),
                              row3(mla_new_s), mla_t, wv)

        cw = ssd_conv_w[l].astype(F32)
        cb = ssd_conv_b[l].reshape(1, -1).astype(F32)
        dtb = _pad_row(ssd_dt_bias[l])
        alog = _pad_row(ssd_a_log[l])
        dsk = _pad_row(ssd_d[l])
        sn = ssd_norm[l].reshape(1, -1).astype(F32)
        lb = lower_bounds[l].reshape(1, -1)
        hn = hg_norm[l].reshape(1, -1).astype(F32)
        o_ssd, ssd_p = _ssd_prompt(ssd_xbc, ssd_dt, ssd_z, cw, cb, dtb, alog, dsk, sn, BP, T)
        o_hg, hgrn_p = _hgrn_prompt(hg, lb, hn, BP, T)
        o_ssd_s, o_hg_s, ssd_s, hgrn_s = _rec_decode(
            l, row3(ssd_xbc_s), state_ssd_conv, row3(ssd_dt_s), row3(ssd_z_s), row3(hg_s), state_ssd,
            state_hgrn, cw, cb, dtb, alog, dsk, sn, lb, hn)

        wbr = w_branch[l].astype(BF16)
        wout = w_out[l].astype(BF16)
        xp = _merge(xp, (o_nsa, o_mla, o_ssd, o_hg), grow(2), grow(3), w_gate, wbr, wout)
        xs = _merge(xs, (o_nsa_s, o_mla_s, o_ssd_s, o_hg_s), grow(2), grow(3), w_gate, wbr, wout)
        xp = _ffn(xp, grow(4), grow(5), wi[1], wo[1])
        xs = _ffn(xs, grow(4), grow(5), wi[1], wo[1])

        n_keep = min(NSA_WINDOW, T)
        layer_out = (
            nsa_pg.reshape(BP, T, 4, DK), nsa_pg_s.reshape(BS, 1, 4, DK),
            mla_new.reshape(BP, T, MLA_CACHE), mla_new_s.reshape(BS, 1, MLA_CACHE),
            nsa_w.reshape(BP, T, 2, DK)[:, T - n_keep:],
            jnp.concatenate([cache_nsa_win[l], nsa_w_s.reshape(BS, 1, 2, DK)], axis=1)[:, 1:],
            ssd_p, ssd_s,
            ssd_xbc.reshape(BP, T, SSD_CONV_DIM)[:, T - (SSD_CONV - 1):],
            jnp.concatenate([state_ssd_conv[l], ssd_xbc_s.reshape(BS, 1, SSD_CONV_DIM)], axis=1)[:, 1:],
            hgrn_p, hgrn_s)
        for acc, o in zip(outs, layer_out):
            acc.append(o)

    stacked = tuple(jnp.stack(o) for o in outs)
    return (xp.reshape(BP, T, D), xs.reshape(BS, 1, D)) + stacked
```

```python
import functools
import math

import numpy as np
import jax
import jax.numpy as jnp
from jax import lax
from jax.experimental import pallas as pl
from jax.experimental.pallas import tpu as pltpu

F32 = jnp.float32
BF16 = jnp.bfloat16
I32 = jnp.int32
HIGHEST = lax.Precision.HIGHEST

EPS = 1e-6
NEG = -1e30
LB_FLOOR = 1e-20
ROPE_THETA = 10000.0

D_MODEL = 1024
D_FF = 2816
HEADS = 4
DK = 64
BRANCH = HEADS * DK
NSA_BLOCK = 64
NSA_TOPK = 16
NSA_WINDOW = 512
NSA_FORCED_BONUS = float(HEADS + 1)
MLA_Q_LORA = 256
MLA_KV_LORA = 128
MLA_NOPE = 64
MLA_ROPE = 32
MLA_V = 64
MLA_CACHE = MLA_KV_LORA + MLA_ROPE
SSD_GROUPS = 2
SSD_CONV = 4
SSD_CONV_DIM = BRANCH + 2 * SSD_GROUPS * DK
CHUNK = 128
QB = 128

LANES = 128
SUBLANES = 8
VMEM_LIMIT_BYTES = 56 * 1024 * 1024

_SEG = {}
_off = 0
for _name, _w in (("nsa_q", 256), ("nsa_pg", 256), ("nsa_w", 128), ("nsa_g", 128),
                  ("mla_cq", 256), ("mla_ckv", 128), ("mla_kpe", 128), ("mla_kpr", 128),
                  ("ssd_dt", 128), ("ssd_z", 256), ("ssd_xbc", 512), ("hg", 1024)):
    _SEG[_name] = (_off, _off + _w)
    _off += _w
PACKED_WIDTH = _off


def _cparams(sem):
    return pltpu.CompilerParams(dimension_semantics=sem, vmem_limit_bytes=VMEM_LIMIT_BYTES)


def _rms(x, g):
    return x * lax.rsqrt(jnp.mean(x * x, axis=-1, keepdims=True) + EPS) * g


def _dot(a, b):
    return jnp.dot(a.astype(BF16), b.astype(BF16), preferred_element_type=F32)


def _dot_nt(a, b):
    return lax.dot_general(a.astype(BF16), b.astype(BF16), (((1,), (1,)), ((), ())),
                           preferred_element_type=F32)


def _dot_rows_split(a, b):
    half = a.shape[0] // 2
    return jnp.concatenate([_dot(a[:half], b), _dot(a[half:], b)], axis=0)


def _dot_nt_rows_split(a, b):
    half = a.shape[0] // 2
    return jnp.concatenate([_dot_nt(a[:half], b), _dot_nt(a[half:], b)], axis=0)


def _dot_f32(a, b):
    return jnp.dot(a, b, precision=HIGHEST, preferred_element_type=F32)


def _dot_nt_f32(a, b):
    return lax.dot_general(a, b, (((1,), (1,)), ((), ())), precision=HIGHEST,
                           preferred_element_type=F32)


def _sigmoid(x):
    return 1.0 / (1.0 + jnp.exp(-x))


def _silu(x):
    return x * _sigmoid(x)


def _softplus(x):
    return jnp.maximum(x, 0.0) + jnp.log1p(jnp.exp(-jnp.abs(x)))


def _masked_softmax(s, mask):
    s = jnp.where(mask, s, NEG)
    m = jnp.max(s, axis=-1, keepdims=True)
    e = jnp.exp(s - m)
    r = 1.0 / jnp.sum(e, axis=-1, keepdims=True)
    return jnp.where(mask, e * r, 0.0)


def _unnormalised_softmax(s, mask):
    s = jnp.where(mask, s, NEG)
    e = jnp.exp(s - jnp.max(s, axis=-1, keepdims=True))
    return e, 1.0 / jnp.sum(e, axis=-1, keepdims=True)


def _row_tile(rows, cap=864):
    best = SUBLANES
    for t in range(SUBLANES, min(rows, cap) + 1, SUBLANES):
        if rows % t == 0:
            best = t
    assert rows % best == 0
    return best


def _const_spec(shape):
    nd = len(shape)
    return pl.BlockSpec(shape, lambda *_: (0,) * nd)


MXU_TILE = 256
FFN_BOUNDS = (0, 6 * MXU_TILE, D_FF)


def _ffn_kernel(x_ref, ga_ref, gb_ref, wi_ref, wo_ref, o_ref):
    x = x_ref[...]
    h = _rms(x, ga_ref[...]).astype(BF16)
    acc = None
    for a, b in zip(FFN_BOUNDS[:-1], FFN_BOUNDS[1:]):
        gate = jnp.dot(h, wi_ref[:, a:b], preferred_element_type=F32)
        up = jnp.dot(h, wi_ref[:, D_FF + a:D_FF + b], preferred_element_type=F32)
        part = _dot(_silu(gate) * up, wo_ref[a:b, :])
        acc = part if acc is None else acc + part
    o_ref[...] = x + 0.5 * _rms(acc, gb_ref[...])


def _resident_spec(shape):
    nd = len(shape)
    return pl.BlockSpec(shape, lambda *_: (0,) * nd, pipeline_mode=pl.Buffered(1))


def _ffn(x, ga, gb, wi, wo):
    rows = x.shape[0]
    tm = _row_tile(rows, cap=512)
    assert all(b % MXU_TILE == 0 for b in FFN_BOUNDS)
    return pl.pallas_call(
        _ffn_kernel,
        grid=(rows // tm,),
        in_specs=[
            pl.BlockSpec((tm, D_MODEL), lambda i: (i, 0)),
            _const_spec((1, D_MODEL)),
            _const_spec((1, D_MODEL)),
            _resident_spec(wi.shape),
            _resident_spec(wo.shape),
        ],
        out_specs=pl.BlockSpec((tm, D_MODEL), lambda i: (i, 0)),
        out_shape=jax.ShapeDtypeStruct((rows, D_MODEL), F32),
        compiler_params=_cparams(("parallel",)),
        name="ffn",
    )(x, ga, gb, wi, wo)


def _inproj_kernel(x_ref, g_ref, cs_ref, w_ref, mqn_ref, mkvn_ref, wuq_ref, wukn_ref,
                   q_o, pg_o, w_o, g_o, qlat_o, qpe_o, mla_o, z_o, xbc_o, dt_o, hg_o):
    hn = _rms(x_ref[...], g_ref[...]).astype(BF16)

    def seg(first, last=None):
        a, b = _SEG[first][0], _SEG[last or first][1]
        return jnp.dot(hn, w_ref[:, a:b], preferred_element_type=F32)

    q_o[...] = seg("nsa_q") * (DK ** -0.5)
    pg_o[...] = seg("nsa_pg")
    w_g = seg("nsa_w", "nsa_g")
    w_o[...] = w_g[:, 0:LANES]
    g_o[...] = _sigmoid(w_g[:, LANES:2 * LANES])
    z_o[...] = seg("ssd_z")
    xbc_o[...] = seg("ssd_xbc")
    hg_o[...] = seg("hg")
    ckv_kpe = seg("mla_ckv", "mla_kpe")
    kpr_dt = seg("mla_kpr", "ssd_dt")
    dt_o[...] = kpr_dt[:, LANES:2 * LANES]

    cos_p = cs_ref[:, 0:LANES]
    sin_p = cs_ref[:, LANES:2 * LANES]
    cqn = _rms(seg("mla_cq"), mqn_ref[...])
    qf = _dot(cqn, wuq_ref[...])
    nope_w = HEADS * MLA_NOPE
    qlat_o[...] = _dot(qf[:, 0:nope_w], wukn_ref[...])
    for h in range(HEADS):
        pe = qf[:, nope_w + LANES * h: nope_w + LANES * (h + 1)]
        pr = qf[:, nope_w + LANES * (HEADS + h): nope_w + LANES * (HEADS + h + 1)]
        qpe_o[:, LANES * h:LANES * (h + 1)] = pe * cos_p + pr * sin_p
    mla_o[:, 0:MLA_KV_LORA] = _rms(ckv_kpe[:, 0:LANES], mkvn_ref[...])
    kpe = ckv_kpe[:, LANES:2 * LANES] * cos_p + kpr_dt[:, 0:LANES] * sin_p
    mla_o[:, MLA_KV_LORA:MLA_CACHE] = kpe[:, 0:MLA_ROPE]


_INPROJ_OUT_WIDTHS = (256, 256, 128, 128, 512, 512, MLA_CACHE, 256, 512, 128, 1024)


def _inproj(x, g, cs, w, mqn, mkvn, wuq, wukn):
    rows = x.shape[0]
    tm = _row_tile(rows)
    row_spec = lambda c: pl.BlockSpec((tm, c), lambda i: (i, 0))
    return pl.pallas_call(
        _inproj_kernel,
        grid=(rows // tm,),
        in_specs=[row_spec(D_MODEL), _const_spec((1, D_MODEL)), row_spec(2 * LANES),
                  _const_spec(w.shape), _const_spec(mqn.shape), _const_spec(mkvn.shape),
                  _const_spec(wuq.shape), _const_spec(wukn.shape)],
        out_specs=[row_spec(c) for c in _INPROJ_OUT_WIDTHS],
        out_shape=[jax.ShapeDtypeStruct((rows, c), F32) for c in _INPROJ_OUT_WIDTHS],
        compiler_params=_cparams(("parallel",)),
        name="inproj",
    )(x, g, cs, w, mqn, mkvn, wuq, wukn)


def _merge_kernel(x_ref, b0_ref, b1_ref, b2_ref, b3_ref, g2_ref, g3_ref, wg_ref, wbr_ref, wout_ref,
                  o_ref):
    x = x_ref[...]
    hn = _rms(x, g2_ref[...]).astype(BF16)
    acc = None
    for k, b_ref in enumerate((b0_ref, b1_ref, b2_ref, b3_ref)):
        gate = _sigmoid(jnp.dot(hn, wg_ref[:, D_MODEL * k:D_MODEL * (k + 1)],
                                preferred_element_type=F32))
        u = _dot(b_ref[...], wbr_ref[k])
        acc = gate * u if acc is None else acc + gate * u
    out = _dot(acc, wout_ref[...])
    o_ref[...] = x + _rms(out, g3_ref[...])


def _merge(x, branches, g2, g3, wgate, wbr, wout):
    rows = x.shape[0]
    tm = _row_tile(rows, cap=512)
    row_spec = lambda c: pl.BlockSpec((tm, c), lambda i: (i, 0))
    return pl.pallas_call(
        _merge_kernel,
        grid=(rows // tm,),
        in_specs=[row_spec(D_MODEL)] + [row_spec(BRANCH)] * 4 +
                 [_const_spec((1, D_MODEL)), _const_spec((1, D_MODEL)),
                  _const_spec(wgate.shape), _const_spec(wbr.shape), _const_spec(wout.shape)],
        out_specs=row_spec(D_MODEL),
        out_shape=jax.ShapeDtypeStruct((rows, D_MODEL), F32),
        compiler_params=_cparams(("parallel",)),
        name="merge",
    )(x, *branches, g2, g3, wgate, wbr, wout)


KEY_STEP = 256


def _key_extents(T):
    his = sorted({min(T, KEY_STEP * (c + 1)) for c in range(-(-T // KEY_STEP))})
    return list(zip([0] + his[:-1], his))


def _stack_heads(x, width, take):
    return jnp.concatenate([x[:, width * h: width * h + take] for h in range(HEADS)], axis=0)


def _nsa_prompt_kernel(q_ref, pg_ref, wk_ref, g_ref, e_ref, o_ref, *, T, band):
    nb = T // NSA_BLOCK
    k_top = min(NSA_TOPK, nb)
    q0 = pl.program_id(1) * QB
    q = q_ref[...]
    zeros = jnp.zeros((QB, DK), F32)
    q_rows = jnp.concatenate(
        [jnp.concatenate([q[:, DK * h:DK * (h + 1)], zeros], axis=1) for h in range(HEADS)],
        axis=0).astype(BF16)
    pos = q0 + lax.broadcasted_iota(I32, (QB, 1), 0)
    lane = lax.broadcasted_iota(I32, (QB, LANES), 1)

    def per_head(fn):
        return jnp.concatenate([fn(h) for h in range(HEADS)], axis=0)

    def attend_values(s, mask, kv):
        parts = [_unnormalised_softmax(s[QB * h:QB * (h + 1)], mask) for h in range(HEADS)]
        e = jnp.concatenate([p[0] for p in parts], axis=0)
        r = jnp.concatenate([p[1] for p in parts], axis=0)
        return _dot_rows_split(e, kv) * r

    cm = jnp.sum(pg_ref[:, 0:LANES].reshape(nb, NSA_BLOCK, LANES), axis=1) * (1.0 / NSA_BLOCK)
    if nb < LANES:
        cm = jnp.concatenate([cm, jnp.zeros((LANES - nb, LANES), F32)], axis=0)
    cm = cm.astype(BF16)
    s_c = _dot_nt_rows_split(q_rows, cm)
    cmask = (lane + 1) * NSA_BLOCK - 1 <= pos
    p_c = per_head(lambda h: _masked_softmax(s_c[QB * h:QB * (h + 1)], cmask))
    o_c = _dot_rows_split(p_c, cm)

    cur = pos // NSA_BLOCK
    imp = p_c[0:QB] + p_c[QB:2 * QB] + p_c[2 * QB:3 * QB] + p_c[3 * QB:4 * QB]
    forced = (lane == 0) | (lane == cur) | (lane == cur - 1)
    imp = imp + jnp.where(forced, NSA_FORCED_BONUS, 0.0)
    imp = jnp.where(lane > cur, -1.0, imp)
    rank = jnp.zeros((QB, LANES), F32)
    for n in range(nb):
        c = imp[:, n:n + 1]
        beats = (c > imp) | ((c == imp) & (lane > n))
        rank = rank + jnp.where(beats, 1.0, 0.0)
    sel = jnp.where(rank < float(k_top), 1.0, 0.0)

    def selected(tk):
        in_sel = jnp.dot(sel.astype(BF16), e_ref[:, 0:tk], preferred_element_type=F32)
        kpos = lax.broadcasted_iota(I32, (QB, tk), 1)
        smask = (in_sel > 0.5) & (kpos <= pos)
        kv_s = pg_ref[0:tk, LANES:2 * LANES].astype(BF16)
        s_s = _dot_nt(q_rows, kv_s)
        return attend_values(s_s, smask, kv_s)

    start = pl.multiple_of(jnp.clip(q0 - NSA_WINDOW, 0, T - band), LANES)
    kv_w = wk_ref[pl.ds(start, band), :].astype(BF16)
    kposw = start + lax.broadcasted_iota(I32, (QB, band), 1)
    wmask = (kposw <= pos) & (kposw >= pos - NSA_WINDOW)
    o_w = attend_values(_dot_nt(q_rows, kv_w), wmask, kv_w)

    g = g_ref[...]

    def finish(o_s):
        outs = []
        for h in range(HEADS):
            r = slice(QB * h, QB * (h + 1))
            comb = (g[:, 3 * h:3 * h + 1] * o_c[r] + g[:, 3 * h + 1:3 * h + 2] * o_s[r]
                    + g[:, 3 * h + 2:3 * h + 3] * o_w[r])
            outs.append(comb[:, DK:2 * DK])
        o_ref[...] = jnp.concatenate(outs, axis=1)

    for lo, tk in _key_extents(T):
        @pl.when((q0 + QB > lo) & (q0 + QB <= tk))
        def _(tk=tk):
            finish(selected(tk))


def _nsa_prompt(q, pg, wkv, g, B, T):
    assert T % QB == 0 and T // NSA_BLOCK <= LANES
    rows = B * T
    nq = T // QB
    band = min(NSA_WINDOW + QB, T)
    nb = T // NSA_BLOCK
    expand = (np.arange(LANES)[:, None] == (np.arange(T)[None, :] // NSA_BLOCK)) & (np.arange(LANES)[:, None] < nb)
    expand = jnp.asarray(expand, BF16)
    return pl.pallas_call(
        functools.partial(_nsa_prompt_kernel, T=T, band=band),
        grid=(B, nq),
        in_specs=[pl.BlockSpec((QB, BRANCH), lambda b, i: (b * nq + i, 0)),
                  pl.BlockSpec((T, 256), lambda b, i: (b, 0)),
                  pl.BlockSpec((T, LANES), lambda b, i: (b, 0)),
                  pl.BlockSpec((QB, LANES), lambda b, i: (b * nq + i, 0)),
                  _const_spec((LANES, T))],
        out_specs=pl.BlockSpec((QB, BRANCH), lambda b, i: (b * nq + i, 0)),
        out_shape=jax.ShapeDtypeStruct((rows, BRANCH), F32),
        compiler_params=_cparams(("parallel", "parallel")),
        name="nsa_prompt",
    )(q, pg, wkv, g, expand)


def _mla_prompt_kernel(qlat_ref, qpe_ref, kv_ref, wv_ref, o_ref, *, T):
    q0 = pl.program_id(1) * QB
    q_cat = jnp.concatenate([_stack_heads(qlat_ref[...], LANES, LANES),
                             _stack_heads(qpe_ref[...], LANES, MLA_ROPE)], axis=1).astype(BF16)
    scale = (MLA_NOPE + MLA_ROPE) ** -0.5
    pos = q0 + lax.broadcasted_iota(I32, (QB, 1), 0)

    def attend(tk):
        kv = kv_ref[0:tk, :].astype(BF16)
        ckv = kv[:, 0:MLA_KV_LORA]
        s = _dot_nt(q_cat, kv) * scale
        mask = lax.broadcasted_iota(I32, (QB, tk), 1) <= pos
        parts = [_unnormalised_softmax(s[QB * h:QB * (h + 1)], mask) for h in range(HEADS)]
        e = jnp.concatenate([p[0] for p in parts], axis=0)
        r = jnp.concatenate([p[1] for p in parts], axis=0)
        o_lat = _dot_rows_split(e, ckv) * r
        o_cat = jnp.concatenate([o_lat[QB * h:QB * (h + 1)] for h in range(HEADS)], axis=1)
        o_ref[...] = _dot_rows_split(o_cat, wv_ref[...])

    for lo, tk in _key_extents(T):
        @pl.when((q0 + QB > lo) & (q0 + QB <= tk))
        def _(tk=tk):
            attend(tk)


def _mla_prompt(qlat, qpe, mla_new, wv, B, T):
    nq = T // QB
    rows = B * T
    return pl.pallas_call(
        functools.partial(_mla_prompt_kernel, T=T),
        grid=(B, nq),
        in_specs=[pl.BlockSpec((QB, 512), lambda b, i: (b * nq + i, 0)),
                  pl.BlockSpec((QB, 512), lambda b, i: (b * nq + i, 0)),
                  pl.BlockSpec((T, MLA_CACHE), lambda b, i: (b, 0)),
                  _const_spec(wv.shape)],
        out_specs=pl.BlockSpec((QB, BRANCH), lambda b, i: (b * nq + i, 0)),
        out_shape=jax.ShapeDtypeStruct((rows, BRANCH), F32),
        compiler_params=_cparams(("parallel", "parallel")),
        name="mla_prompt",
    )(qlat, qpe, mla_new, wv)


def _head_lanes(x_cols, shape):
    lane = lax.broadcasted_iota(I32, shape, 1)
    out = jnp.broadcast_to(x_cols[:, HEADS - 1:HEADS], shape)
    for h in range(HEADS - 2, -1, -1):
        out = jnp.where(lane < DK * (h + 1), jnp.broadcast_to(x_cols[:, h:h + 1], shape), out)
    return out


def _ssd_prompt_kernel(xbc_ref, dt_ref, z_ref, cw_ref, cb_ref, dtb_ref, alog_ref, dsk_ref, sn_ref,
                       o_ref, st_ref, buf_ref, s_ref):
    c = pl.program_id(1)
    C = CHUNK

    @pl.when(c == 0)
    def _():
        buf_ref[0:SUBLANES, :] = jnp.zeros((SUBLANES, SSD_CONV_DIM), F32)
        s_ref[...] = jnp.zeros_like(s_ref)

    x = xbc_ref[...]
    buf_ref[SUBLANES:SUBLANES + C, :] = x
    conv = cb_ref[...] + cw_ref[SSD_CONV - 1:SSD_CONV, :] * x
    for j in range(SSD_CONV - 1):
        conv = conv + cw_ref[j:j + 1, :] * buf_ref[pl.ds(SUBLANES - (SSD_CONV - 1) + j, C), :]
    buf_ref[0:SUBLANES, :] = x[C - SUBLANES:C, :]
    xa = _silu(conv)
    xs = xa[:, 0:BRANCH]
    dt = _softplus(dt_ref[...] + dtb_ref[...])
    a = dt * (-jnp.exp(alog_ref[...]))
    row = lax.broadcasted_iota(I32, (C, C), 0)
    col = lax.broadcasted_iota(I32, (C, C), 1)
    tril = row >= col
    tri = jnp.where(tril, 1.0, 0.0)
    cum_c = _dot_f32(tri, a)
    cum_r = _dot_nt_f32(a.T, tri)
    xdt = xs * _head_lanes(dt, (C, BRANCH))
    xdt_t = xdt.T
    dsk = dsk_ref[...]
    ys = []
    for h in range(HEADS):
        g = h // (HEADS // SSD_GROUPS)
        bm = xa[:, BRANCH + DK * g:BRANCH + DK * (g + 1)]
        cm = xa[:, BRANCH + DK * (SSD_GROUPS + g):BRANCH + DK * (SSD_GROUPS + g + 1)]
        ch = cum_c[:, h:h + 1]
        decay = jnp.exp(jnp.where(tril, ch - cum_r[h:h + 1, :], NEG))
        scores = _dot_nt(cm, bm) * decay
        xdt_h = xdt[:, DK * h:DK * (h + 1)]
        s_old = s_ref[h]
        y = _dot(scores, xdt_h) + _dot_nt(cm, s_old) * jnp.exp(ch)
        last = cum_c[C - 1:C, h:h + 1]
        w = jnp.exp(last - ch)
        s_ref[h] = s_old * jnp.exp(last) + _dot(xdt_t[DK * h:DK * (h + 1), :], bm * w)
        ys.append(y + dsk[:, h:h + 1] * xs[:, DK * h:DK * (h + 1)])
    y = jnp.concatenate(ys, axis=1)
    o_ref[...] = _rms(y * _silu(z_ref[...]), sn_ref[...])

    @pl.when(c == pl.num_programs(1) - 1)
    def _():
        st_ref[0] = s_ref[...]


def _ssd_prompt(xbc, dt, z, cw, cb, dtb, alog, dsk, sn, B, T):
    assert T % CHUNK == 0
    nc = T // CHUNK
    rows = B * T
    row_spec = lambda w: pl.BlockSpec((CHUNK, w), lambda b, c: (b * nc + c, 0))
    return pl.pallas_call(
        _ssd_prompt_kernel,
        grid=(B, nc),
        in_specs=[row_spec(SSD_CONV_DIM), row_spec(LANES), row_spec(BRANCH),
                  _const_spec(cw.shape), _const_spec(cb.shape), _const_spec(dtb.shape),
                  _const_spec(alog.shape), _const_spec(dsk.shape), _const_spec(sn.shape)],
        out_specs=[row_spec(BRANCH), pl.BlockSpec((1, HEADS, DK, DK), lambda b, c: (b, 0, 0, 0))],
        out_shape=[jax.ShapeDtypeStruct((rows, BRANCH), F32),
                   jax.ShapeDtypeStruct((B, HEADS, DK, DK), F32)],
        scratch_shapes=[pltpu.VMEM((SUBLANES + CHUNK, SSD_CONV_DIM), F32),
                        pltpu.VMEM((HEADS, DK, DK), F32)],
        compiler_params=_cparams(("parallel", "arbitrary")),
        name="ssd_prompt",
    )(xbc, dt, z, cw, cb, dtb, alog, dsk, sn)


def _hgrn_gates(fr, lb):
    log_sig = jnp.minimum(fr, 0.0) - jnp.log1p(jnp.exp(-jnp.abs(fr)))
    a = jnp.log(jnp.maximum(lb, LB_FLOOR))
    b = jnp.log1p(-lb) + log_sig
    log_f = jnp.maximum(a, b) + jnp.log1p(jnp.exp(-jnp.abs(a - b)))
    return log_f, (1.0 - lb) * _sigmoid(-fr)


def _block_reference_rows(G, m):
    C = G.shape[0]
    if m >= 4:
        parts = []
        for p in range(C // (2 * m)):
            r = p * 2 * m + m - 1
            parts.append(jnp.broadcast_to(G[r:r + 1, :], (2 * m, G.shape[1])))
        return parts[0] if len(parts) == 1 else jnp.concatenate(parts, axis=0)
    t = lax.broadcasted_iota(I32, G.shape, 0)
    if m == 1:
        return jnp.where((t & 1) == 1, pltpu.roll(G, 1, 0), G)
    r = t & 3
    return jnp.where(r == 0, pltpu.roll(G, C - 1, 0),
                     jnp.where(r == 1, G, jnp.where(r == 2, pltpu.roll(G, 1, 0), pltpu.roll(G, 2, 0))))


def _hgrn_prompt_kernel(hg_ref, lb_ref, hn_ref, seg_ref, o_ref, st_ref, s_ref):
    c = pl.program_id(1)
    C = CHUNK
    W = BRANCH

    @pl.when(c == 0)
    def _():
        s_ref[...] = jnp.zeros_like(s_ref)

    q = hg_ref[:, 0:W]
    v = hg_ref[:, 2 * W:3 * W]
    log_f, kin = _hgrn_gates(hg_ref[:, W:2 * W], lb_ref[...])
    row = lax.broadcasted_iota(I32, (C, C), 0)
    col = lax.broadcasted_iota(I32, (C, C), 1)
    G = _dot_f32(jnp.where(row >= col, 1.0, 0.0), log_f)
    lane_head = lax.broadcasted_iota(I32, (C, W), 1) // DK
    t_idx = lax.broadcasted_iota(I32, (C, W), 0)

    def stack_heads(a):
        return jnp.concatenate([jnp.where(lane_head == h, a, 0.0) for h in range(HEADS)],
                               axis=0).astype(BF16)

    t4 = lax.broadcasted_iota(I32, (HEADS * C, C), 0) & (C - 1)
    s4 = lax.broadcasted_iota(I32, (HEADS * C, C), 1)
    att = jnp.where(t4 == s4, _dot_nt_rows_split(stack_heads(q), kin), 0.0)
    m = C // 2
    while m >= 1:
        R = _block_reference_rows(G, m)
        upper = (t_idx & m) != 0
        A = jnp.where(upper, q * jnp.exp(jnp.minimum(G - R, 0.0)), 0.0)
        Bm = jnp.where(upper, 0.0, kin * jnp.exp(jnp.minimum(R - G, 0.0)))
        lm = int(math.log2(m))
        pair = ((t4 >> lm) ^ (s4 >> lm)) == 1
        att = att + jnp.where(pair & (t4 > s4), _dot_nt_rows_split(stack_heads(A), Bm), 0.0)
        m //= 2
    y4 = _dot_rows_split(att, v)
    y = jnp.zeros((C, W), F32)
    for h in range(HEADS):
        y = y + jnp.where(lane_head == h, y4[C * h:C * (h + 1)], 0.0)
    s_old = s_ref[...]
    y = y + _dot(q * jnp.exp(G), s_old)
    last = G[C - 1:C, :]
    ke_t = (kin * jnp.exp(last - G)).T
    G_t = G.T
    blk = (lax.broadcasted_iota(I32, (W, W), 0) // DK) == (lax.broadcasted_iota(I32, (W, W), 1) // DK)
    s_ref[...] = s_old * jnp.exp(G_t[:, C - 1:C]) + jnp.where(blk, _dot(ke_t, v), 0.0)
    ms = _dot_f32(y * y, seg_ref[...])
    o = y * lax.rsqrt(ms + EPS) * hn_ref[...]
    o_ref[...] = o * _silu(hg_ref[:, 3 * W:4 * W])

    @pl.when(c == pl.num_programs(1) - 1)
    def _():
        for h in range(HEADS):
            st_ref[0, h] = s_ref[DK * h:DK * (h + 1), DK * h:DK * (h + 1)]


def _hgrn_prompt(hg, lb, hn, B, T):
    nc = T // CHUNK
    rows = B * T
    seg = (np.arange(BRANCH)[:, None] // DK == np.arange(BRANCH)[None, :] // DK) / float(DK)
    seg = jnp.asarray(seg, F32)
    row_spec = lambda w: pl.BlockSpec((CHUNK, w), lambda b, c: (b * nc + c, 0))
    return pl.pallas_call(
        _hgrn_prompt_kernel,
        grid=(B, nc),
        in_specs=[row_spec(4 * BRANCH), _const_spec(lb.shape), _const_spec(hn.shape),
                  _const_spec(seg.shape)],
        out_specs=[row_spec(BRANCH), pl.BlockSpec((1, HEADS, DK, DK), lambda b, c: (b, 0, 0, 0))],
        out_shape=[jax.ShapeDtypeStruct((rows, BRANCH), F32),
                   jax.ShapeDtypeStruct((B, HEADS, DK, DK), F32)],
        scratch_shapes=[pltpu.VMEM((BRANCH, BRANCH), F32)],
        compiler_params=_cparams(("parallel", "arbitrary")),
        name="hgrn_prompt",
    )(hg, lb, hn, seg)


PAGES_PER_STEP = 32


def _mla_decode_kernel(pt_ref, q_ref, qt_ref, new_ref, *rest, n_pg):
    page_refs = rest[:n_pg]
    wv_ref, o_ref, qb_ref, m_ref, l_ref, acc_ref = rest[n_pg:]
    del pt_ref
    b = pl.program_id(0)
    j = pl.program_id(1)
    scale = (MLA_NOPE + MLA_ROPE) ** -0.5
    page = page_refs[0].shape[1]

    @pl.when(j == 0)
    def _():
        qt = qt_ref[0]
        for h in range(HEADS):
            qb_ref[h] = jnp.broadcast_to(qt[:, h:h + 1], (MLA_CACHE, LANES))
        m_ref[...] = jnp.full_like(m_ref, NEG)
        l_ref[...] = jnp.zeros_like(l_ref)
        acc_ref[...] = jnp.zeros_like(acc_ref)

    scores = [[] for _ in range(HEADS)]
    for h in range(HEADS):
        qb = qb_ref[h]
        for r in page_refs:
            scores[h].append(jnp.sum(r[...] * qb, axis=0, keepdims=True))
    s = jnp.concatenate([jnp.concatenate(scores[h], axis=1) for h in range(HEADS)], axis=0) * scale
    m_old = m_ref[0:HEADS, :]
    m_new = jnp.maximum(m_old, jnp.max(s, axis=-1, keepdims=True))
    alpha = jnp.exp(m_old - m_new)
    p = jnp.exp(s - m_new[:, 0:1])
    l_ref[0:HEADS, :] = alpha * l_ref[0:HEADS, :] + jnp.sum(p, axis=-1, keepdims=True)
    m_ref[0:HEADS, :] = m_new
    for h in range(HEADS):
        acc = alpha[h:h + 1, 0:1] * acc_ref[h]
        for t, r in enumerate(page_refs):
            acc = acc + r[0:MLA_KV_LORA, :] * p[h:h + 1, t * page:(t + 1) * page]
        acc_ref[h] = acc

    @pl.when(j == pl.num_programs(1) - 1)
    def _():
        new = new_ref[0]
        s_new = jnp.sum(q_ref[0] * new, axis=-1, keepdims=True) * scale
        ones = jnp.ones((SUBLANES, LANES), F32)
        outs = []
        for h in range(HEADS):
            m_old = m_ref[h:h + 1, :]
            m_fin = jnp.maximum(m_old, s_new[h:h + 1, :])
            alpha = jnp.exp(m_old - m_fin)
            p_n = jnp.exp(s_new[h:h + 1, :] - m_fin)
            l = alpha * l_ref[h:h + 1, :] + p_n
            lat = _dot_nt_f32(ones, acc_ref[h])[0:1, :]
            outs.append((alpha * lat + p_n * new[:, 0:MLA_KV_LORA]) / l)
        o_cat = jnp.concatenate(outs, axis=1)
        o = _dot(jnp.broadcast_to(o_cat, (SUBLANES, HEADS * MLA_KV_LORA)), wv_ref[...])
        o_ref[pl.ds(b, 1), :] = o[0:1, :]


def _mla_decode(layer, page_table, q8, qt, new_rows, cache_t, wv):
    BS, npages = page_table.shape
    page = cache_t.shape[3]
    n_pg = min(PAGES_PER_STEP, npages)
    assert npages % n_pg == 0 and page == LANES
    steps = npages // n_pg

    def page_spec(i):
        return pl.BlockSpec((None, None, MLA_CACHE, page),
                            lambda b, j, pt: (layer, pt[b, j * n_pg + i], 0, 0))

    grid_spec = pltpu.PrefetchScalarGridSpec(
        num_scalar_prefetch=1,
        grid=(BS, steps),
        in_specs=[pl.BlockSpec((1, SUBLANES, MLA_CACHE), lambda b, j, pt: (b, 0, 0)),
                  pl.BlockSpec((1, MLA_CACHE, HEADS), lambda b, j, pt: (b, 0, 0)),
                  pl.BlockSpec((1, 1, MLA_CACHE), lambda b, j, pt: (b, 0, 0))]
                 + [page_spec(i) for i in range(n_pg)]
                 + [pl.BlockSpec(wv.shape, lambda b, j, pt: (0, 0))],
        out_specs=pl.BlockSpec((BS, BRANCH), lambda b, j, pt: (0, 0)),
        scratch_shapes=[pltpu.VMEM((HEADS, MLA_CACHE, LANES), F32),
                        pltpu.VMEM((SUBLANES, LANES), F32), pltpu.VMEM((SUBLANES, LANES), F32),
                        pltpu.VMEM((HEADS, MLA_KV_LORA, LANES), F32)],
    )
    return pl.pallas_call(
        functools.partial(_mla_decode_kernel, n_pg=n_pg),
        grid_spec=grid_spec,
        out_shape=jax.ShapeDtypeStruct((BS, BRANCH), F32),
        compiler_params=_cparams(("arbitrary", "arbitrary")),
        name="mla_decode",
    )(page_table, q8, qt, new_rows, *([cache_t] * n_pg), wv)


def _nsa_cmp_decode_kernel(pt_ref, qt_ref, *rest, n_pg, past):
    page_refs = rest[:n_pg]
    oc_ref, imp_ref, qb_ref, ts_ref, vb_ref = rest[n_pg:]
    del pt_ref
    j = pl.program_id(1)
    steps = ts_ref.shape[0]
    page = page_refs[0].shape[1]
    nblk = 2 * LANES

    @pl.when(j == 0)
    def _():
        qt = qt_ref[0]
        for h in range(HEADS):
            qb_ref[h] = jnp.broadcast_to(qt[:, h:h + 1], (DK, LANES))

    for t, r in enumerate(page_refs):
        kt = r[0:DK, :]
        for h in range(HEADS):
            ts_ref[j, h:h + 1, t * page:(t + 1) * page] = jnp.sum(kt * qb_ref[h], axis=0, keepdims=True)
        vb_ref[j, :, t * page:(t + 1) * page] = r[DK:2 * DK, :]

    @pl.when(j == steps - 1)
    def _():
        blk = lax.broadcasted_iota(I32, (HEADS, nblk), 1)
        first = lax.broadcasted_iota(I32, (HEADS, page), 1) < NSA_BLOCK
        s = jnp.zeros((HEADS, nblk), F32)
        for jj in range(steps):
            ts = ts_ref[jj, 0:HEADS, :]
            for t in range(n_pg):
                x = ts[:, t * page:(t + 1) * page]
                b0 = 2 * (jj * n_pg + t)
                s0 = jnp.sum(jnp.where(first, x, 0.0), axis=1, keepdims=True)
                s1 = jnp.sum(jnp.where(first, 0.0, x), axis=1, keepdims=True)
                s = jnp.where(blk == b0, s0, jnp.where(blk == b0 + 1, s1, s))
        s = s * (1.0 / NSA_BLOCK)
        cmask = (blk + 1) * NSA_BLOCK - 1 <= past
        p = _masked_softmax(s, cmask)
        cur = past // NSA_BLOCK
        imp = jnp.sum(p, axis=0, keepdims=True)
        blk1 = blk[0:1]
        forced = (blk1 == 0) | (blk1 == cur - 1)
        imp = imp + jnp.where(forced, NSA_FORCED_BONUS, 0.0)
        imp_ref[0] = jnp.where(blk1 >= cur, -3.0, imp)

        pw = jnp.concatenate([p * (1.0 / NSA_BLOCK), jnp.zeros((SUBLANES - HEADS, nblk), F32)], axis=0)
        nb_step = 2 * n_pg
        expand = jnp.where(lax.broadcasted_iota(I32, (nb_step, n_pg * page), 0)
                           == lax.broadcasted_iota(I32, (nb_step, n_pg * page), 1) // NSA_BLOCK, 1.0, 0.0)
        accs = [jnp.zeros((DK, page), F32) for _ in range(HEADS)]
        for jj in range(steps):
            w = _dot_f32(pw[:, nb_step * jj:nb_step * (jj + 1)], expand)
            for t in range(n_pg):
                v = vb_ref[jj, :, t * page:(t + 1) * page]
                for h in range(HEADS):
                    accs[h] = accs[h] + v * w[h:h + 1, t * page:(t + 1) * page]
        ones = jnp.ones((SUBLANES, LANES), F32)
        oc_ref[0] = jnp.concatenate([_dot_nt_f32(ones, a)[0:1, :] for a in accs], axis=0)


def _nsa_cmp_decode(layer, page_table, qt, cache_t, past):
    BS, npages = page_table.shape
    page = cache_t.shape[3]
    n_pg = min(PAGES_PER_STEP, npages)
    assert page == 2 * NSA_BLOCK and page == LANES and 2 * npages <= 2 * LANES and npages % n_pg == 0
    steps = npages // n_pg

    def page_spec(i):
        return pl.BlockSpec((None, None, 2 * DK, page),
                            lambda b, j, pt: (layer, pt[b, j * n_pg + i], 0, 0))

    grid_spec = pltpu.PrefetchScalarGridSpec(
        num_scalar_prefetch=1,
        grid=(BS, steps),
        in_specs=[pl.BlockSpec((1, DK, HEADS), lambda b, j, pt: (b, 0, 0))]
                 + [page_spec(i) for i in range(n_pg)],
        out_specs=[pl.BlockSpec((1, HEADS, DK), lambda b, j, pt: (b, 0, 0)),
                   pl.BlockSpec((1, 1, 2 * LANES), lambda b, j, pt: (b, 0, 0))],
        scratch_shapes=[pltpu.VMEM((HEADS, DK, LANES), F32),
                        pltpu.VMEM((steps, SUBLANES, n_pg * page), F32),
                        pltpu.VMEM((steps, DK, n_pg * page), F32)],
    )
    return pl.pallas_call(
        functools.partial(_nsa_cmp_decode_kernel, n_pg=n_pg, past=past),
        grid_spec=grid_spec,
        out_shape=[jax.ShapeDtypeStruct((BS, HEADS, DK), F32),
                   jax.ShapeDtypeStruct((BS, 1, 2 * LANES), F32)],
        compiler_params=_cparams(("parallel", "arbitrary")),
        name="nsa_cmp_decode",
    )(page_table, qt, *([cache_t] * n_pg))


def _topk_kernel(imp_ref, idx_ref, *, past, k_top):
    BS = imp_ref.shape[0]
    cur = past // NSA_BLOCK
    blk = lax.broadcasted_iota(I32, (BS, 2 * LANES), 1).astype(F32)
    lane_x = lax.broadcasted_iota(I32, (BS, LANES), 1)
    imp = jnp.concatenate([imp_ref[...], jnp.where(lane_x == 0, NSA_FORCED_BONUS, -3.0)], axis=1)
    blk = jnp.concatenate([blk, (cur + lane_x).astype(F32)], axis=1)
    out = jnp.zeros((BS, LANES), F32)
    for k in range(k_top):
        m = jnp.max(imp, axis=-1, keepdims=True)
        pick = jnp.min(jnp.where(imp == m, blk, 1e9), axis=-1, keepdims=True)
        imp = jnp.where(blk == pick, -4.0, imp)
        out = jnp.where(lane_x == k, pick, out)
    idx_ref[...] = out.astype(I32)


def _topk(imp, past, k_top):
    BS = imp.shape[0]
    return pl.pallas_call(
        functools.partial(_topk_kernel, past=past, k_top=k_top),
        out_shape=jax.ShapeDtypeStruct((BS, LANES), I32),
        name="nsa_topk",
    )(imp)


def _nsa_sel_decode_kernel(pt_ref, idx_ref, q_ref, pgn_ref, wn_ref, win_ref, g_ref, oc_ref, *rest,
                           k_top, n_blocks):
    blk_refs = rest[:k_top]
    (o_ref,) = rest[k_top:]
    del pt_ref
    b = pl.program_id(0)
    q = q_ref[0]
    page = blk_refs[0].shape[1]

    def attend(kt, vt, s_mask, k_new, v_new, new_on):
        s = _dot(q, kt)
        if s_mask is not None:
            s = jnp.where(s_mask, s, NEG)
        s_n = jnp.where(new_on, jnp.sum(q * k_new, axis=-1, keepdims=True), NEG)
        m = jnp.maximum(jnp.max(s, axis=-1, keepdims=True), s_n)
        e = jnp.exp(s - m)
        if s_mask is not None:
            e = jnp.where(s_mask, e, 0.0)
        e_n = jnp.where(new_on, jnp.exp(s_n - m), 0.0)
        l = jnp.sum(e, axis=-1, keepdims=True) + e_n
        return (_dot_nt(e, vt) + e_n * v_new) / l

    kt = jnp.concatenate([r[0:DK, :] for r in blk_refs], axis=1).astype(BF16)
    vt = jnp.concatenate([r[DK:2 * DK, :] for r in blk_refs], axis=1).astype(BF16)
    lane = lax.broadcasted_iota(I32, (1, k_top * page), 1)
    slot = lane // page
    half = (lane % page) // NSA_BLOCK
    valid = jnp.zeros((1, k_top * page), I32)
    has_new = jnp.zeros((), jnp.bool_)
    for k in range(k_top):
        ik = idx_ref[b, k]
        ok = jnp.where(half == (ik & 1), (ik < n_blocks).astype(I32), 0)
        valid = jnp.where(slot == k, ok, valid)
        has_new = has_new | (ik == n_blocks)
    pgn = pgn_ref[0]
    o_s = attend(kt, vt, valid > 0, pgn[:, 2 * DK:3 * DK], pgn[:, 3 * DK:4 * DK], has_new)
    wn = wn_ref[0]
    o_w = attend(win_ref[0:DK, :].astype(BF16), win_ref[DK:2 * DK, :].astype(BF16), None,
                 wn[:, 0:DK], wn[:, DK:2 * DK], True)
    o_c = oc_ref[0]
    g = g_ref[0]
    outs = []
    for h in range(HEADS):
        outs.append(g[:, 3 * h:3 * h + 1] * o_c[h:h + 1] + g[:, 3 * h + 1:3 * h + 2] * o_s[h:h + 1]
                    + g[:, 3 * h + 2:3 * h + 3] * o_w[h:h + 1])
    o_ref[pl.ds(b, 1), :] = jnp.concatenate(outs, axis=1)


def _nsa_sel_decode(layer, page_table, idx, q8, pg_new, w_new, win_t, g_new, o_c, cache_t, k_top):
    BS, npages = page_table.shape
    n_blocks = 2 * npages
    page = cache_t.shape[3]
    nwin = win_t.shape[3]
    blk = jnp.minimum(idx[:, :k_top], n_blocks - 1)
    phys = jnp.take_along_axis(page_table, blk // 2, axis=1)

    def blk_spec(k):
        return pl.BlockSpec((None, None, 2 * DK, page), lambda b, ph, ix: (layer, ph[b, k], 1, 0))

    row3 = lambda w: pl.BlockSpec((1, 1, w), lambda b, pt, ix: (b, 0, 0))
    grid_spec = pltpu.PrefetchScalarGridSpec(
        num_scalar_prefetch=2,
        grid=(BS,),
        in_specs=[pl.BlockSpec((1, SUBLANES, DK), lambda b, pt, ix: (b, 0, 0)),
                  row3(256), row3(LANES),
                  pl.BlockSpec((None, None, 2 * DK, nwin), lambda b, pt, ix: (layer, b, 0, 0)),
                  row3(LANES),
                  pl.BlockSpec((1, HEADS, DK), lambda b, pt, ix: (b, 0, 0))]
                 + [blk_spec(k) for k in range(k_top)],
        out_specs=pl.BlockSpec((BS, BRANCH), lambda b, pt, ix: (0, 0)),
    )
    return pl.pallas_call(
        functools.partial(_nsa_sel_decode_kernel, k_top=k_top, n_blocks=n_blocks),
        grid_spec=grid_spec,
        out_shape=jax.ShapeDtypeStruct((BS, BRANCH), F32),
        compiler_params=_cparams(("arbitrary",)),
        name="nsa_sel_decode",
    )(phys, idx, q8, pg_new, w_new, win_t, g_new, o_c, *([cache_t] * k_top))


def _column(row, eye):
    return jnp.sum(jnp.where(eye, jnp.broadcast_to(row, eye.shape), 0.0), axis=1, keepdims=True)


def _rec_decode_kernel(xbc_ref, cbuf_ref, dt_ref, z_ref, hg_ref, s_ref, hs_ref,
                       cw_ref, cb_ref, dtb_ref, alog_ref, dsk_ref, sn_ref, lb_ref, hn_ref,
                       os_ref, oh_ref, sn_out, hs_out):
    b = pl.program_id(0)
    eye = lax.broadcasted_iota(I32, (DK, DK), 0) == lax.broadcasted_iota(I32, (DK, DK), 1)

    cbuf = cbuf_ref[...]
    xn = xbc_ref[0]
    conv = cb_ref[...] + cw_ref[SSD_CONV - 1:SSD_CONV, :] * xn
    for j in range(SSD_CONV - 1):
        conv = conv + cw_ref[j:j + 1, :] * cbuf[j:j + 1, :]
    xa = _silu(conv)
    dt = _softplus(dt_ref[0] + dtb_ref[...])
    ea = jnp.exp(dt * (-jnp.exp(alog_ref[...])))
    dsk = dsk_ref[...]
    ys = []
    for h in range(HEADS):
        g = h // (HEADS // SSD_GROUPS)
        xs = xa[:, DK * h:DK * (h + 1)]
        bm = xa[:, BRANCH + DK * g:BRANCH + DK * (g + 1)]
        cm = xa[:, BRANCH + DK * (SSD_GROUPS + g):BRANCH + DK * (SSD_GROUPS + g + 1)]
        xdt_col = _column(xs * dt[:, h:h + 1], eye)
        s_old = s_ref[h]
        ea_h = ea[:, h:h + 1]
        y_col = (jnp.sum(cm * bm, axis=-1, keepdims=True) * xdt_col
                 + jnp.sum(s_old * cm, axis=-1, keepdims=True) * ea_h)
        sn_out[0, h] = s_old * ea_h + xdt_col * bm
        y_row = jnp.sum(jnp.where(eye, jnp.broadcast_to(y_col, (DK, DK)), 0.0), axis=0, keepdims=True)
        ys.append(y_row + dsk[:, h:h + 1] * xs)
    y = jnp.concatenate(ys, axis=1)
    os_ref[pl.ds(b, 1), :] = _rms(y * _silu(z_ref[0]), sn_ref[...])

    W = BRANCH
    hg = hg_ref[0]
    q = hg[:, 0:W]
    v = hg[:, 2 * W:3 * W]
    log_f, kin = _hgrn_gates(hg[:, W:2 * W], lb_ref[...])
    ef = jnp.exp(log_f)
    hn = hn_ref[...]
    outs = []
    for h in range(HEADS):
        r = slice(DK * h, DK * (h + 1))
        s_old = hs_ref[h]
        att = jnp.sum(q[:, r] * kin[:, r], axis=-1, keepdims=True)
        y = att * v[:, r] + jnp.sum(s_old * _column(q[:, r] * ef[:, r], eye), axis=0, keepdims=True)
        hs_out[0, h] = s_old * _column(ef[:, r], eye) + _column(kin[:, r], eye) * v[:, r]
        outs.append(_rms(y, hn[:, r]))
    o = jnp.concatenate(outs, axis=1)
    oh_ref[pl.ds(b, 1), :] = o * _silu(hg[:, 3 * W:4 * W])


def _rec_decode(layer, xbc_new, conv_state, dt_new, z_new, hg_new, state_ssd, state_hgrn,
                cw, cb, dtb, alog, dsk, sn, lb, hn):
    BS = xbc_new.shape[0]
    row3 = lambda w: pl.BlockSpec((1, 1, w), lambda b: (b, 0, 0))
    st_in = pl.BlockSpec((None, None, HEADS, DK, DK), lambda b: (layer, b, 0, 0, 0))
    st_out = pl.BlockSpec((1, HEADS, DK, DK), lambda b: (b, 0, 0, 0))
    o_spec = pl.BlockSpec((BS, BRANCH), lambda b: (0, 0))
    consts = (cw, cb, dtb, alog, dsk, sn, lb, hn)
    return pl.pallas_call(
        _rec_decode_kernel,
        grid=(BS,),
        in_specs=[row3(SSD_CONV_DIM),
                  pl.BlockSpec((None, None, SSD_CONV - 1, SSD_CONV_DIM), lambda b: (layer, b, 0, 0)),
                  row3(LANES), row3(BRANCH), row3(4 * BRANCH), st_in, st_in]
                 + [_const_spec(c.shape) for c in consts],
        out_specs=[o_spec, o_spec, st_out, st_out],
        out_shape=[jax.ShapeDtypeStruct((BS, BRANCH), F32), jax.ShapeDtypeStruct((BS, BRANCH), F32),
                   jax.ShapeDtypeStruct((BS, HEADS, DK, DK), F32),
                   jax.ShapeDtypeStruct((BS, HEADS, DK, DK), F32)],
        compiler_params=_cparams(("arbitrary",)),
        name="rec_decode",
    )(xbc_new, conv_state, dt_new, z_new, hg_new, state_ssd, state_hgrn, *consts)


def _rot_cols(w):
    half = MLA_ROPE // 2
    return jnp.concatenate([-w[..., half:], w[..., :half]], axis=-1)


def _pad_cols(w, width):
    return jnp.pad(w, [(0, 0)] * (w.ndim - 1) + [(0, width - w.shape[-1])])


def _pack_w_in(w):
    sizes = (256, 384, 12, MLA_Q_LORA, MLA_KV_LORA, MLA_ROPE, 256, SSD_CONV_DIM, HEADS, 256, 256, 256, 256)
    offs = np.cumsum((0,) + sizes)
    (nsa_q, nsa_kv, nsa_g, cq, ckv, kpe, z, xbc, dt, hq, hf, hi, hgate) = (
        w[:, offs[i]:offs[i + 1]] for i in range(len(sizes)))
    gate = w[:, offs[-1]:]
    packed = jnp.concatenate([
        nsa_q, nsa_kv[:, :256], nsa_kv[:, 256:], _pad_cols(nsa_g, LANES), cq, ckv,
        _pad_cols(kpe, LANES), _pad_cols(_rot_cols(kpe), LANES), _pad_cols(dt, LANES), z, xbc,
        hq, hf, hi, hgate], axis=1)
    assert packed.shape[1] == PACKED_WIDTH
    return packed.astype(BF16), gate.astype(BF16)


def _pack_mla(wuq, wukv):
    nope = wuq[:, :, :MLA_NOPE].reshape(MLA_Q_LORA, HEADS * MLA_NOPE)
    pe = wuq[:, :, MLA_NOPE:]
    pe_p = _pad_cols(pe, LANES).reshape(MLA_Q_LORA, HEADS * LANES)
    pr_p = _pad_cols(_rot_cols(pe), LANES).reshape(MLA_Q_LORA, HEADS * LANES)
    wuq_p = jnp.concatenate([nope, pe_p, pr_p], axis=1).astype(BF16)
    eye = jnp.eye(HEADS, dtype=wukv.dtype)
    wukn = jnp.einsum('rhn,hg->hngr', wukv[:, :, :MLA_NOPE], eye).reshape(HEADS * MLA_NOPE, HEADS * MLA_KV_LORA)
    wv = jnp.einsum('rhv,hg->hrgv', wukv[:, :, MLA_NOPE:], eye).reshape(HEADS * MLA_KV_LORA, HEADS * MLA_V)
    return wuq_p, wukn.astype(BF16), wv.astype(BF16)


def _rope_table(positions):
    half = MLA_ROPE // 2
    freq = ROPE_THETA ** (-jnp.arange(half, dtype=F32) / half)
    ang = positions.astype(F32)[:, None] * freq
    cos = jnp.concatenate([jnp.cos(ang), jnp.cos(ang)], axis=1)
    sin = jnp.concatenate([jnp.sin(ang), jnp.sin(ang)], axis=1)
    return jnp.concatenate([_pad_cols(cos, LANES), _pad_cols(sin, LANES)], axis=1)


def _pad_row(v, width=LANES):
    return _pad_cols(v.reshape(1, -1).astype(F32), width)


def kernel(x_prompt, x_sample, cache_nsa_kv, cache_mla, cache_nsa_win, state_ssd, state_ssd_conv, state_hgrn, page_table, norm_g, ffn_w_in, ffn_w_out, w_in, mla_q_norm, mla_kv_norm, mla_w_uq, mla_w_ukv, ssd_conv_w, ssd_conv_b, ssd_dt_bias, ssd_a_log, ssd_d, ssd_norm, hg_lb_logits, hg_norm, w_branch, w_out):
    BP, T, D = x_prompt.shape
    BS = x_sample.shape[0]
    depth = w_in.shape[0]
    n_pool, page = cache_nsa_kv.shape[1], cache_nsa_kv.shape[2]
    npages = page_table.shape[1]
    past = npages * page
    n_win = cache_nsa_win.shape[2]
    rows_p = BP * T
    assert D == D_MODEL and x_sample.shape[1] == 1 and BS % SUBLANES == 0
    assert n_win == min(NSA_WINDOW, past) and past % NSA_BLOCK == 0
    k_top = min(NSA_TOPK, past // NSA_BLOCK + 1)

    xp = x_prompt.reshape(rows_p, D)
    xs = x_sample.reshape(BS, D)
    cs_p = _rope_table(jnp.tile(jnp.arange(T, dtype=I32), BP))
    cs_s = _rope_table(jnp.full((BS,), past, I32))
    lbp = jax.nn.softmax(hg_lb_logits.astype(F32), axis=0)
    lower_bounds = jnp.cumsum(lbp, axis=0) - lbp[0]

    nsa_t = jnp.transpose(cache_nsa_kv, (0, 1, 3, 4, 2)).reshape(depth, n_pool, 4 * DK, page)
    mla_t = jnp.transpose(cache_mla, (0, 1, 3, 2))
    win_t = jnp.transpose(cache_nsa_win, (0, 1, 3, 4, 2)).reshape(depth, BS, 2 * DK, n_win)

    def row3(a):
        return a.reshape(BS, 1, a.shape[1])

    outs = [[] for _ in range(12)]
    for l in range(depth):
        g = norm_g[l].astype(F32)
        grow = lambda i: g[i].reshape(1, D)
        wi = ffn_w_in[l].astype(BF16)
        wo = ffn_w_out[l].astype(BF16)
        w_pack, w_gate = _pack_w_in(w_in[l])
        wuq_p, wukn, wv = _pack_mla(mla_w_uq[l], mla_w_ukv[l])
        mqn = mla_q_norm[l].reshape(1, -1)
        mkvn = mla_kv_norm[l].reshape(1, -1)

        xp = _ffn(xp, grow(0), grow(1), wi[0], wo[0])
        xs = _ffn(xs, grow(0), grow(1), wi[0], wo[0])
        (nsa_q, nsa_pg, nsa_w, nsa_g, qlat, qpe, mla_new, ssd_z, ssd_xbc, ssd_dt, hg) = _inproj(
            xp, grow(2), cs_p, w_pack, mqn, mkvn, wuq_p, wukn)
        (nsa_q_s, nsa_pg_s, nsa_w_s, nsa_g_s, qlat_s, qpe_s, mla_new_s, ssd_z_s, ssd_xbc_s, ssd_dt_s,
         hg_s) = _inproj(xs, grow(2), cs_s, w_pack, mqn, mkvn, wuq_p, wukn)

        o_nsa = _nsa_prompt(nsa_q, nsa_pg, nsa_w, nsa_g, BP, T)
        pad8 = lambda a: jnp.pad(a, ((0, 0), (0, SUBLANES - HEADS), (0, 0)))
        q4 = nsa_q_s.reshape(BS, HEADS, DK)
        q8 = pad8(q4)
        o_c, imp = _nsa_cmp_decode(l, page_table, jnp.transpose(q4, (0, 2, 1)), nsa_t, past)
        idx = _topk(imp.reshape(BS, 2 * LANES), past, k_top)
        o_nsa_s = _nsa_sel_decode(l, page_table, idx, q8, row3(nsa_pg_s), row3(nsa_w_s), win_t,
                                  row3(nsa_g_s), o_c, nsa_t, k_top)

        o_mla = _mla_prompt(qlat, qpe, mla_new, wv, BP, T)
        q_cat = jnp.concatenate([qlat_s.reshape(BS, HEADS, LANES),
                                 qpe_s.reshape(BS, HEADS, LANES)[:, :, :MLA_ROPE]], axis=-1)
        o_mla_s = _mla_decode(l, page_table, pad8(q_cat), jnp.transpose(q_cat, (0, 2, 1)),
                              row3(mla_new_s), mla_t, wv)

        cw = ssd_conv_w[l].astype(F32)
        cb = ssd_conv_b[l].reshape(1, -1).astype(F32)
        dtb = _pad_row(ssd_dt_bias[l])
        alog = _pad_row(ssd_a_log[l])
        dsk = _pad_row(ssd_d[l])
        sn = ssd_norm[l].reshape(1, -1).astype(F32)
        lb = lower_bounds[l].reshape(1, -1)
        hn = hg_norm[l].reshape(1, -1).astype(F32)
        o_ssd, ssd_p = _ssd_prompt(ssd_xbc, ssd_dt, ssd_z, cw, cb, dtb, alog, dsk, sn, BP, T)
        o_hg, hgrn_p = _hgrn_prompt(hg, lb, hn, BP, T)
        o_ssd_s, o_hg_s, ssd_s, hgrn_s = _rec_decode(
            l, row3(ssd_xbc_s), state_ssd_conv, row3(ssd_dt_s), row3(ssd_z_s), row3(hg_s), state_ssd,
            state_hgrn, cw, cb, dtb, alog, dsk, sn, lb, hn)

        wbr = w_branch[l].astype(BF16)
        wout = w_out[l].astype(BF16)
        xp = _merge(xp, (o_nsa, o_mla, o_ssd, o_hg), grow(2), grow(3), w_gate, wbr, wout)
        xs = _merge(xs, (o_nsa_s, o_mla_s, o_ssd_s, o_hg_s), grow(2), grow(3), w_gate, wbr, wout)
        xp = _ffn(xp, grow(4), grow(5), wi[1], wo[1])
        xs = _ffn(xs, grow(4), grow(5), wi[1], wo[1])

        n_keep = min(NSA_WINDOW, T)
        layer_out = (
            nsa_pg.reshape(BP, T, 4, DK), nsa_pg_s.reshape(BS, 1, 4, DK),
            mla_new.reshape(BP, T, MLA_CACHE), mla_new_s.reshape(BS, 1, MLA_CACHE),
            nsa_w.reshape(BP, T, 2, DK)[:, T - n_keep:],
            jnp.concatenate([cache_nsa_win[l], nsa_w_s.reshape(BS, 1, 2, DK)], axis=1)[:, 1:],
            ssd_p, ssd_s,
            ssd_xbc.reshape(BP, T, SSD_CONV_DIM)[:, T - (SSD_CONV - 1):],
            jnp.concatenate([state_ssd_conv[l], ssd_xbc_s.reshape(BS, 1, SSD_CONV_DIM)], axis=1)[:, 1:],
            hgrn_p, hgrn_s)
        for acc, o in zip(outs, layer_out):
            acc.append(o)

    stacked = tuple(jnp.stack(o) for o in outs)
    return (xp.reshape(BP, T, D), xs.reshape(BS, 1, D)) + stacked
```

```python
import functools
import math

import numpy as np
import jax
import jax.numpy as jnp
from jax import lax
from jax.experimental import pallas as pl
from jax.experimental.pallas import tpu as pltpu

F32 = jnp.float32
BF16 = jnp.bfloat16
I32 = jnp.int32
HIGHEST = lax.Precision.HIGHEST

EPS = 1e-6
NEG = -1e30
LB_FLOOR = 1e-20
ROPE_THETA = 10000.0

D_MODEL = 1024
D_FF = 2816
HEADS = 4
DK = 64
BRANCH = HEADS * DK
NSA_BLOCK = 64
NSA_TOPK = 16
NSA_WINDOW = 512
NSA_FORCED_BONUS = float(HEADS + 1)
MLA_Q_LORA = 256
MLA_KV_LORA = 128
MLA_NOPE = 64
MLA_ROPE = 32
MLA_V = 64
MLA_CACHE = MLA_KV_LORA + MLA_ROPE
SSD_GROUPS = 2
SSD_CONV = 4
SSD_CONV_DIM = BRANCH + 2 * SSD_GROUPS * DK
CHUNK = 128
QB = 128

LANES = 128
SUBLANES = 8
VMEM_LIMIT_BYTES = 56 * 1024 * 1024

_SEG = {}
_off = 0
for _name, _w in (("nsa_q", 256), ("nsa_pg", 256), ("nsa_w", 128), ("nsa_g", 128),
                  ("mla_cq", 256), ("mla_ckv", 128), ("mla_kpe", 128), ("mla_kpr", 128),
                  ("ssd_dt", 128), ("ssd_z", 256), ("ssd_xbc", 512), ("hg", 1024)):
    _SEG[_name] = (_off, _off + _w)
    _off += _w
PACKED_WIDTH = _off


def _cparams(sem):
    return pltpu.CompilerParams(dimension_semantics=sem, vmem_limit_bytes=VMEM_LIMIT_BYTES)


def _rms(x, g):
    return x * lax.rsqrt(jnp.mean(x * x, axis=-1, keepdims=True) + EPS) * g


def _dot(a, b):
    return jnp.dot(a.astype(BF16), b.astype(BF16), preferred_element_type=F32)


def _dot_nt(a, b):
    return lax.dot_general(a.astype(BF16), b.astype(BF16), (((1,), (1,)), ((), ())),
                           preferred_element_type=F32)


def _dot_rows_split(a, b):
    half = a.shape[0] // 2
    return jnp.concatenate([_dot(a[:half], b), _dot(a[half:], b)], axis=0)


def _dot_nt_rows_split(a, b):
    half = a.shape[0] // 2
    return jnp.concatenate([_dot_nt(a[:half], b), _dot_nt(a[half:], b)], axis=0)


def _dot_f32(a, b):
    return jnp.dot(a, b, precision=HIGHEST, preferred_element_type=F32)


def _dot_nt_f32(a, b):
    return lax.dot_general(a, b, (((1,), (1,)), ((), ())), precision=HIGHEST,
                           preferred_element_type=F32)


def _sigmoid(x):
    return 1.0 / (1.0 + jnp.exp(-x))


def _silu(x):
    return x * _sigmoid(x)


def _softplus(x):
    return jnp.maximum(x, 0.0) + jnp.log1p(jnp.exp(-jnp.abs(x)))


def _masked_softmax(s, mask):
    s = jnp.where(mask, s, NEG)
    m = jnp.max(s, axis=-1, keepdims=True)
    e = jnp.exp(s - m)
    r = 1.0 / jnp.sum(e, axis=-1, keepdims=True)
    return jnp.where(mask, e * r, 0.0)


def _unnormalised_softmax(s, mask):
    s = jnp.where(mask, s, NEG)
    e = jnp.exp(s - jnp.max(s, axis=-1, keepdims=True))
    return e, 1.0 / jnp.sum(e, axis=-1, keepdims=True)


def _row_tile(rows, cap=864):
    best = SUBLANES
    for t in range(SUBLANES, min(rows, cap) + 1, SUBLANES):
        if rows % t == 0:
            best = t
    assert rows % best == 0
    return best


def _const_spec(shape):
    nd = len(shape)
    return pl.BlockSpec(shape, lambda *_: (0,) * nd)


MXU_TILE = 256
FFN_BOUNDS = (0, 6 * MXU_TILE, D_FF)


def _ffn_kernel(x_ref, ga_ref, gb_ref, wi_ref, wo_ref, o_ref):
    x = x_ref[...]
    h = _rms(x, ga_ref[...]).astype(BF16)
    acc = None
    for a, b in zip(FFN_BOUNDS[:-1], FFN_BOUNDS[1:]):
        gate = jnp.dot(h, wi_ref[:, a:b], preferred_element_type=F32)
        up = jnp.dot(h, wi_ref[:, D_FF + a:D_FF + b], preferred_element_type=F32)
        part = _dot(_silu(gate) * up, wo_ref[a:b, :])
        acc = part if acc is None else acc + part
    o_ref[...] = x + 0.5 * _rms(acc, gb_ref[...])


def _resident_spec(shape):
    nd = len(shape)
    return pl.BlockSpec(shape, lambda *_: (0,) * nd, pipeline_mode=pl.Buffered(1))


def _ffn(x, ga, gb, wi, wo):
    rows = x.shape[0]
    tm = _row_tile(rows, cap=512)
    assert all(b % MXU_TILE == 0 for b in FFN_BOUNDS)
    return pl.pallas_call(
        _ffn_kernel,
        grid=(rows // tm,),
        in_specs=[
            pl.BlockSpec((tm, D_MODEL), lambda i: (i, 0)),
            _const_spec((1, D_MODEL)),
            _const_spec((1, D_MODEL)),
            _resident_spec(wi.shape),
            _resident_spec(wo.shape),
        ],
        out_specs=pl.BlockSpec((tm, D_MODEL), lambda i: (i, 0)),
        out_shape=jax.ShapeDtypeStruct((rows, D_MODEL), F32),
        compiler_params=_cparams(("parallel",)),
        name="ffn",
    )(x, ga, gb, wi, wo)


def _inproj_kernel(x_ref, g_ref, cs_ref, w_ref, mqn_ref, mkvn_ref, wuq_ref, wukn_ref,
                   q_o, pg_o, w_o, g_o, qlat_o, qpe_o, mla_o, z_o, xbc_o, dt_o, hg_o):
    hn = _rms(x_ref[...], g_ref[...]).astype(BF16)

    def seg(first, last=None):
        a, b = _SEG[first][0], _SEG[last or first][1]
        return jnp.dot(hn, w_ref[:, a:b], preferred_element_type=F32)

    q_o[...] = seg("nsa_q") * (DK ** -0.5)
    pg_o[...] = seg("nsa_pg")
    w_g = seg("nsa_w", "nsa_g")
    w_o[...] = w_g[:, 0:LANES]
    g_o[...] = _sigmoid(w_g[:, LANES:2 * LANES])
    z_o[...] = seg("ssd_z")
    xbc_o[...] = seg("ssd_xbc")
    hg_o[...] = seg("hg")
    ckv_kpe = seg("mla_ckv", "mla_kpe")
    kpr_dt = seg("mla_kpr", "ssd_dt")
    dt_o[...] = kpr_dt[:, LANES:2 * LANES]

    cos_p = cs_ref[:, 0:LANES]
    sin_p = cs_ref[:, LANES:2 * LANES]
    cqn = _rms(seg("mla_cq"), mqn_ref[...])
    qf = _dot(cqn, wuq_ref[...])
    nope_w = HEADS * MLA_NOPE
    qlat_o[...] = _dot(qf[:, 0:nope_w], wukn_ref[...])
    for h in range(HEADS):
        pe = qf[:, nope_w + LANES * h: nope_w + LANES * (h + 1)]
        pr = qf[:, nope_w + LANES * (HEADS + h): nope_w + LANES * (HEADS + h + 1)]
        qpe_o[:, LANES * h:LANES * (h + 1)] = pe * cos_p + pr * sin_p
    mla_o[:, 0:MLA_KV_LORA] = _rms(ckv_kpe[:, 0:LANES], mkvn_ref[...])
    kpe = ckv_kpe[:, LANES:2 * LANES] * cos_p + kpr_dt[:, 0:LANES] * sin_p
    mla_o[:, MLA_KV_LORA:MLA_CACHE] = kpe[:, 0:MLA_ROPE]


_INPROJ_OUT_WIDTHS = (256, 256, 128, 128, 512, 512, MLA_CACHE, 256, 512, 128, 1024)


def _inproj(x, g, cs, w, mqn, mkvn, wuq, wukn):
    rows = x.shape[0]
    tm = _row_tile(rows)
    row_spec = lambda c: pl.BlockSpec((tm, c), lambda i: (i, 0))
    return pl.pallas_call(
        _inproj_kernel,
        grid=(rows // tm,),
        in_specs=[row_spec(D_MODEL), _const_spec((1, D_MODEL)), row_spec(2 * LANES),
                  _const_spec(w.shape), _const_spec(mqn.shape), _const_spec(mkvn.shape),
                  _const_spec(wuq.shape), _const_spec(wukn.shape)],
        out_specs=[row_spec(c) for c in _INPROJ_OUT_WIDTHS],
        out_shape=[jax.ShapeDtypeStruct((rows, c), F32) for c in _INPROJ_OUT_WIDTHS],
        compiler_params=_cparams(("parallel",)),
        name="inproj",
    )(x, g, cs, w, mqn, mkvn, wuq, wukn)


def _merge_kernel(x_ref, b0_ref, b1_ref, b2_ref, b3_ref, g2_ref, g3_ref, wg_ref, wbr_ref, wout_ref,
                  o_ref):
    x = x_ref[...]
    hn = _rms(x, g2_ref[...]).astype(BF16)
    acc = None
    for k, b_ref in enumerate((b0_ref, b1_ref, b2_ref, b3_ref)):
        gate = _sigmoid(jnp.dot(hn, wg_ref[:, D_MODEL * k:D_MODEL * (k + 1)],
                                preferred_element_type=F32))
        u = _dot(b_ref[...], wbr_ref[k])
        acc = gate * u if acc is None else acc + gate * u
    out = _dot(acc, wout_ref[...])
    o_ref[...] = x + _rms(out, g3_ref[...])


def _merge(x, branches, g2, g3, wgate, wbr, wout):
    rows = x.shape[0]
    tm = _row_tile(rows, cap=512)
    row_spec = lambda c: pl.BlockSpec((tm, c), lambda i: (i, 0))
    return pl.pallas_call(
        _merge_kernel,
        grid=(rows // tm,),
        in_specs=[row_spec(D_MODEL)] + [row_spec(BRANCH)] * 4 +
                 [_const_spec((1, D_MODEL)), _const_spec((1, D_MODEL)),
                  _const_spec(wgate.shape), _const_spec(wbr.shape), _const_spec(wout.shape)],
        out_specs=row_spec(D_MODEL),
        out_shape=jax.ShapeDtypeStruct((rows, D_MODEL), F32),
        compiler_params=_cparams(("parallel",)),
        name="merge",
    )(x, *branches, g2, g3, wgate, wbr, wout)


KEY_STEP = 256


def _key_extents(T):
    his = sorted({min(T, KEY_STEP * (c + 1)) for c in range(-(-T // KEY_STEP))})
    return list(zip([0] + his[:-1], his))


def _stack_heads(x, width, take):
    return jnp.concatenate([x[:, width * h: width * h + take] for h in range(HEADS)], axis=0)


def _nsa_prompt_kernel(q_ref, pg_ref, wk_ref, g_ref, e_ref, o_ref, *, T, band):
    nb = T // NSA_BLOCK
    k_top = min(NSA_TOPK, nb)
    q0 = pl.program_id(1) * QB
    q = q_ref[...]
    zeros = jnp.zeros((QB, DK), F32)
    q_rows = jnp.concatenate(
        [jnp.concatenate([q[:, DK * h:DK * (h + 1)], zeros], axis=1) for h in range(HEADS)],
        axis=0).astype(BF16)
    pos = q0 + lax.broadcasted_iota(I32, (QB, 1), 0)
    lane = lax.broadcasted_iota(I32, (QB, LANES), 1)

    def per_head(fn):
        return jnp.concatenate([fn(h) for h in range(HEADS)], axis=0)

    def attend_values(s, mask, kv):
        parts = [_unnormalised_softmax(s[QB * h:QB * (h + 1)], mask) for h in range(HEADS)]
        e = jnp.concatenate([p[0] for p in parts], axis=0)
        r = jnp.concatenate([p[1] for p in parts], axis=0)
        return _dot_rows_split(e, kv) * r

    cm = jnp.sum(pg_ref[:, 0:LANES].reshape(nb, NSA_BLOCK, LANES), axis=1) * (1.0 / NSA_BLOCK)
    if nb < LANES:
        cm = jnp.concatenate([cm, jnp.zeros((LANES - nb, LANES), F32)], axis=0)
    cm = cm.astype(BF16)
    s_c = _dot_nt_rows_split(q_rows, cm)
    cmask = (lane + 1) * NSA_BLOCK - 1 <= pos
    p_c = per_head(lambda h: _masked_softmax(s_c[QB * h:QB * (h + 1)], cmask))
    o_c = _dot_rows_split(p_c, cm)

    cur = pos // NSA_BLOCK
    imp = p_c[0:QB] + p_c[QB:2 * QB] + p_c[2 * QB:3 * QB] + p_c[3 * QB:4 * QB]
    forced = (lane == 0) | (lane == cur) | (lane == cur - 1)
    imp = imp + jnp.where(forced, NSA_FORCED_BONUS, 0.0)
    imp = jnp.where(lane > cur, -1.0, imp)
    rank = jnp.zeros((QB, LANES), F32)
    for n in range(nb):
        c = imp[:, n:n + 1]
        beats = (c > imp) | ((c == imp) & (lane > n))
        rank = rank + jnp.where(beats, 1.0, 0.0)
    sel = jnp.where(rank < float(k_top), 1.0, 0.0)

    def selected(tk):
        in_sel = jnp.dot(sel.astype(BF16), e_ref[:, 0:tk], preferred_element_type=F32)
        kpos = lax.broadcasted_iota(I32, (QB, tk), 1)
        smask = (in_sel > 0.5) & (kpos <= pos)
        kv_s = pg_ref[0:tk, LANES:2 * LANES].astype(BF16)
        s_s = _dot_nt(q_rows, kv_s)
        return attend_values(s_s, smask, kv_s)

    start = pl.multiple_of(jnp.clip(q0 - NSA_WINDOW, 0, T - band), LANES)
    kv_w = wk_ref[pl.ds(start, band), :].astype(BF16)
    kposw = start + lax.broadcasted_iota(I32, (QB, band), 1)
    wmask = (kposw <= pos) & (kposw >= pos - NSA_WINDOW)
    o_w = attend_values(_dot_nt(q_rows, kv_w), wmask, kv_w)

    g = g_ref[...]

    def finish(o_s):
        outs = []
        for h in range(HEADS):
            r = slice(QB * h, QB * (h + 1))
            comb = (g[:, 3 * h:3 * h + 1] * o_c[r] + g[:, 3 * h + 1:3 * h + 2] * o_s[r]
                    + g[:, 3 * h + 2:3 * h + 3] * o_w[r])
            outs.append(comb[:, DK:2 * DK])
        o_ref[...] = jnp.concatenate(outs, axis=1)

    for lo, tk in _key_extents(T):
        @pl.when((q0 + QB > lo) & (q0 + QB <= tk))
        def _(tk=tk):
            finish(selected(tk))


def _nsa_prompt(q, pg, wkv, g, B, T):
    assert T % QB == 0 and T // NSA_BLOCK <= LANES
    rows = B * T
    nq = T // QB
    band = min(NSA_WINDOW + QB, T)
    nb = T // NSA_BLOCK
    expand = (np.arange(LANES)[:, None] == (np.arange(T)[None, :] // NSA_BLOCK)) & (np.arange(LANES)[:, None] < nb)
    expand = jnp.asarray(expand, BF16)
    return pl.pallas_call(
        functools.partial(_nsa_prompt_kernel, T=T, band=band),
        grid=(B, nq),
        in_specs=[pl.BlockSpec((QB, BRANCH), lambda b, i: (b * nq + i, 0)),
                  pl.BlockSpec((T, 256), lambda b, i: (b, 0)),
                  pl.BlockSpec((T, LANES), lambda b, i: (b, 0)),
                  pl.BlockSpec((QB, LANES), lambda b, i: (b * nq + i, 0)),
                  _const_spec((LANES, T))],
        out_specs=pl.BlockSpec((QB, BRANCH), lambda b, i: (b * nq + i, 0)),
        out_shape=jax.ShapeDtypeStruct((rows, BRANCH), F32),
        compiler_params=_cparams(("parallel", "parallel")),
        name="nsa_prompt",
    )(q, pg, wkv, g, expand)


def _mla_prompt_kernel(qlat_ref, qpe_ref, kv_ref, wv_ref, o_ref, *, T):
    q0 = pl.program_id(1) * QB
    q_cat = jnp.concatenate([_stack_heads(qlat_ref[...], LANES, LANES),
                             _stack_heads(qpe_ref[...], LANES, MLA_ROPE)], axis=1).astype(BF16)
    scale = (MLA_NOPE + MLA_ROPE) ** -0.5
    pos = q0 + lax.broadcasted_iota(I32, (QB, 1), 0)

    def attend(tk):
        kv = kv_ref[0:tk, :].astype(BF16)
        ckv = kv[:, 0:MLA_KV_LORA]
        s = _dot_nt(q_cat, kv) * scale
        mask = lax.broadcasted_iota(I32, (QB, tk), 1) <= pos
        parts = [_unnormalised_softmax(s[QB * h:QB * (h + 1)], mask) for h in range(HEADS)]
        e = jnp.concatenate([p[0] for p in parts], axis=0)
        r = jnp.concatenate([p[1] for p in parts], axis=0)
        o_lat = _dot_rows_split(e, ckv) * r
        o_cat = jnp.concatenate([o_lat[QB * h:QB * (h + 1)] for h in range(HEADS)], axis=1)
        o_ref[...] = _dot_rows_split(o_cat, wv_ref[...])

    for lo, tk in _key_extents(T):
        @pl.when((q0 + QB > lo) & (q0 + QB <= tk))
        def _(tk=tk):
            attend(tk)


def _mla_prompt(qlat, qpe, mla_new, wv, B, T):
    nq = T // QB
    rows = B * T
    return pl.pallas_call(
        functools.partial(_mla_prompt_kernel, T=T),
        grid=(B, nq),
        in_specs=[pl.BlockSpec((QB, 512), lambda b, i: (b * nq + i, 0)),
                  pl.BlockSpec((QB, 512), lambda b, i: (b * nq + i, 0)),
                  pl.BlockSpec((T, MLA_CACHE), lambda b, i: (b, 0)),
                  _const_spec(wv.shape)],
        out_specs=pl.BlockSpec((QB, BRANCH), lambda b, i: (b * nq + i, 0)),
        out_shape=jax.ShapeDtypeStruct((rows, BRANCH), F32),
        compiler_params=_cparams(("parallel", "parallel")),
        name="mla_prompt",
    )(qlat, qpe, mla_new, wv)


def _head_lanes(x_cols, shape):
    lane = lax.broadcasted_iota(I32, shape, 1)
    out = jnp.broadcast_to(x_cols[:, HEADS - 1:HEADS], shape)
    for h in range(HEADS - 2, -1, -1):
        out = jnp.where(lane < DK * (h + 1), jnp.broadcast_to(x_cols[:, h:h + 1], shape), out)
    return out


def _ssd_prompt_kernel(xbc_ref, dt_ref, z_ref, cw_ref, cb_ref, dtb_ref, alog_ref, dsk_ref, sn_ref,
                       o_ref, st_ref, buf_ref, s_ref):
    c = pl.program_id(1)
    C = CHUNK

    @pl.when(c == 0)
    def _():
        buf_ref[0:SUBLANES, :] = jnp.zeros((SUBLANES, SSD_CONV_DIM), F32)
        s_ref[...] = jnp.zeros_like(s_ref)

    x = xbc_ref[...]
    buf_ref[SUBLANES:SUBLANES + C, :] = x
    conv = cb_ref[...] + cw_ref[SSD_CONV - 1:SSD_CONV, :] * x
    for j in range(SSD_CONV - 1):
        conv = conv + cw_ref[j:j + 1, :] * buf_ref[pl.ds(SUBLANES - (SSD_CONV - 1) + j, C), :]
    buf_ref[0:SUBLANES, :] = x[C - SUBLANES:C, :]
    xa = _silu(conv)
    xs = xa[:, 0:BRANCH]
    dt = _softplus(dt_ref[...] + dtb_ref[...])
    a = dt * (-jnp.exp(alog_ref[...]))
    row = lax.broadcasted_iota(I32, (C, C), 0)
    col = lax.broadcasted_iota(I32, (C, C), 1)
    tril = row >= col
    tri = jnp.where(tril, 1.0, 0.0)
    cum_c = _dot_f32(tri, a)
    cum_r = _dot_nt_f32(a.T, tri)
    xdt = xs * _head_lanes(dt, (C, BRANCH))
    xdt_t = xdt.T
    dsk = dsk_ref[...]
    ys = []
    for h in range(HEADS):
        g = h // (HEADS // SSD_GROUPS)
        bm = xa[:, BRANCH + DK * g:BRANCH + DK * (g + 1)]
        cm = xa[:, BRANCH + DK * (SSD_GROUPS + g):BRANCH + DK * (SSD_GROUPS + g + 1)]
        ch = cum_c[:, h:h + 1]
        decay = jnp.exp(jnp.where(tril, ch - cum_r[h:h + 1, :], NEG))
        scores = _dot_nt(cm, bm) * decay
        xdt_h = xdt[:, DK * h:DK * (h + 1)]
        s_old = s_ref[h]
        y = _dot(scores, xdt_h) + _dot_nt(cm, s_old) * jnp.exp(ch)
        last = cum_c[C - 1:C, h:h + 1]
        w = jnp.exp(last - ch)
        s_ref[h] = s_old * jnp.exp(last) + _dot(xdt_t[DK * h:DK * (h + 1), :], bm * w)
        ys.append(y + dsk[:, h:h + 1] * xs[:, DK * h:DK * (h + 1)])
    y = jnp.concatenate(ys, axis=1)
    o_ref[...] = _rms(y * _silu(z_ref[...]), sn_ref[...])

    @pl.when(c == pl.num_programs(1) - 1)
    def _():
        st_ref[0] = s_ref[...]


def _ssd_prompt(xbc, dt, z, cw, cb, dtb, alog, dsk, sn, B, T):
    assert T % CHUNK == 0
    nc = T // CHUNK
    rows = B * T
    row_spec = lambda w: pl.BlockSpec((CHUNK, w), lambda b, c: (b * nc + c, 0))
    return pl.pallas_call(
        _ssd_prompt_kernel,
        grid=(B, nc),
        in_specs=[row_spec(SSD_CONV_DIM), row_spec(LANES), row_spec(BRANCH),
                  _const_spec(cw.shape), _const_spec(cb.shape), _const_spec(dtb.shape),
                  _const_spec(alog.shape), _const_spec(dsk.shape), _const_spec(sn.shape)],
        out_specs=[row_spec(BRANCH), pl.BlockSpec((1, HEADS, DK, DK), lambda b, c: (b, 0, 0, 0))],
        out_shape=[jax.ShapeDtypeStruct((rows, BRANCH), F32),
                   jax.ShapeDtypeStruct((B, HEADS, DK, DK), F32)],
        scratch_shapes=[pltpu.VMEM((SUBLANES + CHUNK, SSD_CONV_DIM), F32),
                        pltpu.VMEM((HEADS, DK, DK), F32)],
        compiler_params=_cparams(("parallel", "arbitrary")),
        name="ssd_prompt",
    )(xbc, dt, z, cw, cb, dtb, alog, dsk, sn)


def _hgrn_gates(fr, lb):
    log_sig = jnp.minimum(fr, 0.0) - jnp.log1p(jnp.exp(-jnp.abs(fr)))
    a = jnp.log(jnp.maximum(lb, LB_FLOOR))
    b = jnp.log1p(-lb) + log_sig
    log_f = jnp.maximum(a, b) + jnp.log1p(jnp.exp(-jnp.abs(a - b)))
    return log_f, (1.0 - lb) * _sigmoid(-fr)


def _block_reference_rows(G, m):
    C = G.shape[0]
    if m >= 4:
        parts = []
        for p in range(C // (2 * m)):
            r = p * 2 * m + m - 1
            parts.append(jnp.broadcast_to(G[r:r + 1, :], (2 * m, G.shape[1])))
        return parts[0] if len(parts) == 1 else jnp.concatenate(parts, axis=0)
    t = lax.broadcasted_iota(I32, G.shape, 0)
    if m == 1:
        return jnp.where((t & 1) == 1, pltpu.roll(G, 1, 0), G)
    r = t & 3
    return jnp.where(r == 0, pltpu.roll(G, C - 1, 0),
                     jnp.where(r == 1, G, jnp.where(r == 2, pltpu.roll(G, 1, 0), pltpu.roll(G, 2, 0))))


def _hgrn_prompt_kernel(hg_ref, lb_ref, hn_ref, seg_ref, o_ref, st_ref, s_ref):
    c = pl.program_id(1)
    C = CHUNK
    W = BRANCH

    @pl.when(c == 0)
    def _():
        s_ref[...] = jnp.zeros_like(s_ref)

    q = hg_ref[:, 0:W]
    v = hg_ref[:, 2 * W:3 * W]
    log_f, kin = _hgrn_gates(hg_ref[:, W:2 * W], lb_ref[...])
    row = lax.broadcasted_iota(I32, (C, C), 0)
    col = lax.broadcasted_iota(I32, (C, C), 1)
    G = _dot_f32(jnp.where(row >= col, 1.0, 0.0), log_f)
    lane_head = lax.broadcasted_iota(I32, (C, W), 1) // DK
    t_idx = lax.broadcasted_iota(I32, (C, W), 0)

    def stack_heads(a):
        return jnp.concatenate([jnp.where(lane_head == h, a, 0.0) for h in range(HEADS)],
                               axis=0).astype(BF16)

    t4 = lax.broadcasted_iota(I32, (HEADS * C, C), 0) & (C - 1)
    s4 = lax.broadcasted_iota(I32, (HEADS * C, C), 1)
    att = jnp.where(t4 == s4, _dot_nt_rows_split(stack_heads(q), kin), 0.0)
    m = C // 2
    while m >= 1:
        R = _block_reference_rows(G, m)
        upper = (t_idx & m) != 0
        A = jnp.where(upper, q * jnp.exp(jnp.minimum(G - R, 0.0)), 0.0)
        Bm = jnp.where(upper, 0.0, kin * jnp.exp(jnp.minimum(R - G, 0.0)))
        lm = int(math.log2(m))
        pair = ((t4 >> lm) ^ (s4 >> lm)) == 1
        att = att + jnp.where(pair & (t4 > s4), _dot_nt_rows_split(stack_heads(A), Bm), 0.0)
        m //= 2
    y4 = _dot_rows_split(att, v)
    y = jnp.zeros((C, W), F32)
    for h in range(HEADS):
        y = y + jnp.where(lane_head == h, y4[C * h:C * (h + 1)], 0.0)
    s_old = s_ref[...]
    y = y + _dot(q * jnp.exp(G), s_old)
    last = G[C - 1:C, :]
    ke_t = (kin * jnp.exp(last - G)).T
    G_t = G.T
    blk = (lax.broadcasted_iota(I32, (W, W), 0) // DK) == (lax.broadcasted_iota(I32, (W, W), 1) // DK)
    s_ref[...] = s_old * jnp.exp(G_t[:, C - 1:C]) + jnp.where(blk, _dot(ke_t, v), 0.0)
    ms = _dot_f32(y * y, seg_ref[...])
    o = y * lax.rsqrt(ms + EPS) * hn_ref[...]
    o_ref[...] = o * _silu(hg_ref[:, 3 * W:4 * W])

    @pl.when(c == pl.num_programs(1) - 1)
    def _():
        for h in range(HEADS):
            st_ref[0, h] = s_ref[DK * h:DK * (h + 1), DK * h:DK * (h + 1)]


def _hgrn_prompt(hg, lb, hn, B, T):
    nc = T // CHUNK
    rows = B * T
    seg = (np.arange(BRANCH)[:, None] // DK == np.arange(BRANCH)[None, :] // DK) / float(DK)
    seg = jnp.asarray(seg, F32)
    row_spec = lambda w: pl.BlockSpec((CHUNK, w), lambda b, c: (b * nc + c, 0))
    return pl.pallas_call(
        _hgrn_prompt_kernel,
        grid=(B, nc),
        in_specs=[row_spec(4 * BRANCH), _const_spec(lb.shape), _const_spec(hn.shape),
                  _const_spec(seg.shape)],
        out_specs=[row_spec(BRANCH), pl.BlockSpec((1, HEADS, DK, DK), lambda b, c: (b, 0, 0, 0))],
        out_shape=[jax.ShapeDtypeStruct((rows, BRANCH), F32),
                   jax.ShapeDtypeStruct((B, HEADS, DK, DK), F32)],
        scratch_shapes=[pltpu.VMEM((BRANCH, BRANCH), F32)],
        compiler_params=_cparams(("parallel", "arbitrary")),
        name="hgrn_prompt",
    )(hg, lb, hn, seg)


PAGES_PER_STEP = 64


def _mla_decode_kernel(pt_ref, q_ref, qt_ref, new_ref, *rest, n_pg):
    page_refs = rest[:n_pg]
    wv_ref, o_ref, qb_ref, m_ref, l_ref, acc_ref = rest[n_pg:]
    del pt_ref
    b = pl.program_id(0)
    j = pl.program_id(1)
    scale = (MLA_NOPE + MLA_ROPE) ** -0.5
    page = page_refs[0].shape[1]

    @pl.when(j == 0)
    def _():
        qt = qt_ref[0]
        for h in range(HEADS):
            qb_ref[h] = jnp.broadcast_to(qt[:, h:h + 1], (MLA_CACHE, LANES))
        m_ref[...] = jnp.full_like(m_ref, NEG)
        l_ref[...] = jnp.zeros_like(l_ref)
        acc_ref[...] = jnp.zeros_like(acc_ref)

    scores = [[] for _ in range(HEADS)]
    for h in range(HEADS):
        qb = qb_ref[h]
        for r in page_refs:
            scores[h].append(jnp.sum(r[...] * qb, axis=0, keepdims=True))
    s = jnp.concatenate([jnp.concatenate(scores[h], axis=1) for h in range(HEADS)], axis=0) * scale
    m_old = m_ref[0:HEADS, :]
    m_new = jnp.maximum(m_old, jnp.max(s, axis=-1, keepdims=True))
    alpha = jnp.exp(m_old - m_new)
    p = jnp.exp(s - m_new[:, 0:1])
    l_ref[0:HEADS, :] = alpha * l_ref[0:HEADS, :] + jnp.sum(p, axis=-1, keepdims=True)
    m_ref[0:HEADS, :] = m_new
    for h in range(HEADS):
        acc = alpha[h:h + 1, 0:1] * acc_ref[h]
        for t, r in enumerate(page_refs):
            acc = acc + r[0:MLA_KV_LORA, :] * p[h:h + 1, t * page:(t + 1) * page]
        acc_ref[h] = acc

    @pl.when(j == pl.num_programs(1) - 1)
    def _():
        new = new_ref[0]
        s_new = jnp.sum(q_ref[0] * new, axis=-1, keepdims=True) * scale
        ones = jnp.ones((SUBLANES, LANES), F32)
        outs = []
        for h in range(HEADS):
            m_old = m_ref[h:h + 1, :]
            m_fin = jnp.maximum(m_old, s_new[h:h + 1, :])
            alpha = jnp.exp(m_old - m_fin)
            p_n = jnp.exp(s_new[h:h + 1, :] - m_fin)
            l = alpha * l_ref[h:h + 1, :] + p_n
            lat = _dot_nt_f32(ones, acc_ref[h])[0:1, :]
            outs.append((alpha * lat + p_n * new[:, 0:MLA_KV_LORA]) / l)
        o_cat = jnp.concatenate(outs, axis=1)
        o = _dot(jnp.broadcast_to(o_cat, (SUBLANES, HEADS * MLA_KV_LORA)), wv_ref[...])
        o_ref[pl.ds(b, 1), :] = o[0:1, :]


def _mla_decode(layer, page_table, q8, qt, new_rows, cache_t, wv):
    BS, npages = page_table.shape
    page = cache_t.shape[3]
    n_pg = min(PAGES_PER_STEP, npages)
    assert npages % n_pg == 0 and page == LANES
    steps = npages // n_pg

    def page_spec(i):
        return pl.BlockSpec((None, None, MLA_CACHE, page),
                            lambda b, j, pt: (layer, pt[b, j * n_pg + i], 0, 0))

    grid_spec = pltpu.PrefetchScalarGridSpec(
        num_scalar_prefetch=1,
        grid=(BS, steps),
        in_specs=[pl.BlockSpec((1, SUBLANES, MLA_CACHE), lambda b, j, pt: (b, 0, 0)),
                  pl.BlockSpec((1, MLA_CACHE, HEADS), lambda b, j, pt: (b, 0, 0)),
                  pl.BlockSpec((1, 1, MLA_CACHE), lambda b, j, pt: (b, 0, 0))]
                 + [page_spec(i) for i in range(n_pg)]
                 + [pl.BlockSpec(wv.shape, lambda b, j, pt: (0, 0))],
        out_specs=pl.BlockSpec((BS, BRANCH), lambda b, j, pt: (0, 0)),
        scratch_shapes=[pltpu.VMEM((HEADS, MLA_CACHE, LANES), F32),
                        pltpu.VMEM((SUBLANES, LANES), F32), pltpu.VMEM((SUBLANES, LANES), F32),
                        pltpu.VMEM((HEADS, MLA_KV_LORA, LANES), F32)],
    )
    return pl.pallas_call(
        functools.partial(_mla_decode_kernel, n_pg=n_pg),
        grid_spec=grid_spec,
        out_shape=jax.ShapeDtypeStruct((BS, BRANCH), F32),
        compiler_params=_cparams(("arbitrary", "arbitrary")),
        name="mla_decode",
    )(page_table, q8, qt, new_rows, *([cache_t] * n_pg), wv)


def _nsa_cmp_decode_kernel(pt_ref, qt_ref, *rest, n_pg, past):
    page_refs = rest[:n_pg]
    oc_ref, imp_ref, qb_ref, ts_ref, vb_ref = rest[n_pg:]
    del pt_ref
    j = pl.program_id(1)
    steps = ts_ref.shape[0]
    page = page_refs[0].shape[1]
    nblk = 2 * LANES

    @pl.when(j == 0)
    def _():
        qt = qt_ref[0]
        for h in range(HEADS):
            qb_ref[h] = jnp.broadcast_to(qt[:, h:h + 1], (DK, LANES))

    for t, r in enumerate(page_refs):
        kt = r[0:DK, :]
        for h in range(HEADS):
            ts_ref[j, h:h + 1, t * page:(t + 1) * page] = jnp.sum(kt * qb_ref[h], axis=0, keepdims=True)
        vb_ref[j, :, t * page:(t + 1) * page] = r[DK:2 * DK, :]

    @pl.when(j == steps - 1)
    def _():
        blk = lax.broadcasted_iota(I32, (HEADS, nblk), 1)
        first = lax.broadcasted_iota(I32, (HEADS, page), 1) < NSA_BLOCK
        s = jnp.zeros((HEADS, nblk), F32)
        for jj in range(steps):
            ts = ts_ref[jj, 0:HEADS, :]
            for t in range(n_pg):
                x = ts[:, t * page:(t + 1) * page]
                b0 = 2 * (jj * n_pg + t)
                s0 = jnp.sum(jnp.where(first, x, 0.0), axis=1, keepdims=True)
                s1 = jnp.sum(jnp.where(first, 0.0, x), axis=1, keepdims=True)
                s = jnp.where(blk == b0, s0, jnp.where(blk == b0 + 1, s1, s))
        s = s * (1.0 / NSA_BLOCK)
        cmask = (blk + 1) * NSA_BLOCK - 1 <= past
        p = _masked_softmax(s, cmask)
        cur = past // NSA_BLOCK
        imp = jnp.sum(p, axis=0, keepdims=True)
        blk1 = blk[0:1]
        forced = (blk1 == 0) | (blk1 == cur - 1)
        imp = imp + jnp.where(forced, NSA_FORCED_BONUS, 0.0)
        imp_ref[0] = jnp.where(blk1 >= cur, -3.0, imp)

        pw = jnp.concatenate([p * (1.0 / NSA_BLOCK), jnp.zeros((SUBLANES - HEADS, nblk), F32)], axis=0)
        nb_step = 2 * n_pg
        expand = jnp.where(lax.broadcasted_iota(I32, (nb_step, n_pg * page), 0)
                           == lax.broadcasted_iota(I32, (nb_step, n_pg * page), 1) // NSA_BLOCK, 1.0, 0.0)
        accs = [jnp.zeros((DK, page), F32) for _ in range(HEADS)]
        for jj in range(steps):
            w = _dot_f32(pw[:, nb_step * jj:nb_step * (jj + 1)], expand)
            for t in range(n_pg):
                v = vb_ref[jj, :, t * page:(t + 1) * page]
                for h in range(HEADS):
                    accs[h] = accs[h] + v * w[h:h + 1, t * page:(t + 1) * page]
        ones = jnp.ones((SUBLANES, LANES), F32)
        oc_ref[0] = jnp.concatenate([_dot_nt_f32(ones, a)[0:1, :] for a in accs], axis=0)


def _nsa_cmp_decode(layer, page_table, qt, cache_t, past):
    BS, npages = page_table.shape
    page = cache_t.shape[3]
    n_pg = min(PAGES_PER_STEP, npages)
    assert page == 2 * NSA_BLOCK and page == LANES and 2 * npages <= 2 * LANES and npages % n_pg == 0
    steps = npages // n_pg

    def page_spec(i):
        return pl.BlockSpec((None, None, 2 * DK, page),
                            lambda b, j, pt: (layer, pt[b, j * n_pg + i], 0, 0))

    grid_spec = pltpu.PrefetchScalarGridSpec(
        num_scalar_prefetch=1,
        grid=(BS, steps),
        in_specs=[pl.BlockSpec((1, DK, HEADS), lambda b, j, pt: (b, 0, 0))]
                 + [page_spec(i) for i in range(n_pg)],
        out_specs=[pl.BlockSpec((1, HEADS, DK), lambda b, j, pt: (b, 0, 0)),
                   pl.BlockSpec((1, 1, 2 * LANES), lambda b, j, pt: (b, 0, 0))],
        scratch_shapes=[pltpu.VMEM((HEADS, DK, LANES), F32),
                        pltpu.VMEM((steps, SUBLANES, n_pg * page), F32),
                        pltpu.VMEM((steps, DK, n_pg * page), F32)],
    )
    return pl.pallas_call(
        functools.partial(_nsa_cmp_decode_kernel, n_pg=n_pg, past=past),
        grid_spec=grid_spec,
        out_shape=[jax.ShapeDtypeStruct((BS, HEADS, DK), F32),
                   jax.ShapeDtypeStruct((BS, 1, 2 * LANES), F32)],
        compiler_params=_cparams(("parallel", "arbitrary")),
        name="nsa_cmp_decode",
    )(page_table, qt, *([cache_t] * n_pg))


def _topk_kernel(imp_ref, idx_ref, *, past, k_top):
    BS = imp_ref.shape[0]
    cur = past // NSA_BLOCK
    blk = lax.broadcasted_iota(I32, (BS, 2 * LANES), 1).astype(F32)
    lane_x = lax.broadcasted_iota(I32, (BS, LANES), 1)
    imp = jnp.concatenate([imp_ref[...], jnp.where(lane_x == 0, NSA_FORCED_BONUS, -3.0)], axis=1)
    blk = jnp.concatenate([blk, (cur + lane_x).astype(F32)], axis=1)
    out = jnp.zeros((BS, LANES), F32)
    for k in range(k_top):
        m = jnp.max(imp, axis=-1, keepdims=True)
        pick = jnp.min(jnp.where(imp == m, blk, 1e9), axis=-1, keepdims=True)
        imp = jnp.where(blk == pick, -4.0, imp)
        out = jnp.where(lane_x == k, pick, out)
    idx_ref[...] = out.astype(I32)


def _topk(imp, past, k_top):
    BS = imp.shape[0]
    return pl.pallas_call(
        functools.partial(_topk_kernel, past=past, k_top=k_top),
        out_shape=jax.ShapeDtypeStruct((BS, LANES), I32),
        name="nsa_topk",
    )(imp)


def _nsa_sel_decode_kernel(pt_ref, idx_ref, q_ref, pgn_ref, wn_ref, win_ref, g_ref, oc_ref, *rest,
                           k_top, n_blocks):
    blk_refs = rest[:k_top]
    (o_ref,) = rest[k_top:]
    del pt_ref
    b = pl.program_id(0)
    q = q_ref[0]
    page = blk_refs[0].shape[1]

    def attend(kt, vt, s_mask, k_new, v_new, new_on):
        s = _dot(q, kt)
        if s_mask is not None:
            s = jnp.where(s_mask, s, NEG)
        s_n = jnp.where(new_on, jnp.sum(q * k_new, axis=-1, keepdims=True), NEG)
        m = jnp.maximum(jnp.max(s, axis=-1, keepdims=True), s_n)
        e = jnp.exp(s - m)
        if s_mask is not None:
            e = jnp.where(s_mask, e, 0.0)
        e_n = jnp.where(new_on, jnp.exp(s_n - m), 0.0)
        l = jnp.sum(e, axis=-1, keepdims=True) + e_n
        return (_dot_nt(e, vt) + e_n * v_new) / l

    kt = jnp.concatenate([r[0:DK, :] for r in blk_refs], axis=1).astype(BF16)
    vt = jnp.concatenate([r[DK:2 * DK, :] for r in blk_refs], axis=1).astype(BF16)
    lane = lax.broadcasted_iota(I32, (1, k_top * page), 1)
    slot = lane // page
    half = (lane % page) // NSA_BLOCK
    valid = jnp.zeros((1, k_top * page), I32)
    has_new = jnp.zeros((), jnp.bool_)
    for k in range(k_top):
        ik = idx_ref[b, k]
        ok = jnp.where(half == (ik & 1), (ik < n_blocks).astype(I32), 0)
        valid = jnp.where(slot == k, ok, valid)
        has_new = has_new | (ik == n_blocks)
    pgn = pgn_ref[0]
    o_s = attend(kt, vt, valid > 0, pgn[:, 2 * DK:3 * DK], pgn[:, 3 * DK:4 * DK], has_new)
    wn = wn_ref[0]
    o_w = attend(win_ref[0:DK, :].astype(BF16), win_ref[DK:2 * DK, :].astype(BF16), None,
                 wn[:, 0:DK], wn[:, DK:2 * DK], True)
    o_c = oc_ref[0]
    g = g_ref[0]
    outs = []
    for h in range(HEADS):
        outs.append(g[:, 3 * h:3 * h + 1] * o_c[h:h + 1] + g[:, 3 * h + 1:3 * h + 2] * o_s[h:h + 1]
                    + g[:, 3 * h + 2:3 * h + 3] * o_w[h:h + 1])
    o_ref[pl.ds(b, 1), :] = jnp.concatenate(outs, axis=1)


def _nsa_sel_decode(layer, page_table, idx, q8, pg_new, w_new, win_t, g_new, o_c, cache_t, k_top):
    BS, npages = page_table.shape
    n_blocks = 2 * npages
    page = cache_t.shape[3]
    nwin = win_t.shape[3]
    blk = jnp.minimum(idx[:, :k_top], n_blocks - 1)
    phys = jnp.take_along_axis(page_table, blk // 2, axis=1)

    def blk_spec(k):
        return pl.BlockSpec((None, None, 2 * DK, page), lambda b, ph, ix: (layer, ph[b, k], 1, 0))

    row3 = lambda w: pl.BlockSpec((1, 1, w), lambda b, pt, ix: (b, 0, 0))
    grid_spec = pltpu.PrefetchScalarGridSpec(
        num_scalar_prefetch=2,
        grid=(BS,),
        in_specs=[pl.BlockSpec((1, SUBLANES, DK), lambda b, pt, ix: (b, 0, 0)),
                  row3(256), row3(LANES),
                  pl.BlockSpec((None, None, 2 * DK, nwin), lambda b, pt, ix: (layer, b, 0, 0)),
                  row3(LANES),
                  pl.BlockSpec((1, HEADS, DK), lambda b, pt, ix: (b, 0, 0))]
                 + [blk_spec(k) for k in range(k_top)],
        out_specs=pl.BlockSpec((BS, BRANCH), lambda b, pt, ix: (0, 0)),
    )
    return pl.pallas_call(
        functools.partial(_nsa_sel_decode_kernel, k_top=k_top, n_blocks=n_blocks),
        grid_spec=grid_spec,
        out_shape=jax.ShapeDtypeStruct((BS, BRANCH), F32),
        compiler_params=_cparams(("arbitrary",)),
        name="nsa_sel_decode",
    )(phys, idx, q8, pg_new, w_new, win_t, g_new, o_c, *([cache_t] * k_top))


def _column(row, eye):
    return jnp.sum(jnp.where(eye, jnp.broadcast_to(row, eye.shape), 0.0), axis=1, keepdims=True)


def _rec_decode_kernel(xbc_ref, cbuf_ref, dt_ref, z_ref, hg_ref, s_ref, hs_ref,
                       cw_ref, cb_ref, dtb_ref, alog_ref, dsk_ref, sn_ref, lb_ref, hn_ref,
                       os_ref, oh_ref, sn_out, hs_out):
    b = pl.program_id(0)
    eye = lax.broadcasted_iota(I32, (DK, DK), 0) == lax.broadcasted_iota(I32, (DK, DK), 1)

    cbuf = cbuf_ref[...]
    xn = xbc_ref[0]
    conv = cb_ref[...] + cw_ref[SSD_CONV - 1:SSD_CONV, :] * xn
    for j in range(SSD_CONV - 1):
        conv = conv + cw_ref[j:j + 1, :] * cbuf[j:j + 1, :]
    xa = _silu(conv)
    dt = _softplus(dt_ref[0] + dtb_ref[...])
    ea = jnp.exp(dt * (-jnp.exp(alog_ref[...])))
    dsk = dsk_ref[...]
    ys = []
    for h in range(HEADS):
        g = h // (HEADS // SSD_GROUPS)
        xs = xa[:, DK * h:DK * (h + 1)]
        bm = xa[:, BRANCH + DK * g:BRANCH + DK * (g + 1)]
        cm = xa[:, BRANCH + DK * (SSD_GROUPS + g):BRANCH + DK * (SSD_GROUPS + g + 1)]
        xdt_col = _column(xs * dt[:, h:h + 1], eye)
        s_old = s_ref[h]
        ea_h = ea[:, h:h + 1]
        y_col = (jnp.sum(cm * bm, axis=-1, keepdims=True) * xdt_col
                 + jnp.sum(s_old * cm, axis=-1, keepdims=True) * ea_h)
        sn_out[0, h] = s_old * ea_h + xdt_col * bm
        y_row = jnp.sum(jnp.where(eye, jnp.broadcast_to(y_col, (DK, DK)), 0.0), axis=0, keepdims=True)
        ys.append(y_row + dsk[:, h:h + 1] * xs)
    y = jnp.concatenate(ys, axis=1)
    os_ref[pl.ds(b, 1), :] = _rms(y * _silu(z_ref[0]), sn_ref[...])

    W = BRANCH
    hg = hg_ref[0]
    q = hg[:, 0:W]
    v = hg[:, 2 * W:3 * W]
    log_f, kin = _hgrn_gates(hg[:, W:2 * W], lb_ref[...])
    ef = jnp.exp(log_f)
    hn = hn_ref[...]
    outs = []
    for h in range(HEADS):
        r = slice(DK * h, DK * (h + 1))
        s_old = hs_ref[h]
        att = jnp.sum(q[:, r] * kin[:, r], axis=-1, keepdims=True)
        y = att * v[:, r] + jnp.sum(s_old * _column(q[:, r] * ef[:, r], eye), axis=0, keepdims=True)
        hs_out[0, h] = s_old * _column(ef[:, r], eye) + _column(kin[:, r], eye) * v[:, r]
        outs.append(_rms(y, hn[:, r]))
    o = jnp.concatenate(outs, axis=1)
    oh_ref[pl.ds(b, 1), :] = o * _silu(hg[:, 3 * W:4 * W])


def _rec_decode(layer, xbc_new, conv_state, dt_new, z_new, hg_new, state_ssd, state_hgrn,
                cw, cb, dtb, alog, dsk, sn, lb, hn):
    BS = xbc_new.shape[0]
    row3 = lambda w: pl.BlockSpec((1, 1, w), lambda b: (b, 0, 0))
    st_in = pl.BlockSpec((None, None, HEADS, DK, DK), lambda b: (layer, b, 0, 0, 0))
    st_out = pl.BlockSpec((1, HEADS, DK, DK), lambda b: (b, 0, 0, 0))
    o_spec = pl.BlockSpec((BS, BRANCH), lambda b: (0, 0))
    consts = (cw, cb, dtb, alog, dsk, sn, lb, hn)
    return pl.pallas_call(
        _rec_decode_kernel,
        grid=(BS,),
        in_specs=[row3(SSD_CONV_DIM),
                  pl.BlockSpec((None, None, SSD_CONV - 1, SSD_CONV_DIM), lambda b: (layer, b, 0, 0)),
                  row3(LANES), row3(BRANCH), row3(4 * BRANCH), st_in, st_in]
                 + [_const_spec(c.shape) for c in consts],
        out_specs=[o_spec, o_spec, st_out, st_out],
        out_shape=[jax.ShapeDtypeStruct((BS, BRANCH), F32), jax.ShapeDtypeStruct((BS, BRANCH), F32),
                   jax.ShapeDtypeStruct((BS, HEADS, DK, DK), F32),
                   jax.ShapeDtypeStruct((BS, HEADS, DK, DK), F32)],
        compiler_params=_cparams(("arbitrary",)),
        name="rec_decode",
    )(xbc_new, conv_state, dt_new, z_new, hg_new, state_ssd, state_hgrn, *consts)


def _rot_cols(w):
    half = MLA_ROPE // 2
    return jnp.concatenate([-w[..., half:], w[..., :half]], axis=-1)


def _pad_cols(w, width):
    return jnp.pad(w, [(0, 0)] * (w.ndim - 1) + [(0, width - w.shape[-1])])


def _pack_w_in(w):
    sizes = (256, 384, 12, MLA_Q_LORA, MLA_KV_LORA, MLA_ROPE, 256, SSD_CONV_DIM, HEADS, 256, 256, 256, 256)
    offs = np.cumsum((0,) + sizes)
    (nsa_q, nsa_kv, nsa_g, cq, ckv, kpe, z, xbc, dt, hq, hf, hi, hgate) = (
        w[:, offs[i]:offs[i + 1]] for i in range(len(sizes)))
    gate = w[:, offs[-1]:]
    packed = jnp.concatenate([
        nsa_q, nsa_kv[:, :256], nsa_kv[:, 256:], _pad_cols(nsa_g, LANES), cq, ckv,
        _pad_cols(kpe, LANES), _pad_cols(_rot_cols(kpe), LANES), _pad_cols(dt, LANES), z, xbc,
        hq, hf, hi, hgate], axis=1)
    assert packed.shape[1] == PACKED_WIDTH
    return packed.astype(BF16), gate.astype(BF16)


def _pack_mla(wuq, wukv):
    nope = wuq[:, :, :MLA_NOPE].reshape(MLA_Q_LORA, HEADS * MLA_NOPE)
    pe = wuq[:, :, MLA_NOPE:]
    pe_p = _pad_cols(pe, LANES).reshape(MLA_Q_LORA, HEADS * LANES)
    pr_p = _pad_cols(_rot_cols(pe), LANES).reshape(MLA_Q_LORA, HEADS * LANES)
    wuq_p = jnp.concatenate([nope, pe_p, pr_p], axis=1).astype(BF16)
    eye = jnp.eye(HEADS, dtype=wukv.dtype)
    wukn = jnp.einsum('rhn,hg->hngr', wukv[:, :, :MLA_NOPE], eye).reshape(HEADS * MLA_NOPE, HEADS * MLA_KV_LORA)
    wv = jnp.einsum('rhv,hg->hrgv', wukv[:, :, MLA_NOPE:], eye).reshape(HEADS * MLA_KV_LORA, HEADS * MLA_V)
    return wuq_p, wukn.astype(BF16), wv.astype(BF16)


def _rope_table(positions):
    half = MLA_ROPE // 2
    freq = ROPE_THETA ** (-jnp.arange(half, dtype=F32) / half)
    ang = positions.astype(F32)[:, None] * freq
    cos = jnp.concatenate([jnp.cos(ang), jnp.cos(ang)], axis=1)
    sin = jnp.concatenate([jnp.sin(ang), jnp.sin(ang)], axis=1)
    return jnp.concatenate([_pad_cols(cos, LANES), _pad_cols(sin, LANES)], axis=1)


def _pad_row(v, width=LANES):
    return _pad_cols(v.reshape(1, -1).astype(F32), width)


def kernel(x_prompt, x_sample, cache_nsa_kv, cache_mla, cache_nsa_win, state_ssd, state_ssd_conv, state_hgrn, page_table, norm_g, ffn_w_in, ffn_w_out, w_in, mla_q_norm, mla_kv_norm, mla_w_uq, mla_w_ukv, ssd_conv_w, ssd_conv_b, ssd_dt_bias, ssd_a_log, ssd_d, ssd_norm, hg_lb_logits, hg_norm, w_branch, w_out):
    BP, T, D = x_prompt.shape
    BS = x_sample.shape[0]
    depth = w_in.shape[0]
    n_pool, page = cache_nsa_kv.shape[1], cache_nsa_kv.shape[2]
    npages = page_table.shape[1]
    past = npages * page
    n_win = cache_nsa_win.shape[2]
    rows_p = BP * T
    assert D == D_MODEL and x_sample.shape[1] == 1 and BS % SUBLANES == 0
    assert n_win == min(NSA_WINDOW, past) and past % NSA_BLOCK == 0
    k_top = min(NSA_TOPK, past // NSA_BLOCK + 1)

    xp = x_prompt.reshape(rows_p, D)
    xs = x_sample.reshape(BS, D)
    cs_p = _rope_table(jnp.tile(jnp.arange(T, dtype=I32), BP))
    cs_s = _rope_table(jnp.full((BS,), past, I32))
    lbp = jax.nn.softmax(hg_lb_logits.astype(F32), axis=0)
    lower_bounds = jnp.cumsum(lbp, axis=0) - lbp[0]

    nsa_t = jnp.transpose(cache_nsa_kv, (0, 1, 3, 4, 2)).reshape(depth, n_pool, 4 * DK, page)
    mla_t = jnp.transpose(cache_mla, (0, 1, 3, 2))
    win_t = jnp.transpose(cache_nsa_win, (0, 1, 3, 4, 2)).reshape(depth, BS, 2 * DK, n_win)

    def row3(a):
        return a.reshape(BS, 1, a.shape[1])

    outs = [[] for _ in range(12)]
    for l in range(depth):
        g = norm_g[l].astype(F32)
        grow = lambda i: g[i].reshape(1, D)
        wi = ffn_w_in[l].astype(BF16)
        wo = ffn_w_out[l].astype(BF16)
        w_pack, w_gate = _pack_w_in(w_in[l])
        wuq_p, wukn, wv = _pack_mla(mla_w_uq[l], mla_w_ukv[l])
        mqn = mla_q_norm[l].reshape(1, -1)
        mkvn = mla_kv_norm[l].reshape(1, -1)

        xp = _ffn(xp, grow(0), grow(1), wi[0], wo[0])
        xs = _ffn(xs, grow(0), grow(1), wi[0], wo[0])
        (nsa_q, nsa_pg, nsa_w, nsa_g, qlat, qpe, mla_new, ssd_z, ssd_xbc, ssd_dt, hg) = _inproj(
            xp, grow(2), cs_p, w_pack, mqn, mkvn, wuq_p, wukn)
        (nsa_q_s, nsa_pg_s, nsa_w_s, nsa_g_s, qlat_s, qpe_s, mla_new_s, ssd_z_s, ssd_xbc_s, ssd_dt_s,
         hg_s) = _inproj(xs, grow(2), cs_s, w_pack, mqn, mkvn, wuq_p, wukn)

        o_nsa = _nsa_prompt(nsa_q, nsa_pg, nsa_w, nsa_g, BP, T)
        pad8 = lambda a: jnp.pad(a, ((0, 0), (0, SUBLANES - HEADS), (0, 0)))
        q4 = nsa_q_s.reshape(BS, HEADS, DK)
        q8 = pad8(q4)
        o_c, imp = _nsa_cmp_decode(l, page_table, jnp.transpose(q4, (0, 2, 1)), nsa_t, past)
        idx = _topk(imp.reshape(BS, 2 * LANES), past, k_top)
        o_nsa_s = _nsa_sel_decode(l, page_table, idx, q8, row3(nsa_pg_s), row3(nsa_w_s), win_t,
                                  row3(nsa_g_s), o_c, nsa_t, k_top)

        o_mla = _mla_prompt(qlat, qpe, mla_new, wv, BP, T)
        q_cat = jnp.concatenate([qlat_s.reshape(BS, HEADS, LANES),
                                 qpe_s.reshape(BS, HEADS, LANES)[:, :, :MLA_ROPE]], axis=-1)
        o_mla_s = _mla_decode(l, page_table, pad8(q_cat), jnp.transpose(q_cat, (0, 2, 1)),
                              row3(mla_new_s), mla_t, wv)

        cw = ssd_conv_w[l].astype(F32)
        cb = ssd_conv_b[l].reshape(1, -1).astype(F32)
        dtb = _pad_row(ssd_dt_bias[l])
        alog = _pad_row(ssd_a_log[l])
        dsk = _pad_row(ssd_d[l])
        sn = ssd_norm[l].reshape(1, -1).astype(F32)
        lb = lower_bounds[l].reshape(1, -1)
        hn = hg_norm[l].reshape(1, -1).astype(F32)
        o_ssd, ssd_p = _ssd_prompt(ssd_xbc, ssd_dt, ssd_z, cw, cb, dtb, alog, dsk, sn, BP, T)
        o_hg, hgrn_p = _hgrn_prompt(hg, lb, hn, BP, T)
        o_ssd_s, o_hg_s, ssd_s, hgrn_s = _rec_decode(
            l, row3(ssd_xbc_s), state_ssd_conv, row3(ssd_dt_s), row3(ssd_z_s), row3(hg_s), state_ssd,
            state_hgrn, cw, cb, dtb, alog, dsk, sn, lb, hn)

        wbr = w_branch[l].astype(BF16)
        wout = w_out[l].astype(BF16)
        xp = _merge(xp, (o_nsa, o_mla, o_ssd, o_hg), grow(2), grow(3), w_gate, wbr, wout)
        xs = _merge(xs, (o_nsa_s, o_mla_s, o_ssd_s, o_hg_s), grow(2), grow(3), w_gate, wbr, wout)
        xp = _ffn(xp, grow(4), grow(5), wi[1], wo[1])
        xs = _ffn(xs, grow(4), grow(5), wi[1], wo[1])

        n_keep = min(NSA_WINDOW, T)
        layer_out = (
            nsa_pg.reshape(BP, T, 4, DK), nsa_pg_s.reshape(BS, 1, 4, DK),
            mla_new.reshape(BP, T, MLA_CACHE), mla_new_s.reshape(BS, 1, MLA_CACHE),
            nsa_w.reshape(BP, T, 2, DK)[:, T - n_keep:],
            jnp.concatenate([cache_nsa_win[l], nsa_w_s.reshape(BS, 1, 2, DK)], axis=1)[:, 1:],
            ssd_p, ssd_s,
            ssd_xbc.reshape(BP, T, SSD_CONV_DIM)[:, T - (SSD_CONV - 1):],
            jnp.concatenate([state_ssd_conv[l], ssd_xbc_s.reshape(BS, 1, SSD_CONV_DIM)], axis=1)[:, 1:],
            hgrn_p, hgrn_s)
        for acc, o in zip(outs, layer_out):
            acc.append(o)

    stacked = tuple(jnp.stack(o) for o in outs)
    return (xp.reshape(BP, T, D), xs.reshape(BS, 1, D)) + stacked
```

```python
import functools
import math

import numpy as np
import jax
import jax.numpy as jnp
from jax import lax
from jax.experimental import pallas as pl
from jax.experimental.pallas import tpu as pltpu

F32 = jnp.float32
BF16 = jnp.bfloat16
I32 = jnp.int32
HIGHEST = lax.Precision.HIGHEST

EPS = 1e-6
NEG = -1e30
LB_FLOOR = 1e-20
ROPE_THETA = 10000.0

D_MODEL = 1024
D_FF = 2816
HEADS = 4
DK = 64
BRANCH = HEADS * DK
NSA_BLOCK = 64
NSA_TOPK = 16
NSA_WINDOW = 512
NSA_FORCED_BONUS = float(HEADS + 1)
MLA_Q_LORA = 256
MLA_KV_LORA = 128
MLA_NOPE = 64
MLA_ROPE = 32
MLA_V = 64
MLA_CACHE = MLA_KV_LORA + MLA_ROPE
SSD_GROUPS = 2
SSD_CONV = 4
SSD_CONV_DIM = BRANCH + 2 * SSD_GROUPS * DK
CHUNK = 128
QB = 128

LANES = 128
SUBLANES = 8
VMEM_LIMIT_BYTES = 56 * 1024 * 1024

_SEG = {}
_off = 0
for _name, _w in (("nsa_q", 256), ("nsa_pg", 256), ("nsa_w", 128), ("nsa_g", 128),
                  ("mla_cq", 256), ("mla_ckv", 128), ("mla_kpe", 128), ("mla_kpr", 128),
                  ("ssd_dt", 128), ("ssd_z", 256), ("ssd_xbc", 512), ("hg", 1024)):
    _SEG[_name] = (_off, _off + _w)
    _off += _w
PACKED_WIDTH = _off


def _cparams(sem):
    return pltpu.CompilerParams(dimension_semantics=sem, vmem_limit_bytes=VMEM_LIMIT_BYTES)


def _rms(x, g):
    return x * lax.rsqrt(jnp.mean(x * x, axis=-1, keepdims=True) + EPS) * g


def _dot(a, b):
    return jnp.dot(a.astype(BF16), b.astype(BF16), preferred_element_type=F32)


def _dot_nt(a, b):
    return lax.dot_general(a.astype(BF16), b.astype(BF16), (((1,), (1,)), ((), ())),
                           preferred_element_type=F32)


def _dot_rows_split(a, b):
    half = a.shape[0] // 2
    return jnp.concatenate([_dot(a[:half], b), _dot(a[half:], b)], axis=0)


def _dot_nt_rows_split(a, b):
    half = a.shape[0] // 2
    return jnp.concatenate([_dot_nt(a[:half], b), _dot_nt(a[half:], b)], axis=0)


def _split3(a):
    a1 = a.astype(BF16)
    r = a - a1.astype(F32)
    a2 = r.astype(BF16)
    return a1, a2, (r - a2.astype(F32)).astype(BF16)


def _dot_f32_exact_lhs(sel, b):
    sel = sel.astype(BF16)
    return sum(jnp.dot(sel, t, preferred_element_type=F32) for t in _split3(b))


def _dot_f32_exact_rhs(a, sel):
    sel = sel.astype(BF16)
    return sum(jnp.dot(t, sel, preferred_element_type=F32) for t in _split3(a))


def _dot_f32(a, b):
    return jnp.dot(a, b, precision=HIGHEST, preferred_element_type=F32)


def _dot_nt_f32(a, b):
    return lax.dot_general(a, b, (((1,), (1,)), ((), ())), precision=HIGHEST,
                           preferred_element_type=F32)


def _sigmoid(x):
    return 1.0 / (1.0 + jnp.exp(-x))


def _silu(x):
    return x * _sigmoid(x)


def _softplus(x):
    return jnp.maximum(x, 0.0) + jnp.log1p(jnp.exp(-jnp.abs(x)))


def _masked_softmax(s, mask):
    s = jnp.where(mask, s, NEG)
    m = jnp.max(s, axis=-1, keepdims=True)
    e = jnp.exp(s - m)
    r = 1.0 / jnp.sum(e, axis=-1, keepdims=True)
    return jnp.where(mask, e * r, 0.0)


def _unnormalised_softmax(s, mask):
    s = jnp.where(mask, s, NEG)
    e = jnp.exp(s - jnp.max(s, axis=-1, keepdims=True))
    return e, 1.0 / jnp.sum(e, axis=-1, keepdims=True)


def _row_tile(rows, cap=864):
    best = SUBLANES
    for t in range(SUBLANES, min(rows, cap) + 1, SUBLANES):
        if rows % t == 0:
            best = t
    assert rows % best == 0
    return best


def _const_spec(shape):
    nd = len(shape)
    return pl.BlockSpec(shape, lambda *_: (0,) * nd)


MXU_TILE = 256
FFN_BOUNDS = (0, 6 * MXU_TILE, D_FF)


def _ffn_kernel(x_ref, ga_ref, gb_ref, wi_ref, wo_ref, o_ref):
    x = x_ref[...]
    h = _rms(x, ga_ref[...]).astype(BF16)
    acc = None
    for a, b in zip(FFN_BOUNDS[:-1], FFN_BOUNDS[1:]):
        gate = jnp.dot(h, wi_ref[:, a:b], preferred_element_type=F32)
        up = jnp.dot(h, wi_ref[:, D_FF + a:D_FF + b], preferred_element_type=F32)
        part = _dot(_silu(gate) * up, wo_ref[a:b, :])
        acc = part if acc is None else acc + part
    o_ref[...] = x + 0.5 * _rms(acc, gb_ref[...])


def _resident_spec(shape):
    nd = len(shape)
    return pl.BlockSpec(shape, lambda *_: (0,) * nd, pipeline_mode=pl.Buffered(1))


def _ffn(x, ga, gb, wi, wo):
    rows = x.shape[0]
    tm = _row_tile(rows, cap=512)
    assert all(b % MXU_TILE == 0 for b in FFN_BOUNDS)
    return pl.pallas_call(
        _ffn_kernel,
        grid=(rows // tm,),
        in_specs=[
            pl.BlockSpec((tm, D_MODEL), lambda i: (i, 0)),
            _const_spec((1, D_MODEL)),
            _const_spec((1, D_MODEL)),
            _resident_spec(wi.shape),
            _resident_spec(wo.shape),
        ],
        out_specs=pl.BlockSpec((tm, D_MODEL), lambda i: (i, 0)),
        out_shape=jax.ShapeDtypeStruct((rows, D_MODEL), F32),
        compiler_params=_cparams(("parallel",)),
        name="ffn",
    )(x, ga, gb, wi, wo)


def _inproj_kernel(x_ref, g_ref, cs_ref, w_ref, mqn_ref, mkvn_ref, wuq_ref, wukn_ref,
                   q_o, pg_o, w_o, g_o, qlat_o, qpe_o, mla_o, z_o, xbc_o, dt_o, hg_o):
    hn = _rms(x_ref[...], g_ref[...]).astype(BF16)

    def seg(first, last=None):
        a, b = _SEG[first][0], _SEG[last or first][1]
        return jnp.dot(hn, w_ref[:, a:b], preferred_element_type=F32)

    q_o[...] = seg("nsa_q") * (DK ** -0.5)
    pg_o[...] = seg("nsa_pg")
    w_g = seg("nsa_w", "nsa_g")
    w_o[...] = w_g[:, 0:LANES]
    g_o[...] = _sigmoid(w_g[:, LANES:2 * LANES])
    z_o[...] = seg("ssd_z")
    xbc_o[...] = seg("ssd_xbc")
    hg_o[...] = seg("hg")
    ckv_kpe = seg("mla_ckv", "mla_kpe")
    kpr_dt = seg("mla_kpr", "ssd_dt")
    dt_o[...] = kpr_dt[:, LANES:2 * LANES]

    cos_p = cs_ref[:, 0:LANES]
    sin_p = cs_ref[:, LANES:2 * LANES]
    cqn = _rms(seg("mla_cq"), mqn_ref[...])
    qf = _dot(cqn, wuq_ref[...])
    nope_w = HEADS * MLA_NOPE
    qlat_o[...] = _dot(qf[:, 0:nope_w], wukn_ref[...])
    for h in range(HEADS):
        pe = qf[:, nope_w + LANES * h: nope_w + LANES * (h + 1)]
        pr = qf[:, nope_w + LANES * (HEADS + h): nope_w + LANES * (HEADS + h + 1)]
        qpe_o[:, LANES * h:LANES * (h + 1)] = pe * cos_p + pr * sin_p
    mla_o[:, 0:MLA_KV_LORA] = _rms(ckv_kpe[:, 0:LANES], mkvn_ref[...])
    kpe = ckv_kpe[:, LANES:2 * LANES] * cos_p + kpr_dt[:, 0:LANES] * sin_p
    mla_o[:, MLA_KV_LORA:MLA_CACHE] = kpe[:, 0:MLA_ROPE]


_INPROJ_OUT_WIDTHS = (256, 256, 128, 128, 512, 512, MLA_CACHE, 256, 512, 128, 1024)


def _inproj(x, g, cs, w, mqn, mkvn, wuq, wukn):
    rows = x.shape[0]
    tm = _row_tile(rows)
    row_spec = lambda c: pl.BlockSpec((tm, c), lambda i: (i, 0))
    return pl.pallas_call(
        _inproj_kernel,
        grid=(rows // tm,),
        in_specs=[row_spec(D_MODEL), _const_spec((1, D_MODEL)), row_spec(2 * LANES),
                  _const_spec(w.shape), _const_spec(mqn.shape), _const_spec(mkvn.shape),
                  _const_spec(wuq.shape), _const_spec(wukn.shape)],
        out_specs=[row_spec(c) for c in _INPROJ_OUT_WIDTHS],
        out_shape=[jax.ShapeDtypeStruct((rows, c), F32) for c in _INPROJ_OUT_WIDTHS],
        compiler_params=_cparams(("parallel",)),
        name="inproj",
    )(x, g, cs, w, mqn, mkvn, wuq, wukn)


def _merge_kernel(x_ref, b0_ref, b1_ref, b2_ref, b3_ref, g2_ref, g3_ref, wg_ref, wbr_ref, wout_ref,
                  o_ref):
    x = x_ref[...]
    hn = _rms(x, g2_ref[...]).astype(BF16)
    acc = None
    for k, b_ref in enumerate((b0_ref, b1_ref, b2_ref, b3_ref)):
        gate = _sigmoid(jnp.dot(hn, wg_ref[:, D_MODEL * k:D_MODEL * (k + 1)],
                                preferred_element_type=F32))
        u = _dot(b_ref[...], wbr_ref[k])
        acc = gate * u if acc is None else acc + gate * u
    out = _dot(acc, wout_ref[...])
    o_ref[...] = x + _rms(out, g3_ref[...])


def _merge(x, branches, g2, g3, wgate, wbr, wout):
    rows = x.shape[0]
    tm = _row_tile(rows, cap=512)
    row_spec = lambda c: pl.BlockSpec((tm, c), lambda i: (i, 0))
    return pl.pallas_call(
        _merge_kernel,
        grid=(rows // tm,),
        in_specs=[row_spec(D_MODEL)] + [row_spec(BRANCH)] * 4 +
                 [_const_spec((1, D_MODEL)), _const_spec((1, D_MODEL)),
                  _const_spec(wgate.shape), _const_spec(wbr.shape), _const_spec(wout.shape)],
        out_specs=row_spec(D_MODEL),
        out_shape=jax.ShapeDtypeStruct((rows, D_MODEL), F32),
        compiler_params=_cparams(("parallel",)),
        name="merge",
    )(x, *branches, g2, g3, wgate, wbr, wout)


KEY_STEP = 256


def _key_extents(T):
    his = sorted({min(T, KEY_STEP * (c + 1)) for c in range(-(-T // KEY_STEP))})
    return list(zip([0] + his[:-1], his))


def _stack_heads(x, width, take):
    return jnp.concatenate([x[:, width * h: width * h + take] for h in range(HEADS)], axis=0)


def _nsa_prompt_kernel(q_ref, pg_ref, wk_ref, g_ref, e_ref, o_ref, *, T, band):
    nb = T // NSA_BLOCK
    k_top = min(NSA_TOPK, nb)
    q0 = pl.program_id(1) * QB
    q = q_ref[...]
    zeros = jnp.zeros((QB, DK), F32)
    q_rows = jnp.concatenate(
        [jnp.concatenate([q[:, DK * h:DK * (h + 1)], zeros], axis=1) for h in range(HEADS)],
        axis=0).astype(BF16)
    pos = q0 + lax.broadcasted_iota(I32, (QB, 1), 0)
    lane = lax.broadcasted_iota(I32, (QB, LANES), 1)

    def per_head(fn):
        return jnp.concatenate([fn(h) for h in range(HEADS)], axis=0)

    def attend_values(s, mask, kv):
        parts = [_unnormalised_softmax(s[QB * h:QB * (h + 1)], mask) for h in range(HEADS)]
        e = jnp.concatenate([p[0] for p in parts], axis=0)
        r = jnp.concatenate([p[1] for p in parts], axis=0)
        return _dot_rows_split(e, kv) * r

    cm = jnp.sum(pg_ref[:, 0:LANES].reshape(nb, NSA_BLOCK, LANES), axis=1) * (1.0 / NSA_BLOCK)
    if nb < LANES:
        cm = jnp.concatenate([cm, jnp.zeros((LANES - nb, LANES), F32)], axis=0)
    cm = cm.astype(BF16)
    s_c = _dot_nt_rows_split(q_rows, cm)
    cmask = (lane + 1) * NSA_BLOCK - 1 <= pos
    p_c = per_head(lambda h: _masked_softmax(s_c[QB * h:QB * (h + 1)], cmask))
    o_c = _dot_rows_split(p_c, cm)

    cur = pos // NSA_BLOCK
    imp = p_c[0:QB] + p_c[QB:2 * QB] + p_c[2 * QB:3 * QB] + p_c[3 * QB:4 * QB]
    forced = (lane == 0) | (lane == cur) | (lane == cur - 1)
    imp = imp + jnp.where(forced, NSA_FORCED_BONUS, 0.0)
    imp = jnp.where(lane > cur, -1.0, imp)
    rank = jnp.zeros((QB, LANES), F32)
    for n in range(nb):
        c = imp[:, n:n + 1]
        beats = (c > imp) | ((c == imp) & (lane > n))
        rank = rank + jnp.where(beats, 1.0, 0.0)
    sel = jnp.where(rank < float(k_top), 1.0, 0.0)

    def selected(tk):
        in_sel = jnp.dot(sel.astype(BF16), e_ref[:, 0:tk], preferred_element_type=F32)
        kpos = lax.broadcasted_iota(I32, (QB, tk), 1)
        smask = (in_sel > 0.5) & (kpos <= pos)
        kv_s = pg_ref[0:tk, LANES:2 * LANES].astype(BF16)
        s_s = _dot_nt(q_rows, kv_s)
        return attend_values(s_s, smask, kv_s)

    start = pl.multiple_of(jnp.clip(q0 - NSA_WINDOW, 0, T - band), LANES)
    kv_w = wk_ref[pl.ds(start, band), :].astype(BF16)
    kposw = start + lax.broadcasted_iota(I32, (QB, band), 1)
    wmask = (kposw <= pos) & (kposw >= pos - NSA_WINDOW)
    o_w = attend_values(_dot_nt(q_rows, kv_w), wmask, kv_w)

    g = g_ref[...]

    def finish(o_s):
        outs = []
        for h in range(HEADS):
            r = slice(QB * h, QB * (h + 1))
            comb = (g[:, 3 * h:3 * h + 1] * o_c[r] + g[:, 3 * h + 1:3 * h + 2] * o_s[r]
                    + g[:, 3 * h + 2:3 * h + 3] * o_w[r])
            outs.append(comb[:, DK:2 * DK])
        o_ref[...] = jnp.concatenate(outs, axis=1)

    for lo, tk in _key_extents(T):
        @pl.when((q0 + QB > lo) & (q0 + QB <= tk))
        def _(tk=tk):
            finish(selected(tk))


def _nsa_prompt(q, pg, wkv, g, B, T):
    assert T % QB == 0 and T // NSA_BLOCK <= LANES
    rows = B * T
    nq = T // QB
    band = min(NSA_WINDOW + QB, T)
    nb = T // NSA_BLOCK
    expand = (np.arange(LANES)[:, None] == (np.arange(T)[None, :] // NSA_BLOCK)) & (np.arange(LANES)[:, None] < nb)
    expand = jnp.asarray(expand, BF16)
    return pl.pallas_call(
        functools.partial(_nsa_prompt_kernel, T=T, band=band),
        grid=(B, nq),
        in_specs=[pl.BlockSpec((QB, BRANCH), lambda b, i: (b * nq + i, 0)),
                  pl.BlockSpec((T, 256), lambda b, i: (b, 0)),
                  pl.BlockSpec((T, LANES), lambda b, i: (b, 0)),
                  pl.BlockSpec((QB, LANES), lambda b, i: (b * nq + i, 0)),
                  _const_spec((LANES, T))],
        out_specs=pl.BlockSpec((QB, BRANCH), lambda b, i: (b * nq + i, 0)),
        out_shape=jax.ShapeDtypeStruct((rows, BRANCH), F32),
        compiler_params=_cparams(("parallel", "parallel")),
        name="nsa_prompt",
    )(q, pg, wkv, g, expand)


def _mla_prompt_kernel(qlat_ref, qpe_ref, kv_ref, wv_ref, o_ref, *, T):
    q0 = pl.program_id(1) * QB
    q_cat = jnp.concatenate([_stack_heads(qlat_ref[...], LANES, LANES),
                             _stack_heads(qpe_ref[...], LANES, MLA_ROPE)], axis=1).astype(BF16)
    scale = (MLA_NOPE + MLA_ROPE) ** -0.5
    pos = q0 + lax.broadcasted_iota(I32, (QB, 1), 0)

    def attend(tk):
        kv = kv_ref[0:tk, :].astype(BF16)
        ckv = kv[:, 0:MLA_KV_LORA]
        s = _dot_nt(q_cat, kv) * scale
        mask = lax.broadcasted_iota(I32, (QB, tk), 1) <= pos
        parts = [_unnormalised_softmax(s[QB * h:QB * (h + 1)], mask) for h in range(HEADS)]
        e = jnp.concatenate([p[0] for p in parts], axis=0)
        r = jnp.concatenate([p[1] for p in parts], axis=0)
        o_lat = _dot_rows_split(e, ckv) * r
        o_cat = jnp.concatenate([o_lat[QB * h:QB * (h + 1)] for h in range(HEADS)], axis=1)
        o_ref[...] = _dot_rows_split(o_cat, wv_ref[...])

    for lo, tk in _key_extents(T):
        @pl.when((q0 + QB > lo) & (q0 + QB <= tk))
        def _(tk=tk):
            attend(tk)


def _mla_prompt(qlat, qpe, mla_new, wv, B, T):
    nq = T // QB
    rows = B * T
    return pl.pallas_call(
        functools.partial(_mla_prompt_kernel, T=T),
        grid=(B, nq),
        in_specs=[pl.BlockSpec((QB, 512), lambda b, i: (b * nq + i, 0)),
                  pl.BlockSpec((QB, 512), lambda b, i: (b * nq + i, 0)),
                  pl.BlockSpec((T, MLA_CACHE), lambda b, i: (b, 0)),
                  _const_spec(wv.shape)],
        out_specs=pl.BlockSpec((QB, BRANCH), lambda b, i: (b * nq + i, 0)),
        out_shape=jax.ShapeDtypeStruct((rows, BRANCH), F32),
        compiler_params=_cparams(("parallel", "parallel")),
        name="mla_prompt",
    )(qlat, qpe, mla_new, wv)


def _head_lanes(x_cols, shape):
    lane = lax.broadcasted_iota(I32, shape, 1)
    out = jnp.broadcast_to(x_cols[:, HEADS - 1:HEADS], shape)
    for h in range(HEADS - 2, -1, -1):
        out = jnp.where(lane < DK * (h + 1), jnp.broadcast_to(x_cols[:, h:h + 1], shape), out)
    return out


def _ssd_prompt_kernel(xbc_ref, dt_ref, z_ref, cw_ref, cb_ref, dtb_ref, alog_ref, dsk_ref, sn_ref,
                       o_ref, st_ref, buf_ref, s_ref):
    c = pl.program_id(1)
    C = CHUNK

    @pl.when(c == 0)
    def _():
        buf_ref[0:SUBLANES, :] = jnp.zeros((SUBLANES, SSD_CONV_DIM), F32)
        s_ref[...] = jnp.zeros_like(s_ref)

    x = xbc_ref[...]
    buf_ref[SUBLANES:SUBLANES + C, :] = x
    conv = cb_ref[...] + cw_ref[SSD_CONV - 1:SSD_CONV, :] * x
    for j in range(SSD_CONV - 1):
        conv = conv + cw_ref[j:j + 1, :] * buf_ref[pl.ds(SUBLANES - (SSD_CONV - 1) + j, C), :]
    buf_ref[0:SUBLANES, :] = x[C - SUBLANES:C, :]
    xa = _silu(conv)
    xs = xa[:, 0:BRANCH]
    dt = _softplus(dt_ref[...] + dtb_ref[...])
    a = dt * (-jnp.exp(alog_ref[...]))
    row = lax.broadcasted_iota(I32, (C, C), 0)
    col = lax.broadcasted_iota(I32, (C, C), 1)
    tril = row >= col
    tri = jnp.where(tril, 1.0, 0.0)
    cum_c = _dot_f32_exact_lhs(tri, a)
    cum_r = _dot_f32_exact_rhs(a.T, jnp.where(row <= col, 1.0, 0.0))
    xdt = xs * _head_lanes(dt, (C, BRANCH))
    xdt_t = xdt.T
    dsk = dsk_ref[...]
    ys = []
    for h in range(HEADS):
        g = h // (HEADS // SSD_GROUPS)
        bm = xa[:, BRANCH + DK * g:BRANCH + DK * (g + 1)]
        cm = xa[:, BRANCH + DK * (SSD_GROUPS + g):BRANCH + DK * (SSD_GROUPS + g + 1)]
        ch = cum_c[:, h:h + 1]
        decay = jnp.exp(jnp.where(tril, ch - cum_r[h:h + 1, :], NEG))
        scores = _dot_nt(cm, bm) * decay
        xdt_h = xdt[:, DK * h:DK * (h + 1)]
        s_old = s_ref[h]
        y = _dot(scores, xdt_h) + _dot_nt(cm, s_old) * jnp.exp(ch)
        last = cum_c[C - 1:C, h:h + 1]
        w = jnp.exp(last - ch)
        s_ref[h] = s_old * jnp.exp(last) + _dot(xdt_t[DK * h:DK * (h + 1), :], bm * w)
        ys.append(y + dsk[:, h:h + 1] * xs[:, DK * h:DK * (h + 1)])
    y = jnp.concatenate(ys, axis=1)
    o_ref[...] = _rms(y * _silu(z_ref[...]), sn_ref[...])

    @pl.when(c == pl.num_programs(1) - 1)
    def _():
        st_ref[0] = s_ref[...]


def _ssd_prompt(xbc, dt, z, cw, cb, dtb, alog, dsk, sn, B, T):
    assert T % CHUNK == 0
    nc = T // CHUNK
    rows = B * T
    row_spec = lambda w: pl.BlockSpec((CHUNK, w), lambda b, c: (b * nc + c, 0))
    return pl.pallas_call(
        _ssd_prompt_kernel,
        grid=(B, nc),
        in_specs=[row_spec(SSD_CONV_DIM), row_spec(LANES), row_spec(BRANCH),
                  _const_spec(cw.shape), _const_spec(cb.shape), _const_spec(dtb.shape),
                  _const_spec(alog.shape), _const_spec(dsk.shape), _const_spec(sn.shape)],
        out_specs=[row_spec(BRANCH), pl.BlockSpec((1, HEADS, DK, DK), lambda b, c: (b, 0, 0, 0))],
        out_shape=[jax.ShapeDtypeStruct((rows, BRANCH), F32),
                   jax.ShapeDtypeStruct((B, HEADS, DK, DK), F32)],
        scratch_shapes=[pltpu.VMEM((SUBLANES + CHUNK, SSD_CONV_DIM), F32),
                        pltpu.VMEM((HEADS, DK, DK), F32)],
        compiler_params=_cparams(("parallel", "arbitrary")),
        name="ssd_prompt",
    )(xbc, dt, z, cw, cb, dtb, alog, dsk, sn)


def _hgrn_gates(fr, lb):
    log_sig = jnp.minimum(fr, 0.0) - jnp.log1p(jnp.exp(-jnp.abs(fr)))
    a = jnp.log(jnp.maximum(lb, LB_FLOOR))
    b = jnp.log1p(-lb) + log_sig
    log_f = jnp.maximum(a, b) + jnp.log1p(jnp.exp(-jnp.abs(a - b)))
    return log_f, (1.0 - lb) * _sigmoid(-fr)


def _block_reference_rows(G, m):
    C = G.shape[0]
    if m >= 4:
        parts = []
        for p in range(C // (2 * m)):
            r = p * 2 * m + m - 1
            parts.append(jnp.broadcast_to(G[r:r + 1, :], (2 * m, G.shape[1])))
        return parts[0] if len(parts) == 1 else jnp.concatenate(parts, axis=0)
    t = lax.broadcasted_iota(I32, G.shape, 0)
    if m == 1:
        return jnp.where((t & 1) == 1, pltpu.roll(G, 1, 0), G)
    r = t & 3
    return jnp.where(r == 0, pltpu.roll(G, C - 1, 0),
                     jnp.where(r == 1, G, jnp.where(r == 2, pltpu.roll(G, 1, 0), pltpu.roll(G, 2, 0))))


def _hgrn_prompt_kernel(hg_ref, lb_ref, hn_ref, seg_ref, o_ref, st_ref, s_ref):
    c = pl.program_id(1)
    C = CHUNK
    W = BRANCH

    @pl.when(c == 0)
    def _():
        s_ref[...] = jnp.zeros_like(s_ref)

    q = hg_ref[:, 0:W]
    v = hg_ref[:, 2 * W:3 * W]
    log_f, kin = _hgrn_gates(hg_ref[:, W:2 * W], lb_ref[...])
    row = lax.broadcasted_iota(I32, (C, C), 0)
    col = lax.broadcasted_iota(I32, (C, C), 1)
    G = _dot_f32_exact_lhs(jnp.where(row >= col, 1.0, 0.0), log_f)
    lane_head = lax.broadcasted_iota(I32, (C, W), 1) // DK
    t_idx = lax.broadcasted_iota(I32, (C, W), 0)

    def stack_heads(a):
        return jnp.concatenate([jnp.where(lane_head == h, a, 0.0) for h in range(HEADS)],
                               axis=0).astype(BF16)

    t4 = lax.broadcasted_iota(I32, (HEADS * C, C), 0) & (C - 1)
    s4 = lax.broadcasted_iota(I32, (HEADS * C, C), 1)
    att = jnp.where(t4 == s4, _dot_nt_rows_split(stack_heads(q), kin), 0.0)
    m = C // 2
    while m >= 1:
        R = _block_reference_rows(G, m)
        upper = (t_idx & m) != 0
        A = jnp.where(upper, q * jnp.exp(jnp.minimum(G - R, 0.0)), 0.0)
        Bm = jnp.where(upper, 0.0, kin * jnp.exp(jnp.minimum(R - G, 0.0)))
        lm = int(math.log2(m))
        pair = ((t4 >> lm) ^ (s4 >> lm)) == 1
        att = att + jnp.where(pair & (t4 > s4), _dot_nt_rows_split(stack_heads(A), Bm), 0.0)
        m //= 2
    y4 = _dot_rows_split(att, v)
    y = jnp.zeros((C, W), F32)
    for h in range(HEADS):
        y = y + jnp.where(lane_head == h, y4[C * h:C * (h + 1)], 0.0)
    s_old = s_ref[...]
    y = y + _dot(q * jnp.exp(G), s_old)
    last = G[C - 1:C, :]
    ke_t = (kin * jnp.exp(last - G)).T
    G_t = G.T
    blk = (lax.broadcasted_iota(I32, (W, W), 0) // DK) == (lax.broadcasted_iota(I32, (W, W), 1) // DK)
    s_ref[...] = s_old * jnp.exp(G_t[:, C - 1:C]) + jnp.where(blk, _dot(ke_t, v), 0.0)
    ms = _dot_f32_exact_rhs(y * y, seg_ref[...])
    o = y * lax.rsqrt(ms + EPS) * hn_ref[...]
    o_ref[...] = o * _silu(hg_ref[:, 3 * W:4 * W])

    @pl.when(c == pl.num_programs(1) - 1)
    def _():
        for h in range(HEADS):
            st_ref[0, h] = s_ref[DK * h:DK * (h + 1), DK * h:DK * (h + 1)]


def _hgrn_prompt(hg, lb, hn, B, T):
    nc = T // CHUNK
    rows = B * T
    seg = (np.arange(BRANCH)[:, None] // DK == np.arange(BRANCH)[None, :] // DK) / float(DK)
    seg = jnp.asarray(seg, F32)
    row_spec = lambda w: pl.BlockSpec((CHUNK, w), lambda b, c: (b * nc + c, 0))
    return pl.pallas_call(
        _hgrn_prompt_kernel,
        grid=(B, nc),
        in_specs=[row_spec(4 * BRANCH), _const_spec(lb.shape), _const_spec(hn.shape),
                  _const_spec(seg.shape)],
        out_specs=[row_spec(BRANCH), pl.BlockSpec((1, HEADS, DK, DK), lambda b, c: (b, 0, 0, 0))],
        out_shape=[jax.ShapeDtypeStruct((rows, BRANCH), F32),
                   jax.ShapeDtypeStruct((B, HEADS, DK, DK), F32)],
        scratch_shapes=[pltpu.VMEM((BRANCH, BRANCH), F32)],
        compiler_params=_cparams(("parallel", "arbitrary")),
        name="hgrn_prompt",
    )(hg, lb, hn, seg)


PAGES_PER_STEP = 64


def _mla_decode_kernel(pt_ref, q_ref, qt_ref, new_ref, *rest, n_pg):
    page_refs = rest[:n_pg]
    wv_ref, o_ref, qb_ref, m_ref, l_ref, acc_ref = rest[n_pg:]
    del pt_ref
    b = pl.program_id(0)
    j = pl.program_id(1)
    scale = (MLA_NOPE + MLA_ROPE) ** -0.5
    page = page_refs[0].shape[1]

    @pl.when(j == 0)
    def _():
        qt = qt_ref[0]
        for h in range(HEADS):
            qb_ref[h] = jnp.broadcast_to(qt[:, h:h + 1], (MLA_CACHE, LANES))
        m_ref[...] = jnp.full_like(m_ref, NEG)
        l_ref[...] = jnp.zeros_like(l_ref)
        acc_ref[...] = jnp.zeros_like(acc_ref)

    scores = [[] for _ in range(HEADS)]
    for h in range(HEADS):
        qb = qb_ref[h]
        for r in page_refs:
            scores[h].append(jnp.sum(r[...] * qb, axis=0, keepdims=True))
    s = jnp.concatenate([jnp.concatenate(scores[h], axis=1) for h in range(HEADS)], axis=0) * scale
    m_old = m_ref[0:HEADS, :]
    m_new = jnp.maximum(m_old, jnp.max(s, axis=-1, keepdims=True))
    alpha = jnp.exp(m_old - m_new)
    p = jnp.exp(s - m_new[:, 0:1])
    l_ref[0:HEADS, :] = alpha * l_ref[0:HEADS, :] + jnp.sum(p, axis=-1, keepdims=True)
    m_ref[0:HEADS, :] = m_new
    for h in range(HEADS):
        acc = alpha[h:h + 1, 0:1] * acc_ref[h]
        for t, r in enumerate(page_refs):
            acc = acc + r[0:MLA_KV_LORA, :] * p[h:h + 1, t * page:(t + 1) * page]
        acc_ref[h] = acc

    @pl.when(j == pl.num_programs(1) - 1)
    def _():
        new = new_ref[0]
        s_new = jnp.sum(q_ref[0] * new, axis=-1, keepdims=True) * scale
        ones = jnp.ones((SUBLANES, LANES), F32)
        outs = []
        for h in range(HEADS):
            m_old = m_ref[h:h + 1, :]
            m_fin = jnp.maximum(m_old, s_new[h:h + 1, :])
            alpha = jnp.exp(m_old - m_fin)
            p_n = jnp.exp(s_new[h:h + 1, :] - m_fin)
            l = alpha * l_ref[h:h + 1, :] + p_n
            lat = _dot_nt_f32(ones, acc_ref[h])[0:1, :]
            outs.append((alpha * lat + p_n * new[:, 0:MLA_KV_LORA]) / l)
        o_cat = jnp.concatenate(outs, axis=1)
        o = _dot(jnp.broadcast_to(o_cat, (SUBLANES, HEADS * MLA_KV_LORA)), wv_ref[...])
        o_ref[pl.ds(b, 1), :] = o[0:1, :]


def _mla_decode(layer, page_table, q8, qt, new_rows, cache_t, wv):
    BS, npages = page_table.shape
    page = cache_t.shape[3]
    n_pg = min(PAGES_PER_STEP, npages)
    assert npages % n_pg == 0 and page == LANES
    steps = npages // n_pg

    def page_spec(i):
        return pl.BlockSpec((None, None, MLA_CACHE, page),
                            lambda b, j, pt: (layer, pt[b, j * n_pg + i], 0, 0))

    grid_spec = pltpu.PrefetchScalarGridSpec(
        num_scalar_prefetch=1,
        grid=(BS, steps),
        in_specs=[pl.BlockSpec((1, SUBLANES, MLA_CACHE), lambda b, j, pt: (b, 0, 0)),
                  pl.BlockSpec((1, MLA_CACHE, HEADS), lambda b, j, pt: (b, 0, 0)),
                  pl.BlockSpec((1, 1, MLA_CACHE), lambda b, j, pt: (b, 0, 0))]
                 + [page_spec(i) for i in range(n_pg)]
                 + [pl.BlockSpec(wv.shape, lambda b, j, pt: (0, 0))],
        out_specs=pl.BlockSpec((BS, BRANCH), lambda b, j, pt: (0, 0)),
        scratch_shapes=[pltpu.VMEM((HEADS, MLA_CACHE, LANES), F32),
                        pltpu.VMEM((SUBLANES, LANES), F32), pltpu.VMEM((SUBLANES, LANES), F32),
                        pltpu.VMEM((HEADS, MLA_KV_LORA, LANES), F32)],
    )
    return pl.pallas_call(
        functools.partial(_mla_decode_kernel, n_pg=n_pg),
        grid_spec=grid_spec,
        out_shape=jax.ShapeDtypeStruct((BS, BRANCH), F32),
        compiler_params=_cparams(("arbitrary", "arbitrary")),
        name="mla_decode",
    )(page_table, q8, qt, new_rows, *([cache_t] * n_pg), wv)


def _nsa_cmp_decode_kernel(pt_ref, qt_ref, *rest, n_pg, past):
    page_refs = rest[:n_pg]
    oc_ref, imp_ref, qb_ref, ts_ref, vb_ref = rest[n_pg:]
    del pt_ref
    j = pl.program_id(1)
    steps = ts_ref.shape[0]
    page = page_refs[0].shape[1]
    nblk = 2 * LANES

    @pl.when(j == 0)
    def _():
        qt = qt_ref[0]
        for h in range(HEADS):
            qb_ref[h] = jnp.broadcast_to(qt[:, h:h + 1], (DK, LANES))

    for t, r in enumerate(page_refs):
        kt = r[0:DK, :]
        for h in range(HEADS):
            ts_ref[j, h:h + 1, t * page:(t + 1) * page] = jnp.sum(kt * qb_ref[h], axis=0, keepdims=True)
        vb_ref[j, :, t * page:(t + 1) * page] = r[DK:2 * DK, :]

    @pl.when(j == steps - 1)
    def _():
        blk = lax.broadcasted_iota(I32, (HEADS, nblk), 1)
        first = lax.broadcasted_iota(I32, (HEADS, page), 1) < NSA_BLOCK
        s = jnp.zeros((HEADS, nblk), F32)
        for jj in range(steps):
            ts = ts_ref[jj, 0:HEADS, :]
            for t in range(n_pg):
                x = ts[:, t * page:(t + 1) * page]
                b0 = 2 * (jj * n_pg + t)
                s0 = jnp.sum(jnp.where(first, x, 0.0), axis=1, keepdims=True)
                s1 = jnp.sum(jnp.where(first, 0.0, x), axis=1, keepdims=True)
                s = jnp.where(blk == b0, s0, jnp.where(blk == b0 + 1, s1, s))
        s = s * (1.0 / NSA_BLOCK)
        cmask = (blk + 1) * NSA_BLOCK - 1 <= past
        p = _masked_softmax(s, cmask)
        cur = past // NSA_BLOCK
        imp = jnp.sum(p, axis=0, keepdims=True)
        blk1 = blk[0:1]
        forced = (blk1 == 0) | (blk1 == cur - 1)
        imp = imp + jnp.where(forced, NSA_FORCED_BONUS, 0.0)
        imp_ref[0] = jnp.where(blk1 >= cur, -3.0, imp)

        pw = jnp.concatenate([p * (1.0 / NSA_BLOCK), jnp.zeros((SUBLANES - HEADS, nblk), F32)], axis=0)
        nb_step = 2 * n_pg
        expand = jnp.where(lax.broadcasted_iota(I32, (nb_step, n_pg * page), 0)
                           == lax.broadcasted_iota(I32, (nb_step, n_pg * page), 1) // NSA_BLOCK, 1.0, 0.0)
        accs = [jnp.zeros((DK, page), F32) for _ in range(HEADS)]
        for jj in range(steps):
            w = _dot_f32(pw[:, nb_step * jj:nb_step * (jj + 1)], expand)
            for t in range(n_pg):
                v = vb_ref[jj, :, t * page:(t + 1) * page]
                for h in range(HEADS):
                    accs[h] = accs[h] + v * w[h:h + 1, t * page:(t + 1) * page]
        ones = jnp.ones((SUBLANES, LANES), F32)
        oc_ref[0] = jnp.concatenate([_dot_nt_f32(ones, a)[0:1, :] for a in accs], axis=0)


def _nsa_cmp_decode(layer, page_table, qt, cache_t, past):
    BS, npages = page_table.shape
    page = cache_t.shape[3]
    n_pg = min(PAGES_PER_STEP, npages)
    assert page == 2 * NSA_BLOCK and page == LANES and 2 * npages <= 2 * LANES and npages % n_pg == 0
    steps = npages // n_pg

    def page_spec(i):
        return pl.BlockSpec((None, None, 2 * DK, page),
                            lambda b, j, pt: (layer, pt[b, j * n_pg + i], 0, 0))

    grid_spec = pltpu.PrefetchScalarGridSpec(
        num_scalar_prefetch=1,
        grid=(BS, steps),
        in_specs=[pl.BlockSpec((1, DK, HEADS), lambda b, j, pt: (b, 0, 0))]
                 + [page_spec(i) for i in range(n_pg)],
        out_specs=[pl.BlockSpec((1, HEADS, DK), lambda b, j, pt: (b, 0, 0)),
                   pl.BlockSpec((1, 1, 2 * LANES), lambda b, j, pt: (b, 0, 0))],
        scratch_shapes=[pltpu.VMEM((HEADS, DK, LANES), F32),
                        pltpu.VMEM((steps, SUBLANES, n_pg * page), F32),
                        pltpu.VMEM((steps, DK, n_pg * page), F32)],
    )
    return pl.pallas_call(
        functools.partial(_nsa_cmp_decode_kernel, n_pg=n_pg, past=past),
        grid_spec=grid_spec,
        out_shape=[jax.ShapeDtypeStruct((BS, HEADS, DK), F32),
                   jax.ShapeDtypeStruct((BS, 1, 2 * LANES), F32)],
        compiler_params=_cparams(("parallel", "arbitrary")),
        name="nsa_cmp_decode",
    )(page_table, qt, *([cache_t] * n_pg))


def _topk_kernel(imp_ref, idx_ref, *, past, k_top):
    BS = imp_ref.shape[0]
    cur = past // NSA_BLOCK
    blk = lax.broadcasted_iota(I32, (BS, 2 * LANES), 1).astype(F32)
    lane_x = lax.broadcasted_iota(I32, (BS, LANES), 1)
    imp = jnp.concatenate([imp_ref[...], jnp.where(lane_x == 0, NSA_FORCED_BONUS, -3.0)], axis=1)
    blk = jnp.concatenate([blk, (cur + lane_x).astype(F32)], axis=1)
    out = jnp.zeros((BS, LANES), F32)
    for k in range(k_top):
        m = jnp.max(imp, axis=-1, keepdims=True)
        pick = jnp.min(jnp.where(imp == m, blk, 1e9), axis=-1, keepdims=True)
        imp = jnp.where(blk == pick, -4.0, imp)
        out = jnp.where(lane_x == k, pick, out)
    idx_ref[...] = out.astype(I32)


def _topk(imp, past, k_top):
    BS = imp.shape[0]
    return pl.pallas_call(
        functools.partial(_topk_kernel, past=past, k_top=k_top),
        out_shape=jax.ShapeDtypeStruct((BS, LANES), I32),
        name="nsa_topk",
    )(imp)


def _nsa_sel_decode_kernel(pt_ref, idx_ref, q_ref, pgn_ref, wn_ref, win_ref, g_ref, oc_ref, *rest,
                           k_top, n_blocks):
    blk_refs = rest[:k_top]
    (o_ref,) = rest[k_top:]
    del pt_ref
    b = pl.program_id(0)
    q = q_ref[0]
    page = blk_refs[0].shape[1]

    def attend(kt, vt, s_mask, k_new, v_new, new_on):
        s = _dot(q, kt)
        if s_mask is not None:
            s = jnp.where(s_mask, s, NEG)
        s_n = jnp.where(new_on, jnp.sum(q * k_new, axis=-1, keepdims=True), NEG)
        m = jnp.maximum(jnp.max(s, axis=-1, keepdims=True), s_n)
        e = jnp.exp(s - m)
        if s_mask is not None:
            e = jnp.where(s_mask, e, 0.0)
        e_n = jnp.where(new_on, jnp.exp(s_n - m), 0.0)
        l = jnp.sum(e, axis=-1, keepdims=True) + e_n
        return (_dot_nt(e, vt) + e_n * v_new) / l

    kt = jnp.concatenate([r[0:DK, :] for r in blk_refs], axis=1).astype(BF16)
    vt = jnp.concatenate([r[DK:2 * DK, :] for r in blk_refs], axis=1).astype(BF16)
    lane = lax.broadcasted_iota(I32, (1, k_top * page), 1)
    slot = lane // page
    half = (lane % page) // NSA_BLOCK
    valid = jnp.zeros((1, k_top * page), I32)
    has_new = jnp.zeros((), jnp.bool_)
    for k in range(k_top):
        ik = idx_ref[b, k]
        ok = jnp.where(half == (ik & 1), (ik < n_blocks).astype(I32), 0)
        valid = jnp.where(slot == k, ok, valid)
        has_new = has_new | (ik == n_blocks)
    pgn = pgn_ref[0]
    o_s = attend(kt, vt, valid > 0, pgn[:, 2 * DK:3 * DK], pgn[:, 3 * DK:4 * DK], has_new)
    wn = wn_ref[0]
    o_w = attend(win_ref[0:DK, :].astype(BF16), win_ref[DK:2 * DK, :].astype(BF16), None,
                 wn[:, 0:DK], wn[:, DK:2 * DK], True)
    o_c = oc_ref[0]
    g = g_ref[0]
    outs = []
    for h in range(HEADS):
        outs.append(g[:, 3 * h:3 * h + 1] * o_c[h:h + 1] + g[:, 3 * h + 1:3 * h + 2] * o_s[h:h + 1]
                    + g[:, 3 * h + 2:3 * h + 3] * o_w[h:h + 1])
    o_ref[pl.ds(b, 1), :] = jnp.concatenate(outs, axis=1)


def _nsa_sel_decode(layer, page_table, idx, q8, pg_new, w_new, win_t, g_new, o_c, cache_t, k_top):
    BS, npages = page_table.shape
    n_blocks = 2 * npages
    page = cache_t.shape[3]
    nwin = win_t.shape[3]
    blk = jnp.minimum(idx[:, :k_top], n_blocks - 1)
    phys = jnp.take_along_axis(page_table, blk // 2, axis=1)

    def blk_spec(k):
        return pl.BlockSpec((None, None, 2 * DK, page), lambda b, ph, ix: (layer, ph[b, k], 1, 0))

    row3 = lambda w: pl.BlockSpec((1, 1, w), lambda b, pt, ix: (b, 0, 0))
    grid_spec = pltpu.PrefetchScalarGridSpec(
        num_scalar_prefetch=2,
        grid=(BS,),
        in_specs=[pl.BlockSpec((1, SUBLANES, DK), lambda b, pt, ix: (b, 0, 0)),
                  row3(256), row3(LANES),
                  pl.BlockSpec((None, None, 2 * DK, nwin), lambda b, pt, ix: (layer, b, 0, 0)),
                  row3(LANES),
                  pl.BlockSpec((1, HEADS, DK), lambda b, pt, ix: (b, 0, 0))]
                 + [blk_spec(k) for k in range(k_top)],
        out_specs=pl.BlockSpec((BS, BRANCH), lambda b, pt, ix: (0, 0)),
    )
    return pl.pallas_call(
        functools.partial(_nsa_sel_decode_kernel, k_top=k_top, n_blocks=n_blocks),
        grid_spec=grid_spec,
        out_shape=jax.ShapeDtypeStruct((BS, BRANCH), F32),
        compiler_params=_cparams(("arbitrary",)),
        name="nsa_sel_decode",
    )(phys, idx, q8, pg_new, w_new, win_t, g_new, o_c, *([cache_t] * k_top))


def _column(row, eye):
    return jnp.sum(jnp.where(eye, jnp.broadcast_to(row, eye.shape), 0.0), axis=1, keepdims=True)


def _rec_decode_kernel(xbc_ref, cbuf_ref, dt_ref, z_ref, hg_ref, s_ref, hs_ref,
                       cw_ref, cb_ref, dtb_ref, alog_ref, dsk_ref, sn_ref, lb_ref, hn_ref,
                       os_ref, oh_ref, sn_out, hs_out):
    b = pl.program_id(0)
    eye = lax.broadcasted_iota(I32, (DK, DK), 0) == lax.broadcasted_iota(I32, (DK, DK), 1)

    cbuf = cbuf_ref[...]
    xn = xbc_ref[0]
    conv = cb_ref[...] + cw_ref[SSD_CONV - 1:SSD_CONV, :] * xn
    for j in range(SSD_CONV - 1):
        conv = conv + cw_ref[j:j + 1, :] * cbuf[j:j + 1, :]
    xa = _silu(conv)
    dt = _softplus(dt_ref[0] + dtb_ref[...])
    ea = jnp.exp(dt * (-jnp.exp(alog_ref[...])))
    dsk = dsk_ref[...]
    ys = []
    for h in range(HEADS):
        g = h // (HEADS // SSD_GROUPS)
        xs = xa[:, DK * h:DK * (h + 1)]
        bm = xa[:, BRANCH + DK * g:BRANCH + DK * (g + 1)]
        cm = xa[:, BRANCH + DK * (SSD_GROUPS + g):BRANCH + DK * (SSD_GROUPS + g + 1)]
        xdt_col = _column(xs * dt[:, h:h + 1], eye)
        s_old = s_ref[h]
        ea_h = ea[:, h:h + 1]
        y_col = (jnp.sum(cm * bm, axis=-1, keepdims=True) * xdt_col
                 + jnp.sum(s_old * cm, axis=-1, keepdims=True) * ea_h)
        sn_out[0, h] = s_old * ea_h + xdt_col * bm
        y_row = jnp.sum(jnp.where(eye, jnp.broadcast_to(y_col, (DK, DK)), 0.0), axis=0, keepdims=True)
        ys.append(y_row + dsk[:, h:h + 1] * xs)
    y = jnp.concatenate(ys, axis=1)
    os_ref[pl.ds(b, 1), :] = _rms(y * _silu(z_ref[0]), sn_ref[...])

    W = BRANCH
    hg = hg_ref[0]
    q = hg[:, 0:W]
    v = hg[:, 2 * W:3 * W]
    log_f, kin = _hgrn_gates(hg[:, W:2 * W], lb_ref[...])
    ef = jnp.exp(log_f)
    hn = hn_ref[...]
    outs = []
    for h in range(HEADS):
        r = slice(DK * h, DK * (h + 1))
        s_old = hs_ref[h]
        att = jnp.sum(q[:, r] * kin[:, r], axis=-1, keepdims=True)
        y = att * v[:, r] + jnp.sum(s_old * _column(q[:, r] * ef[:, r], eye), axis=0, keepdims=True)
        hs_out[0, h] = s_old * _column(ef[:, r], eye) + _column(kin[:, r], eye) * v[:, r]
        outs.append(_rms(y, hn[:, r]))
    o = jnp.concatenate(outs, axis=1)
    oh_ref[pl.ds(b, 1), :] = o * _silu(hg[:, 3 * W:4 * W])


def _rec_decode(layer, xbc_new, conv_state, dt_new, z_new, hg_new, state_ssd, state_hgrn,
                cw, cb, dtb, alog, dsk, sn, lb, hn):
    BS = xbc_new.shape[0]
    row3 = lambda w: pl.BlockSpec((1, 1, w), lambda b: (b, 0, 0))
    st_in = pl.BlockSpec((None, None, HEADS, DK, DK), lambda b: (layer, b, 0, 0, 0))
    st_out = pl.BlockSpec((1, HEADS, DK, DK), lambda b: (b, 0, 0, 0))
    o_spec = pl.BlockSpec((BS, BRANCH), lambda b: (0, 0))
    consts = (cw, cb, dtb, alog, dsk, sn, lb, hn)
    return pl.pallas_call(
        _rec_decode_kernel,
        grid=(BS,),
        in_specs=[row3(SSD_CONV_DIM),
                  pl.BlockSpec((None, None, SSD_CONV - 1, SSD_CONV_DIM), lambda b: (layer, b, 0, 0)),
                  row3(LANES), row3(BRANCH), row3(4 * BRANCH), st_in, st_in]
                 + [_const_spec(c.shape) for c in consts],
        out_specs=[o_spec, o_spec, st_out, st_out],
        out_shape=[jax.ShapeDtypeStruct((BS, BRANCH), F32), jax.ShapeDtypeStruct((BS, BRANCH), F32),
                   jax.ShapeDtypeStruct((BS, HEADS, DK, DK), F32),
                   jax.ShapeDtypeStruct((BS, HEADS, DK, DK), F32)],
        compiler_params=_cparams(("arbitrary",)),
        name="rec_decode",
    )(xbc_new, conv_state, dt_new, z_new, hg_new, state_ssd, state_hgrn, *consts)


def _rot_cols(w):
    half = MLA_ROPE // 2
    return jnp.concatenate([-w[..., half:], w[..., :half]], axis=-1)


def _pad_cols(w, width):
    return jnp.pad(w, [(0, 0)] * (w.ndim - 1) + [(0, width - w.shape[-1])])


def _pack_w_in(w):
    sizes = (256, 384, 12, MLA_Q_LORA, MLA_KV_LORA, MLA_ROPE, 256, SSD_CONV_DIM, HEADS, 256, 256, 256, 256)
    offs = np.cumsum((0,) + sizes)
    (nsa_q, nsa_kv, nsa_g, cq, ckv, kpe, z, xbc, dt, hq, hf, hi, hgate) = (
        w[:, offs[i]:offs[i + 1]] for i in range(len(sizes)))
    gate = w[:, offs[-1]:]
    packed = jnp.concatenate([
        nsa_q, nsa_kv[:, :256], nsa_kv[:, 256:], _pad_cols(nsa_g, LANES), cq, ckv,
        _pad_cols(kpe, LANES), _pad_cols(_rot_cols(kpe), LANES), _pad_cols(dt, LANES), z, xbc,
        hq, hf, hi, hgate], axis=1)
    assert packed.shape[1] == PACKED_WIDTH
    return packed.astype(BF16), gate.astype(BF16)


def _pack_mla(wuq, wukv):
    nope = wuq[:, :, :MLA_NOPE].reshape(MLA_Q_LORA, HEADS * MLA_NOPE)
    pe = wuq[:, :, MLA_NOPE:]
    pe_p = _pad_cols(pe, LANES).reshape(MLA_Q_LORA, HEADS * LANES)
    pr_p = _pad_cols(_rot_cols(pe), LANES).reshape(MLA_Q_LORA, HEADS * LANES)
    wuq_p = jnp.concatenate([nope, pe_p, pr_p], axis=1).astype(BF16)
    eye = jnp.eye(HEADS, dtype=wukv.dtype)
    wukn = jnp.einsum('rhn,hg->hngr', wukv[:, :, :MLA_NOPE], eye).reshape(HEADS * MLA_NOPE, HEADS * MLA_KV_LORA)
    wv = jnp.einsum('rhv,hg->hrgv', wukv[:, :, MLA_NOPE:], eye).reshape(HEADS * MLA_KV_LORA, HEADS * MLA_V)
    return wuq_p, wukn.astype(BF16), wv.astype(BF16)


def _rope_table(positions):
    half = MLA_ROPE // 2
    freq = ROPE_THETA ** (-jnp.arange(half, dtype=F32) / half)
    ang = positions.astype(F32)[:, None] * freq
    cos = jnp.concatenate([jnp.cos(ang), jnp.cos(ang)], axis=1)
    sin = jnp.concatenate([jnp.sin(ang), jnp.sin(ang)], axis=1)
    return jnp.concatenate([_pad_cols(cos, LANES), _pad_cols(sin, LANES)], axis=1)


def _pad_row(v, width=LANES):
    return _pad_cols(v.reshape(1, -1).astype(F32), width)


def kernel(x_prompt, x_sample, cache_nsa_kv, cache_mla, cache_nsa_win, state_ssd, state_ssd_conv, state_hgrn, page_table, norm_g, ffn_w_in, ffn_w_out, w_in, mla_q_norm, mla_kv_norm, mla_w_uq, mla_w_ukv, ssd_conv_w, ssd_conv_b, ssd_dt_bias, ssd_a_log, ssd_d, ssd_norm, hg_lb_logits, hg_norm, w_branch, w_out):
    BP, T, D = x_prompt.shape
    BS = x_sample.shape[0]
    depth = w_in.shape[0]
    n_pool, page = cache_nsa_kv.shape[1], cache_nsa_kv.shape[2]
    npages = page_table.shape[1]
    past = npages * page
    n_win = cache_nsa_win.shape[2]
    rows_p = BP * T
    assert D == D_MODEL and x_sample.shape[1] == 1 and BS % SUBLANES == 0
    assert n_win == min(NSA_WINDOW, past) and past % NSA_BLOCK == 0
    k_top = min(NSA_TOPK, past // NSA_BLOCK + 1)

    xp = x_prompt.reshape(rows_p, D)
    xs = x_sample.reshape(BS, D)
    cs_p = _rope_table(jnp.tile(jnp.arange(T, dtype=I32), BP))
    cs_s = _rope_table(jnp.full((BS,), past, I32))
    lbp = jax.nn.softmax(hg_lb_logits.astype(F32), axis=0)
    lower_bounds = jnp.cumsum(lbp, axis=0) - lbp[0]

    nsa_t = jnp.transpose(cache_nsa_kv, (0, 1, 3, 4, 2)).reshape(depth, n_pool, 4 * DK, page)
    mla_t = jnp.transpose(cache_mla, (0, 1, 3, 2))
    win_t = jnp.transpose(cache_nsa_win, (0, 1, 3, 4, 2)).reshape(depth, BS, 2 * DK, n_win)

    def row3(a):
        return a.reshape(BS, 1, a.shape[1])

    outs = [[] for _ in range(12)]
    for l in range(depth):
        g = norm_g[l].astype(F32)
        grow = lambda i: g[i].reshape(1, D)
        wi = ffn_w_in[l].astype(BF16)
        wo = ffn_w_out[l].astype(BF16)
        w_pack, w_gate = _pack_w_in(w_in[l])
        wuq_p, wukn, wv = _pack_mla(mla_w_uq[l], mla_w_ukv[l])
        mqn = mla_q_norm[l].reshape(1, -1)
        mkvn = mla_kv_norm[l].reshape(1, -1)

        xp = _ffn(xp, grow(0), grow(1), wi[0], wo[0])
        xs = _ffn(xs, grow(0), grow(1), wi[0], wo[0])
        (nsa_q, nsa_pg, nsa_w, nsa_g, qlat, qpe, mla_new, ssd_z, ssd_xbc, ssd_dt, hg) = _inproj(
            xp, grow(2), cs_p, w_pack, mqn, mkvn, wuq_p, wukn)
        (nsa_q_s, nsa_pg_s, nsa_w_s, nsa_g_s, qlat_s, qpe_s, mla_new_s, ssd_z_s, ssd_xbc_s, ssd_dt_s,
         hg_s) = _inproj(xs, grow(2), cs_s, w_pack, mqn, mkvn, wuq_p, wukn)

        o_nsa = _nsa_prompt(nsa_q, nsa_pg, nsa_w, nsa_g, BP, T)
        pad8 = lambda a: jnp.pad(a, ((0, 0), (0, SUBLANES - HEADS), (0, 0)))
        q4 = nsa_q_s.reshape(BS, HEADS, DK)
        q8 = pad8(q4)
        o_c, imp = _nsa_cmp_decode(l, page_table, jnp.transpose(q4, (0, 2, 1)), nsa_t, past)
        idx = _topk(imp.reshape(BS, 2 * LANES), past, k_top)
        o_nsa_s = _nsa_sel_decode(l, page_table, idx, q8, row3(nsa_pg_s), row3(nsa_w_s), win_t,
                                  row3(nsa_g_s), o_c, nsa_t, k_top)

        o_mla = _mla_prompt(qlat, qpe, mla_new, wv, BP, T)
        q_cat = jnp.concatenate([qlat_s.reshape(BS, HEADS, LANES),
                                 qpe_s.reshape(BS, HEADS, LANES)[:, :, :MLA_ROPE]], axis=-1)
        o_mla_s = _mla_decode(l, page_table, pad8(q_cat), jnp.transpose(q_cat, (0, 2, 1)),
                              row3(mla_new_s), mla_t, wv)

        cw = ssd_conv_w[l].astype(F32)
        cb = ssd_conv_b[l].reshape(1, -1).astype(F32)
        dtb = _pad_row(ssd_dt_bias[l])
        alog = _pad_row(ssd_a_log[l])
        dsk = _pad_row(ssd_d[l])
        sn = ssd_norm[l].reshape(1, -1).astype(F32)
        lb = lower_bounds[l].reshape(1, -1)
        hn = hg_norm[l].reshape(1, -1).astype(F32)
        o_ssd, ssd_p = _ssd_prompt(ssd_xbc, ssd_dt, ssd_z, cw, cb, dtb, alog, dsk, sn, BP, T)
        o_hg, hgrn_p = _hgrn_prompt(hg, lb, hn, BP, T)
        o_ssd_s, o_hg_s, ssd_s, hgrn_s = _rec_decode(
            l, row3(ssd_xbc_s), state_ssd_conv, row3(ssd_dt_s), row3(ssd_z_s), row3(hg_s), state_ssd,
            state_hgrn, cw, cb, dtb, alog, dsk, sn, lb, hn)

        wbr = w_branch[l].astype(BF16)
        wout = w_out[l].astype(BF16)
        xp = _merge(xp, (o_nsa, o_mla, o_ssd, o_hg), grow(2), grow(3), w_gate, wbr, wout)
        xs = _merge(xs, (o_nsa_s, o_mla_s, o_ssd_s, o_hg_s), grow(2), grow(3), w_gate, wbr, wout)
        xp = _ffn(xp, grow(4), grow(5), wi[1], wo[1])
        xs = _ffn(xs, grow(4), grow(5), wi[1], wo[1])

        n_keep = min(NSA_WINDOW, T)
        layer_out = (
            nsa_pg.reshape(BP, T, 4, DK), nsa_pg_s.reshape(BS, 1, 4, DK),
            mla_new.reshape(BP, T, MLA_CACHE), mla_new_s.reshape(BS, 1, MLA_CACHE),
            nsa_w.reshape(BP, T, 2, DK)[:, T - n_keep:],
            jnp.concatenate([cache_nsa_win[l], nsa_w_s.reshape(BS, 1, 2, DK)], axis=1)[:, 1:],
            ssd_p, ssd_s,
            ssd_xbc.reshape(BP, T, SSD_CONV_DIM)[:, T - (SSD_CONV - 1):],
            jnp.concatenate([state_ssd_conv[l], ssd_xbc_s.reshape(BS, 1, SSD_CONV_DIM)], axis=1)[:, 1:],
            hgrn_p, hgrn_s)
        for acc, o in zip(outs, layer_out):
            acc.append(o)

    stacked = tuple(jnp.stack(o) for o in outs)
    return (xp.reshape(BP, T, D), xs.reshape(BS, 1, D)) + stacked
```
